```python
import jax
import jax.numpy as jnp
from jax import lax
import numpy as np

D_MODEL = 1024
BATCH = 32
SEQ = 2048
DEPTH = 1

MLA_HEADS = 8
MLA_NOPE_DIM = 64
MLA_ROPE_DIM = 32
MLA_QK_DIM = MLA_NOPE_DIM + MLA_ROPE_DIM
MLA_V_DIM = 64
MLA_Q_LORA = 256
MLA_KV_LORA = 128
MLA_WIDTH = MLA_HEADS * MLA_V_DIM
MOBA_HEADS = 8
MOBA_HEAD_DIM = 64
MOBA_BLOCK = 256
MOBA_TOPK = 3
MOBA_WIDTH = MOBA_HEADS * MOBA_HEAD_DIM
MIX_WIDTH = MLA_WIDTH + MOBA_WIDTH
IN_SPLITS = [
    MLA_Q_LORA,
    MLA_Q_LORA + MLA_KV_LORA,
    MLA_Q_LORA + MLA_KV_LORA + MLA_ROPE_DIM,
    MLA_Q_LORA + MLA_KV_LORA + MLA_ROPE_DIM + MOBA_WIDTH,
    MLA_Q_LORA + MLA_KV_LORA + MLA_ROPE_DIM + 2 * MOBA_WIDTH,
]
IN_WIDTH = MLA_Q_LORA + MLA_KV_LORA + MLA_ROPE_DIM + 3 * MOBA_WIDTH
ROPE_THETA = 10000.0
NORM_EPS = 1e-6
ATTN_Q_BLOCK = 128
MOBA_Q_CHUNK = 16
N_GROUPS = 4
EXPERTS_PER_GROUP = 8
N_EXPERTS = N_GROUPS * EXPERTS_PER_GROUP
EXPERT_TOPK = 2
EXPERT_HIDDEN = 256
MOE_ROW_BLOCK = 256

kernel_name = 'hymba_style_mla_moba_hiermoe_block'


def rms_norm(x, g):
    xf = x.astype(jnp.float32)
    y = xf * lax.rsqrt(jnp.mean(xf * xf, axis=-1, keepdims=True) + NORM_EPS)
    return (y * g.astype(jnp.float32)).astype(x.dtype)


def rope_tables(seq, dim):
    inv_freq = ROPE_THETA ** (-(jnp.arange(0, dim, 2, dtype=jnp.float32) / dim))
    ang = jnp.arange(seq, dtype=jnp.float32)[:, None] * inv_freq[None, :]
    return jnp.cos(ang), jnp.sin(ang)


def apply_rope(x, cos, sin):
    x1, x2 = jnp.split(x, 2, axis=-1)
    c = cos[None, :, None, :].astype(x.dtype)
    s = sin[None, :, None, :].astype(x.dtype)
    return jnp.concatenate([x1 * c - x2 * s, x2 * c + x1 * s], axis=-1)


def causal_attention_blocked(q, k, v, scale):
    B, S, H, dq = q.shape
    n_qb = S // ATTN_Q_BLOCK
    q_blocks = q.reshape(B, n_qb, ATTN_Q_BLOCK, H, dq).transpose(1, 0, 2, 3, 4)
    k_pos = jnp.arange(S)

    def one_block(args):
        qi, bi = args
        s = jnp.einsum('bqhd,bkhd->bhqk', qi, k, preferred_element_type=jnp.float32) * scale
        q_pos = bi * ATTN_Q_BLOCK + jnp.arange(ATTN_Q_BLOCK)
        s = jnp.where(k_pos[None, :] <= q_pos[:, None], s, -jnp.inf)
        p = jax.nn.softmax(s, axis=-1).astype(v.dtype)
        return jnp.einsum('bhqk,bkhd->bqhd', p, v)

    out = lax.map(one_block, (q_blocks, jnp.arange(n_qb)))
    return out.transpose(1, 0, 2, 3, 4).reshape(B, S, H, v.shape[-1])


def moba_attention(q, k, v):
    B, S, H, d = q.shape
    nb = -(-S // MOBA_BLOCK)
    pad = nb * MOBA_BLOCK - S
    kp = jnp.pad(k, ((0, 0), (0, pad), (0, 0), (0, 0)))
    vp = jnp.pad(v, ((0, 0), (0, pad), (0, 0), (0, 0)))
    k_blocks = kp.reshape(B, nb, MOBA_BLOCK, H, d).transpose(0, 3, 1, 2, 4)
    v_blocks = vp.reshape(B, nb, MOBA_BLOCK, H, d).transpose(0, 3, 1, 2, 4)
    k_mean = jnp.mean(k_blocks.astype(jnp.float32), axis=3).astype(k.dtype)
    top_k = min(MOBA_TOPK, nb)
    n_chunks = S // MOBA_Q_CHUNK
    q_chunks = q.reshape(B, n_chunks, MOBA_Q_CHUNK, H, d).transpose(1, 0, 2, 3, 4)
    scale = d ** -0.5
    b_idx = jnp.arange(B)[:, None, None, None]
    h_idx = jnp.arange(H)[None, :, None, None]
    neg = jnp.finfo(jnp.float32).min
    blk_pos = jnp.arange(MOBA_BLOCK)

    def one_chunk(args):
        qi, ci = args
        q_start = ci * MOBA_Q_CHUNK
        q_pos = q_start + jnp.arange(MOBA_Q_CHUNK)
        cur = q_start // MOBA_BLOCK
        gate = jnp.einsum('bqhd,bhnd->bhqn', qi, k_mean, preferred_element_type=jnp.float32)
        gate = jnp.where(jnp.arange(nb) < cur, gate, neg)
        _, sel = lax.top_k(gate, top_k)
        slot_ok = jnp.arange(top_k) < cur
        k_sel = k_blocks[b_idx, h_idx, sel]
        v_sel = v_blocks[b_idx, h_idx, sel]
        s_sel = jnp.einsum('bqhd,bhqjkd->bhqjk', qi, k_sel, preferred_element_type=jnp.float32) * scale
        s_sel = jnp.where(slot_ok[:, None], s_sel, -jnp.inf)
        s_sel = s_sel.reshape(B, H, MOBA_Q_CHUNK, top_k * MOBA_BLOCK)
        k_own = lax.dynamic_index_in_dim(k_blocks, cur, axis=2, keepdims=False)
        v_own = lax.dynamic_index_in_dim(v_blocks, cur, axis=2, keepdims=False)
        s_own = jnp.einsum('bqhd,bhkd->bhqk', qi, k_own, preferred_element_type=jnp.float32) * scale
        k_pos = cur * MOBA_BLOCK + blk_pos
        s_own = jnp.where(k_pos[None, :] <= q_pos[:, None], s_own, -jnp.inf)
        p = jax.nn.softmax(jnp.concatenate([s_sel, s_own], axis=-1), axis=-1).astype(v.dtype)
        p_sel = p[..., : top_k * MOBA_BLOCK].reshape(B, H, MOBA_Q_CHUNK, top_k, MOBA_BLOCK)
        p_own = p[..., top_k * MOBA_BLOCK:]
        return (jnp.einsum('bhqjk,bhqjkd->bqhd', p_sel, v_sel)
                + jnp.einsum('bhqk,bhkd->bqhd', p_own, v_own))

    out = lax.map(one_chunk, (q_chunks, jnp.arange(n_chunks)))
    return out.transpose(1, 0, 2, 3, 4).reshape(B, S, H, d)


def hierarchical_moe(h, w_rg, b_rg, w_re, b_re, w_gate, w_up, w_down):
    T, D = h.shape
    g_logits = (h @ w_rg).astype(jnp.float32) + b_rg.astype(jnp.float32)
    g_prob = jax.nn.softmax(g_logits, axis=-1)
    _, g_top = lax.top_k(g_logits, 1)
    g_sel = g_top[:, 0]
    g_w = jnp.take_along_axis(g_prob, g_top, axis=-1)[:, 0]
    e_logits = ((h @ w_re).astype(jnp.float32) + b_re.astype(jnp.float32)).reshape(T, N_GROUPS, EXPERTS_PER_GROUP)
    e_in_group = jnp.take_along_axis(e_logits, g_sel[:, None, None], axis=1)[:, 0]
    e_vals, e_top = lax.top_k(e_in_group, EXPERT_TOPK)
    e_w = jax.nn.softmax(e_vals, axis=-1)
    weights = g_w[:, None] * e_w
    expert_ids = g_sel[:, None] * EXPERTS_PER_GROUP + e_top

    A = T * EXPERT_TOPK
    e_flat = expert_ids.reshape(A).astype(jnp.int32)
    tok_flat = jnp.repeat(jnp.arange(T, dtype=jnp.int32), EXPERT_TOPK)
    w_flat = weights.reshape(A)
    order = jnp.argsort(e_flat)
    e_sorted = e_flat[order]
    counts = jnp.zeros((N_EXPERTS,), jnp.int32).at[e_flat].add(1)
    padded = (counts + MOE_ROW_BLOCK - 1) // MOE_ROW_BLOCK * MOE_ROW_BLOCK
    pad_end = jnp.cumsum(padded)
    pad_start = pad_end - padded
    start = jnp.cumsum(counts) - counts
    dest = pad_start[e_sorted] + (jnp.arange(A, dtype=jnp.int32) - start[e_sorted])
    n_blocks = -(-A // MOE_ROW_BLOCK) + N_EXPERTS
    P = n_blocks * MOE_ROW_BLOCK
    row_tok = jnp.zeros((P,), jnp.int32).at[dest].set(tok_flat[order])
    row_w = jnp.zeros((P,), h.dtype).at[dest].set(w_flat[order].astype(h.dtype))
    block_expert = jnp.minimum(
        jnp.searchsorted(pad_end, jnp.arange(n_blocks, dtype=jnp.int32) * MOE_ROW_BLOCK, side='right'),
        N_EXPERTS - 1).astype(jnp.int32)

    def one_block(args):
        toks, eid = args
        xb = h[toks]
        a = xb @ w_gate[eid]
        u = xb @ w_up[eid]
        return (jax.nn.silu(a) * u) @ w_down[eid]

    y = lax.map(one_block, (row_tok.reshape(n_blocks, MOE_ROW_BLOCK), block_expert))
    y = y.reshape(P, D) * row_w[:, None]
    return jnp.zeros((T, D), h.dtype).at[row_tok].add(y)


def setup_inputs(seed: int = 0) -> dict:
    key = jax.random.key(seed)
    ks = jax.random.split(key, 32)
    f32 = jnp.float32
    L = DEPTH

    def dense(k, shape, fan_in):
        return jax.random.normal(k, shape, f32) * fan_in ** -0.5

    def gain(k, n):
        return 1.0 + 0.02 * jax.random.normal(k, (L, n), f32)

    return {
        'x': jax.random.normal(ks[0], (BATCH, SEQ, D_MODEL), f32),
        'attn_norm_g': gain(ks[1], D_MODEL),
        'w_in': dense(ks[2], (L, D_MODEL, IN_WIDTH), D_MODEL),
        'mla_q_lat_g': gain(ks[3], MLA_Q_LORA),
        'w_q_up': dense(ks[4], (L, MLA_Q_LORA, MLA_HEADS * MLA_QK_DIM), MLA_Q_LORA),
        'mla_kv_lat_g': gain(ks[5], MLA_KV_LORA),
        'w_kv_up': dense(ks[6], (L, MLA_KV_LORA, MLA_HEADS * (MLA_NOPE_DIM + MLA_V_DIM)), MLA_KV_LORA),
        'mla_q_head_g': gain(ks[7], MLA_QK_DIM),
        'mla_k_head_g': gain(ks[8], MLA_QK_DIM),
        'moba_q_head_g': gain(ks[9], MOBA_HEAD_DIM),
        'moba_k_head_g': gain(ks[10], MOBA_HEAD_DIM),
        'out_norm_mla_g': gain(ks[11], MLA_WIDTH),
        'out_norm_moba_g': gain(ks[12], MOBA_WIDTH),
        'w_out': dense(ks[13], (L, MIX_WIDTH, D_MODEL), MIX_WIDTH),
        'ffn_norm_g': gain(ks[14], D_MODEL),
        'w_router_group': dense(ks[15], (L, D_MODEL, N_GROUPS), D_MODEL),
        'b_router_group': 0.01 * jax.random.normal(ks[16], (L, N_GROUPS), f32),
        'w_router_expert': dense(ks[17], (L, D_MODEL, N_EXPERTS), D_MODEL),
        'b_router_expert': 0.01 * jax.random.normal(ks[18], (L, N_EXPERTS), f32),
        'w_gate': dense(ks[19], (L, N_EXPERTS, D_MODEL, EXPERT_HIDDEN), D_MODEL),
        'w_up': dense(ks[20], (L, N_EXPERTS, D_MODEL, EXPERT_HIDDEN), D_MODEL),
        'w_down': dense(ks[21], (L, N_EXPERTS, EXPERT_HIDDEN, D_MODEL), EXPERT_HIDDEN),
    }


def reference(x, attn_norm_g, w_in, mla_q_lat_g, w_q_up, mla_kv_lat_g, w_kv_up,
              mla_q_head_g, mla_k_head_g, moba_q_head_g, moba_k_head_g,
              out_norm_mla_g, out_norm_moba_g, w_out, ffn_norm_g,
              w_router_group, b_router_group, w_router_expert, b_router_expert,
              w_gate, w_up, w_down):
    B, S, D = x.shape
    cos_r, sin_r = rope_tables(S, MLA_ROPE_DIM)
    cos_f, sin_f = rope_tables(S, MOBA_HEAD_DIM)
    for l in range(DEPTH):
        h = rms_norm(x, attn_norm_g[l])
        proj = h @ w_in[l]
        q_lat, kv_lat, k_pe, mq, mk, mv = jnp.split(proj, IN_SPLITS, axis=-1)

        q_lat = rms_norm(q_lat, mla_q_lat_g[l])
        q_a = (q_lat @ w_q_up[l]).reshape(B, S, MLA_HEADS, MLA_QK_DIM)
        kv_lat = rms_norm(kv_lat, mla_kv_lat_g[l])
        kv = (kv_lat @ w_kv_up[l]).reshape(B, S, MLA_HEADS, MLA_NOPE_DIM + MLA_V_DIM)
        k_nope, v_a = jnp.split(kv, [MLA_NOPE_DIM], axis=-1)
        k_pe_h = jnp.broadcast_to(k_pe[:, :, None, :], (B, S, MLA_HEADS, MLA_ROPE_DIM))
        k_a = jnp.concatenate([k_nope, k_pe_h], axis=-1)
        q_a = rms_norm(q_a, mla_q_head_g[l])
        k_a = rms_norm(k_a, mla_k_head_g[l])
        q_a = jnp.concatenate([q_a[..., :MLA_NOPE_DIM],
                               apply_rope(q_a[..., MLA_NOPE_DIM:], cos_r, sin_r)], axis=-1)
        k_a = jnp.concatenate([k_a[..., :MLA_NOPE_DIM],
                               apply_rope(k_a[..., MLA_NOPE_DIM:], cos_r, sin_r)], axis=-1)
        o_a = causal_attention_blocked(q_a, k_a, v_a, MLA_QK_DIM ** -0.5).reshape(B, S, MLA_WIDTH)

        q_b = rms_norm(mq.reshape(B, S, MOBA_HEADS, MOBA_HEAD_DIM), moba_q_head_g[l])
        k_b = rms_norm(mk.reshape(B, S, MOBA_HEADS, MOBA_HEAD_DIM), moba_k_head_g[l])
        q_b = apply_rope(q_b, cos_f, sin_f)
        k_b = apply_rope(k_b, cos_f, sin_f)
        v_b = mv.reshape(B, S, MOBA_HEADS, MOBA_HEAD_DIM)
        o_b = moba_attention(q_b, k_b, v_b).reshape(B, S, MOBA_WIDTH)

        mixed = jnp.concatenate([rms_norm(o_a, out_norm_mla_g[l]),
                                 rms_norm(o_b, out_norm_moba_g[l])], axis=-1)
        x = x + mixed @ w_out[l]

        h2 = rms_norm(x, ffn_norm_g[l]).reshape(B * S, D)
        x = x + hierarchical_moe(h2, w_router_group[l], b_router_group[l],
                                 w_router_expert[l], b_router_expert[l],
                                 w_gate[l], w_up[l], w_down[l]).reshape(B, S, D)
    return x
```

```python
import functools

import jax
import jax.numpy as jnp
import numpy as np
from jax import lax
from jax.experimental import pallas as pl
from jax.experimental.pallas import tpu as pltpu

F32 = jnp.float32
BF16 = jnp.bfloat16

D_MODEL = 1024
MLA_HEADS = 8
MLA_NOPE = 64
MLA_ROPE = 32
MLA_QK = MLA_NOPE + MLA_ROPE
MLA_V = 64
MLA_Q_LORA = 256
MLA_KV_LORA = 128
MOBA_HEADS = 8
MOBA_D = 64
MOBA_BLOCK = 256
MOBA_TOPK = 3
ROPE_THETA = 10000.0
EPS = 1e-6
N_GROUPS = 4
EPG = 8
N_EXPERTS = N_GROUPS * EPG
EXPERT_HIDDEN = 256
ROW_BLOCK = 256

LANES = 128
HEAD_SLOT = LANES
PAIR = LANES
PROJ_WIDTH = 2048
NEG_INF = float("-inf")

VMEM_LIMIT = 48 * 1024 * 1024


def _cparams(sem):
    return pltpu.CompilerParams(dimension_semantics=sem, vmem_limit_bytes=VMEM_LIMIT)


def _mla_slot_index():
    idx = -np.ones((HEAD_SLOT,), np.int64)
    idx[0:32] = np.arange(0, 32)
    idx[32:48] = MLA_NOPE + np.arange(0, 16)
    idx[64:96] = np.arange(32, 64)
    idx[96:112] = MLA_NOPE + np.arange(16, 32)
    return idx


def _moba_pair_index():
    head = np.concatenate([np.zeros(32), np.ones(32), np.zeros(32), np.ones(32)]).astype(np.int64)
    feat = np.concatenate([np.arange(32), np.arange(32), 32 + np.arange(32), 32 + np.arange(32)])
    return head, feat


def _gather_cols(w, idx):
    safe = np.where(idx < 0, 0, idx)
    out = jnp.take(w, jnp.asarray(safe), axis=-1)
    return jnp.where(jnp.asarray(idx < 0), 0.0, out)


def _prep_weights(w_in, w_q_up, w_kv_up, q_head_g, k_head_g, mq_g, mk_g):
    slot = _mla_slot_index()
    head, feat = _moba_pair_index()

    kpe_idx = np.where(slot >= MLA_NOPE, slot - MLA_NOPE, -1)
    off_kpe = MLA_Q_LORA + MLA_KV_LORA
    off_mq = off_kpe + MLA_ROPE
    off_mk = off_mq + MOBA_HEADS * MOBA_D
    off_mv = off_mk + MOBA_HEADS * MOBA_D
    moba_idx = np.concatenate([(2 * p + head) * MOBA_D + feat for p in range(MOBA_HEADS // 2)])
    cols = np.concatenate([
        np.arange(0, off_kpe),
        np.where(kpe_idx < 0, -1, off_kpe + kpe_idx),
        off_mq + moba_idx,
        off_mk + moba_idx,
        off_mv + np.arange(MOBA_HEADS * MOBA_D),
    ])
    win = _gather_cols(w_in, cols).astype(BF16)

    q_idx = np.concatenate([np.where(slot < 0, -1, h * MLA_QK + slot) for h in range(MLA_HEADS)])
    wq = _gather_cols(w_q_up, q_idx).astype(BF16)

    nope_slot = np.where((slot >= 0) & (slot < MLA_NOPE), slot, -1)
    kn_idx = np.concatenate([np.where(nope_slot < 0, -1, h * (MLA_NOPE + MLA_V) + nope_slot)
                             for h in range(MLA_HEADS)])
    v_idx = np.concatenate([h * (MLA_NOPE + MLA_V) + MLA_NOPE + np.arange(MLA_V) for h in range(MLA_HEADS)])
    wkv = _gather_cols(w_kv_up, np.concatenate([kn_idx, v_idx])).astype(BF16)

    gq = _gather_cols(q_head_g, slot) * (MLA_QK ** -0.5)
    gk = _gather_cols(k_head_g, slot)
    gmq = jnp.take(mq_g, jnp.asarray(feat)) * (MOBA_D ** -0.5)
    gmk = jnp.take(mk_g, jnp.asarray(feat))
    gains = jnp.stack([gq, gk, gmq, gmk]).astype(F32)
    return win, wq, wkv, gains


def _rope_tables(seq):
    def tab(dim):
        inv = ROPE_THETA ** (-(jnp.arange(0, dim, 2, dtype=F32) / dim))
        ang = jnp.arange(seq, dtype=F32)[:, None] * inv[None, :]
        return jnp.cos(ang), jnp.sin(ang)

    cr, sr = tab(MLA_ROPE)
    cf, sf = tab(MOBA_D)
    one = jnp.ones((seq, 16), F32)
    zero = jnp.zeros((seq, 16), F32)
    cos_a = jnp.concatenate([one, one, cr, one, one, one, cr, one], axis=1)
    sin_a = jnp.concatenate([zero, zero, -sr, zero, zero, zero, sr, zero], axis=1)
    cos_b = jnp.concatenate([cf, cf, cf, cf], axis=1)
    sin_b = jnp.concatenate([-sf, -sf, sf, sf], axis=1)
    return jnp.stack([cos_a, sin_a, cos_b, sin_b])


def _group_matrices():
    lane = np.arange(256)
    g_mla = (lane[:, None] // HEAD_SLOT == lane[None, :] // HEAD_SLOT)
    head, _ = _moba_pair_index()
    hid = np.concatenate([head, 2 + head])
    g_moba = hid[:, None] == hid[None, :]
    return jnp.asarray(np.stack([g_mla, g_moba]), BF16)


def _rms(x, width):
    return lax.rsqrt(jnp.sum(x * x, axis=-1, keepdims=True) * (1.0 / width) + EPS)


def _head_norm(t, gmat, dim):
    sq = (t * t).astype(BF16)
    parts = [jnp.dot(sq[:, c:c + 256], gmat, preferred_element_type=F32)
             for c in range(0, t.shape[1], 256)]
    ssum = jnp.concatenate(parts, axis=1)
    return t * lax.rsqrt(ssum * (1.0 / dim) + EPS)


def _rope(t, cos, sin):
    outs = []
    for c in range(0, t.shape[1], LANES):
        xc = t[:, c:c + LANES]
        outs.append(xc * cos + pltpu.roll(xc, 64, 1) * sin)
    return jnp.concatenate(outs, axis=1)


def _proj_body(x_ref, gx_ref, win_ref, glat_ref, wq_ref, wkv_ref, gains_ref, gmat_ref, rope_ref,
               qa_ref, ka_ref, va_ref, qb_ref, kb_ref, vb_ref, kmean_ref):
    x = x_ref[...]
    h = (x * _rms(x, D_MODEL) * gx_ref[...]).astype(BF16)
    proj = jnp.dot(h, win_ref[...], preferred_element_type=F32)

    q_lat = proj[:, 0:256]
    ql = (q_lat * _rms(q_lat, MLA_Q_LORA) * glat_ref[:, 0:256]).astype(BF16)
    qa = jnp.dot(ql, wq_ref[...], preferred_element_type=F32)

    kv_lat = proj[:, 256:384]
    kvl = (kv_lat * _rms(kv_lat, MLA_KV_LORA) * glat_ref[:, 256:384]).astype(BF16)
    kv = jnp.dot(kvl, wkv_ref[...], preferred_element_type=F32)
    kpe = proj[:, 384:512]
    ka = kv[:, 0:1024] + jnp.concatenate([kpe] * MLA_HEADS, axis=1)
    va_ref[...] = kv[:, 1024:1536].astype(BF16)

    gains = gains_ref[...]
    g_mla = gmat_ref[0]
    g_moba = gmat_ref[1]
    cos_a, sin_a, cos_b, sin_b = rope_ref[0], rope_ref[1], rope_ref[2], rope_ref[3]

    def tile_gain(row, n):
        return jnp.concatenate([gains[row:row + 1, :]] * n, axis=1)

    qa_ref[...] = _rope(_head_norm(qa, g_mla, MLA_QK) * tile_gain(0, 8), cos_a, sin_a).astype(BF16)
    ka_ref[...] = _rope(_head_norm(ka, g_mla, MLA_QK) * tile_gain(1, 8), cos_a, sin_a).astype(BF16)

    mq = proj[:, 512:1024]
    mk = proj[:, 1024:1536]
    qb_ref[...] = _rope(_head_norm(mq, g_moba, MOBA_D) * tile_gain(2, 4), cos_b, sin_b).astype(BF16)
    kb = _rope(_head_norm(mk, g_moba, MOBA_D) * tile_gain(3, 4), cos_b, sin_b)
    kb_ref[...] = kb.astype(BF16)
    kmean_ref[0] = jnp.sum(kb, axis=0, keepdims=True) * (1.0 / MOBA_BLOCK)
    vb_ref[...] = proj[:, 1536:2048].astype(BF16)


def _proj_call(x2, gx, win, glat, wq, wkv, gains, gmat, rope, seq):
    T = x2.shape[0]
    tm = MOBA_BLOCK
    nt = T // tm
    spb = seq // tm
    const = lambda *shape: pl.BlockSpec(shape, lambda i: (0,) * len(shape))
    rows = lambda w: pl.BlockSpec((tm, w), lambda i: (i, 0))
    return pl.pallas_call(
        _proj_body,
        grid=(nt,),
        in_specs=[
            rows(D_MODEL),
            const(1, D_MODEL),
            const(D_MODEL, PROJ_WIDTH),
            const(1, 384),
            const(MLA_Q_LORA, 1024),
            const(MLA_KV_LORA, 1536),
            const(4, LANES),
            const(2, 256, 256),
            pl.BlockSpec((4, tm, LANES), lambda i: (0, i % spb, 0)),
        ],
        out_specs=[rows(1024), rows(1024), rows(512), rows(512), rows(512), rows(512),
                   pl.BlockSpec((1, 1, 512), lambda i: (i, 0, 0))],
        out_shape=[
            jax.ShapeDtypeStruct((T, 1024), BF16),
            jax.ShapeDtypeStruct((T, 1024), BF16),
            jax.ShapeDtypeStruct((T, 512), BF16),
            jax.ShapeDtypeStruct((T, 512), BF16),
            jax.ShapeDtypeStruct((T, 512), BF16),
            jax.ShapeDtypeStruct((T, 512), BF16),
            jax.ShapeDtypeStruct((nt, 1, 512), F32),
        ],
        compiler_params=_cparams(("arbitrary",)),
        name="proj",
    )(x2, gx, win, glat, wq, wkv, gains, gmat, rope)


def _nt_dot(a, b):
    return lax.dot_general(a, b, (((1,), (1,)), ((), ())), preferred_element_type=F32)


def _softmax_step(s, m, l, acc, v):
    m_new = jnp.maximum(m, jnp.max(s, axis=1, keepdims=True))
    alpha = jnp.exp(m - m_new)
    p = jnp.exp(s - m_new)
    l_new = alpha * l + jnp.sum(p, axis=1, keepdims=True)
    acc_new = alpha * acc + jnp.dot(p.astype(BF16), v, preferred_element_type=F32)
    return m_new, l_new, acc_new


def _causal_mask(tq):
    row = lax.broadcasted_iota(jnp.int32, (tq, tq), 0)
    col = lax.broadcasted_iota(jnp.int32, (tq, tq), 1)
    return col <= row


def _mla_body(q_ref, k_ref, v_ref, o_ref):
    tq = q_ref.shape[0]
    qi = pl.program_id(2)
    diag = pl.multiple_of(qi * tq, tq)
    causal = _causal_mask(tq)
    outs = []
    for h in range(2):
        hs = slice(h * HEAD_SLOT, (h + 1) * HEAD_SLOT)
        q = q_ref[:, hs]

        s = jnp.where(causal, _nt_dot(q, k_ref[pl.ds(diag, tq), hs]), NEG_INF)
        m0 = jnp.max(s, axis=1, keepdims=True)
        p = jnp.exp(s - m0)
        l0 = jnp.sum(p, axis=1, keepdims=True)
        acc0 = jnp.dot(p.astype(BF16), v_ref[pl.ds(diag, tq), :], preferred_element_type=F32)

        def past(j, carry):
            m, l, acc = carry
            start = pl.multiple_of(j * tq, tq)
            s = _nt_dot(q, k_ref[pl.ds(start, tq), hs])
            return _softmax_step(s, m, l, acc, v_ref[pl.ds(start, tq), :])

        m, l, acc = lax.fori_loop(0, qi, past, (m0, l0, acc0))
        outs.append(acc / l)
    lane = lax.broadcasted_iota(jnp.int32, (tq, PAIR), 1)
    o_ref[...] = jnp.where(lane < MLA_V, outs[0], outs[1]).astype(o_ref.dtype)


def _mla_call(qa, ka, va, batch, seq):
    T = qa.shape[0]
    tq = 256
    nq = seq // tq
    pairs = MLA_HEADS // 2
    return pl.pallas_call(
        _mla_body,
        grid=(batch, pairs, nq),
        in_specs=[
            pl.BlockSpec((tq, 2 * HEAD_SLOT), lambda b, p, i: (b * nq + i, p)),
            pl.BlockSpec((seq, 2 * HEAD_SLOT), lambda b, p, i: (b, p)),
            pl.BlockSpec((seq, PAIR), lambda b, p, i: (b, p)),
        ],
        out_specs=pl.BlockSpec((tq, PAIR), lambda b, p, i: (b * nq + i, p)),
        out_shape=jax.ShapeDtypeStruct((T, MLA_HEADS * MLA_V), BF16),
        compiler_params=_cparams(("arbitrary", "arbitrary", "arbitrary")),
        name="mla",
    )(qa, ka, va)


def _moba_body(q_ref, k_ref, v_ref, km_ref, o_ref):
    tq = q_ref.shape[0]
    qi = pl.program_id(2)
    diag = pl.multiple_of(qi * tq, tq)
    causal = _causal_mask(tq)
    lane = lax.broadcasted_iota(jnp.int32, (tq, PAIR), 1)
    klane = lax.broadcasted_iota(jnp.int32, (8, PAIR), 1)
    qpair = q_ref[...]
    km = km_ref[...]
    outs = []
    for h in range(2):
        mine = ((lane // 32) % 2) == h
        q = jnp.where(mine, qpair, jnp.zeros_like(qpair))

        kmh = jnp.where(((klane // 32) % 2) == h, km, 0.0)
        km_hi = kmh.astype(BF16)
        km_lo = (kmh - km_hi.astype(F32)).astype(BF16)
        zpad = jnp.zeros((PAIR - 8, PAIR), BF16)
        gate = (_nt_dot(q, jnp.concatenate([km_hi, zpad], axis=0))
                + _nt_dot(q, jnp.concatenate([km_lo, zpad], axis=0)))

        cnt = jnp.zeros((tq, PAIR), F32)
        for i in range(km_ref.shape[0] - 1):
            gi = gate[:, i:i + 1]
            beats = (gi > gate) | ((gi == gate) & (lane > i))
            cnt = cnt + jnp.where(beats & (qi > i), 1.0, 0.0)
        sel = jnp.where((lane < qi) & (cnt < float(MOBA_TOPK)), 1.0, 0.0)

        s = jnp.where(causal, _nt_dot(q, k_ref[pl.ds(diag, tq), :]), NEG_INF)
        m0 = jnp.max(s, axis=1, keepdims=True)
        p = jnp.exp(s - m0)
        l0 = jnp.sum(p, axis=1, keepdims=True)
        acc0 = jnp.dot(p.astype(BF16), v_ref[pl.ds(diag, tq), :], preferred_element_type=F32)

        def past(j, carry):
            m, l, acc = carry
            start = pl.multiple_of(j * tq, tq)
            picked = jnp.sum(jnp.where(lane == j, sel, 0.0), axis=1, keepdims=True) > 0.5
            s = jnp.where(picked, _nt_dot(q, k_ref[pl.ds(start, tq), :]), NEG_INF)
            return _softmax_step(s, m, l, acc, v_ref[pl.ds(start, tq), :])

        m, l, acc = lax.fori_loop(0, qi, past, (m0, l0, acc0))
        outs.append(acc / l)
    o_ref[...] = jnp.where(lane < MOBA_D, outs[0], outs[1]).astype(o_ref.dtype)


def _moba_call(qb, kb, vb, kmean, batch, seq):
    T = qb.shape[0]
    tq = MOBA_BLOCK
    nq = seq // tq
    pairs = MOBA_HEADS // 2
    return pl.pallas_call(
        _moba_body,
        grid=(batch, pairs, nq),
        in_specs=[
            pl.BlockSpec((tq, PAIR), lambda b, p, i: (b * nq + i, p)),
            pl.BlockSpec((seq, PAIR), lambda b, p, i: (b, p)),
            pl.BlockSpec((seq, PAIR), lambda b, p, i: (b, p)),
            pl.BlockSpec((nq, PAIR), lambda b, p, i: (b, p)),
        ],
        out_specs=pl.BlockSpec((tq, PAIR), lambda b, p, i: (b * nq + i, p)),
        out_shape=jax.ShapeDtypeStruct((T, MOBA_HEADS * MOBA_D), BF16),
        compiler_params=_cparams(("arbitrary", "arbitrary", "arbitrary")),
        name="moba",
    )(qb, kb, vb, kmean)


def _merge_body(oa_ref, ob_ref, x_ref, gout_ref, wout_ref, gffn_ref, wr_ref, br_ref, tri_ref,
                x1_ref, h2_ref, meta_ref, wts_ref, counts_ref):
    tm = x_ref.shape[0]

    @pl.when(pl.program_id(0) == 0)
    def _():
        counts_ref[...] = jnp.zeros_like(counts_ref)

    oa = oa_ref[...].astype(F32)
    ob = ob_ref[...].astype(F32)
    na = oa * _rms(oa, 512) * gout_ref[:, 0:512]
    nb = ob * _rms(ob, 512) * gout_ref[:, 512:1024]
    mixed = jnp.concatenate([na, nb], axis=1).astype(BF16)
    x1 = x_ref[...] + jnp.dot(mixed, wout_ref[...], preferred_element_type=F32)
    x1_ref[...] = x1
    h2 = x1 * _rms(x1, D_MODEL) * gffn_ref[...]
    h2_ref[...] = h2

    logits = jnp.dot(h2.astype(BF16), wr_ref[...], preferred_element_type=F32) + br_ref[...]
    lane = lax.broadcasted_iota(jnp.int32, (tm, LANES), 1)
    lane_f = lane.astype(F32)
    big = float(LANES)

    def first_max(vals):
        mx = jnp.max(vals, axis=1, keepdims=True)
        idx = jnp.min(jnp.where(vals == mx, lane_f, big), axis=1, keepdims=True)
        return mx, idx

    is_group = (lane >= N_EXPERTS) & (lane < N_EXPERTS + N_GROUPS)
    gl = jnp.where(is_group, logits, NEG_INF)
    gmax, gidx = first_max(gl)
    g_w = 1.0 / jnp.sum(jnp.exp(gl - gmax), axis=1, keepdims=True)
    g_sel = gidx - float(N_EXPERTS)

    in_group = (lane < N_EXPERTS) & ((lane // EPG).astype(F32) == g_sel)
    el = jnp.where(in_group, logits, NEG_INF)
    v1, i1 = first_max(el)
    el2 = jnp.where(lane_f == i1, NEG_INF, el)
    v2, i2 = first_max(el2)
    t = jnp.exp(v2 - v1)
    w1 = g_w * (1.0 / (1.0 + t))
    w2 = g_w * (t / (1.0 + t))

    oh1 = lane_f == i1
    oh2 = lane_f == i2
    onehot = jnp.where(oh1 | oh2, 1.0, 0.0)
    before = jnp.dot(tri_ref[...], onehot.astype(BF16), preferred_element_type=F32) + counts_ref[0:1, :]
    pos1 = jnp.sum(jnp.where(oh1, before, 0.0), axis=1, keepdims=True)
    pos2 = jnp.sum(jnp.where(oh2, before, 0.0), axis=1, keepdims=True)
    counts_ref[...] = counts_ref[...] + jnp.sum(onehot, axis=0, keepdims=True)

    meta = jnp.where(lane == 0, i1, jnp.where(lane == 1, i2, jnp.where(lane == 2, pos1, jnp.where(lane == 3, pos2, 0.0))))
    meta_ref[...] = meta.astype(jnp.int32)
    wts_ref[...] = jnp.where(lane == 0, w1, jnp.where(lane == 1, w2, 0.0))


def _merge_call(oa, ob, x2, gout, wout, gffn, wr, br, tri):
    T = x2.shape[0]
    tm = tri.shape[0]
    const = lambda *shape: pl.BlockSpec(shape, lambda i: (0,) * len(shape))
    rows = lambda w: pl.BlockSpec((tm, w), lambda i: (i, 0))
    return pl.pallas_call(
        _merge_body,
        grid=(T // tm,),
        in_specs=[rows(512), rows(512), rows(D_MODEL), const(1, D_MODEL), const(D_MODEL, D_MODEL),
                  const(1, D_MODEL), const(D_MODEL, LANES), const(1, LANES), const(tm, tm)],
        out_specs=[rows(D_MODEL), rows(D_MODEL), rows(LANES), rows(LANES), const(8, LANES)],
        out_shape=[
            jax.ShapeDtypeStruct((T, D_MODEL), F32),
            jax.ShapeDtypeStruct((T, D_MODEL), F32),
            jax.ShapeDtypeStruct((T, LANES), jnp.int32),
            jax.ShapeDtypeStruct((T, LANES), F32),
            jax.ShapeDtypeStruct((8, LANES), F32),
        ],
        compiler_params=_cparams(("arbitrary",)),
        name="merge",
    )(oa, ob, x2, gout, wout, gffn, wr, br, tri)


def _dispatch_body(dest_ref, h2_ref, xs_in_ref, xs_ref, sem):
    del xs_in_ref
    n_tok = dest_ref.shape[0] // 2
    base = pl.program_id(0) * n_tok

    def copy(t, k):
        return pltpu.make_async_copy(h2_ref.at[pl.ds(base + t, 1)],
                                     xs_ref.at[pl.ds(dest_ref[2 * t + k], 1)], sem)

    def start(t, c):
        copy(t, 0).start()
        copy(t, 1).start()
        return c

    def wait(t, c):
        copy(t, 0).wait()
        copy(t, 1).wait()
        return c

    lax.fori_loop(0, n_tok, start, 0)
    lax.fori_loop(0, n_tok, wait, 0)


def _dispatch_call(dest, h2, xs_zero, tok_per_step):
    T = h2.shape[0]
    return pl.pallas_call(
        _dispatch_body,
        grid=(T // tok_per_step,),
        in_specs=[
            pl.BlockSpec((2 * tok_per_step,), lambda i: (i,), memory_space=pltpu.SMEM),
            pl.BlockSpec(memory_space=pl.ANY),
            pl.BlockSpec(memory_space=pl.ANY),
        ],
        out_specs=pl.BlockSpec(memory_space=pl.ANY),
        out_shape=jax.ShapeDtypeStruct(xs_zero.shape, xs_zero.dtype),
        scratch_shapes=[pltpu.SemaphoreType.DMA(())],
        input_output_aliases={2: 0},
        compiler_params=_cparams(("arbitrary",)),
        name="dispatch",
    )(dest, h2, xs_zero)


def _expert_body(bexp_ref, nused_ref, xs_ref, wg_ref, wu_ref, wd_ref, y_ref):
    del bexp_ref
    used = pl.program_id(0) < nused_ref[0]

    @pl.when(used)
    def _():
        xb = xs_ref[...].astype(BF16)
        a = jnp.dot(xb, wg_ref[...], preferred_element_type=F32)
        u = jnp.dot(xb, wu_ref[...], preferred_element_type=F32)
        act = (a * (1.0 / (1.0 + jnp.exp(-a))) * u).astype(BF16)
        y_ref[...] = jnp.dot(act, wd_ref[...], preferred_element_type=F32)

    @pl.when(jnp.logical_not(used))
    def _():
        y_ref[...] = jnp.zeros_like(y_ref)


def _expert_call(block_expert, n_used, xs, wg, wu, wd):
    P = xs.shape[0]
    n_blocks = P // ROW_BLOCK

    def row_map(i, be, nu):
        return (jnp.minimum(i, nu[0] - 1), 0)

    def w_map(i, be, nu):
        return (be[jnp.minimum(i, nu[0] - 1)], 0, 0)

    grid_spec = pltpu.PrefetchScalarGridSpec(
        num_scalar_prefetch=2,
        grid=(n_blocks,),
        in_specs=[
            pl.BlockSpec((ROW_BLOCK, D_MODEL), row_map),
            pl.BlockSpec((None, D_MODEL, EXPERT_HIDDEN), w_map),
            pl.BlockSpec((None, D_MODEL, EXPERT_HIDDEN), w_map),
            pl.BlockSpec((None, EXPERT_HIDDEN, D_MODEL), w_map),
        ],
        out_specs=pl.BlockSpec((ROW_BLOCK, D_MODEL), lambda i, be, nu: (i, 0)),
    )
    return pl.pallas_call(
        _expert_body,
        grid_spec=grid_spec,
        out_shape=jax.ShapeDtypeStruct((P, D_MODEL), F32),
        compiler_params=_cparams(("arbitrary",)),
        name="experts",
    )(block_expert, n_used, xs, wg, wu, wd)


def _combine_body(dest_ref, x1_ref, wts_ref, ys_ref, o_ref, y0_buf, y1_buf, sem):
    n_tok = x1_ref.shape[0]

    def copy(t, k):
        buf = y0_buf if k == 0 else y1_buf
        return pltpu.make_async_copy(ys_ref.at[pl.ds(dest_ref[2 * t + k], 1)], buf.at[pl.ds(t, 1)], sem)

    def start(t, c):
        copy(t, 0).start()
        copy(t, 1).start()
        return c

    def wait(t, c):
        copy(t, 0).wait()
        copy(t, 1).wait()
        return c

    lax.fori_loop(0, n_tok, start, 0)
    lax.fori_loop(0, n_tok, wait, 0)
    w = wts_ref[...]
    o_ref[...] = x1_ref[...] + (w[:, 0:1] * y0_buf[...] + w[:, 1:2] * y1_buf[...])


def _combine_call(dest, x1, wts, ys, tok_per_step):
    T = x1.shape[0]
    return pl.pallas_call(
        _combine_body,
        grid=(T // tok_per_step,),
        in_specs=[
            pl.BlockSpec((2 * tok_per_step,), lambda i: (i,), memory_space=pltpu.SMEM),
            pl.BlockSpec((tok_per_step, D_MODEL), lambda i: (i, 0)),
            pl.BlockSpec((tok_per_step, LANES), lambda i: (i, 0)),
            pl.BlockSpec(memory_space=pl.ANY),
        ],
        out_specs=pl.BlockSpec((tok_per_step, D_MODEL), lambda i: (i, 0)),
        out_shape=jax.ShapeDtypeStruct((T, D_MODEL), F32),
        scratch_shapes=[pltpu.VMEM((tok_per_step, D_MODEL), F32),
                        pltpu.VMEM((tok_per_step, D_MODEL), F32),
                        pltpu.SemaphoreType.DMA(())],
        compiler_params=_cparams(("arbitrary",)),
        name="combine",
    )(dest, x1, wts, ys)


def _layer(x, attn_norm_g, w_in, mla_q_lat_g, w_q_up, mla_kv_lat_g, w_kv_up, mla_q_head_g,
           mla_k_head_g, moba_q_head_g, moba_k_head_g, out_norm_mla_g, out_norm_moba_g, w_out,
           ffn_norm_g, w_router_group, b_router_group, w_router_expert, b_router_expert,
           w_gate, w_up, w_down):
    B, S, D = x.shape
    T = B * S
    x2 = x.reshape(T, D)

    win, wq, wkv, gains = _prep_weights(w_in, w_q_up, w_kv_up, mla_q_head_g, mla_k_head_g,
                                        moba_q_head_g, moba_k_head_g)
    glat = jnp.concatenate([mla_q_lat_g, mla_kv_lat_g])[None, :]
    qa, ka, va, qb, kb, vb, kmean = _proj_call(
        x2, attn_norm_g[None, :], win, glat, wq, wkv, gains, _group_matrices(), _rope_tables(S), S)

    oa = _mla_call(qa, ka, va, B, S)
    ob = _moba_call(qb, kb, vb, kmean.reshape(T // MOBA_BLOCK, 512), B, S)

    tm = 512
    wr = jnp.zeros((D, LANES), F32).at[:, :N_EXPERTS].set(w_router_expert)
    wr = wr.at[:, N_EXPERTS:N_EXPERTS + N_GROUPS].set(w_router_group).astype(BF16)
    br = jnp.zeros((1, LANES), F32).at[0, :N_EXPERTS].set(b_router_expert)
    br = br.at[0, N_EXPERTS:N_EXPERTS + N_GROUPS].set(b_router_group)
    tri = jnp.asarray(np.tril(np.ones((tm, tm), np.float32), -1), BF16)
    gout = jnp.concatenate([out_norm_mla_g, out_norm_moba_g])[None, :]
    x1, h2, meta, wts, counts = _merge_call(oa, ob, x2, gout, w_out.astype(BF16), ffn_norm_g[None, :],
                                            wr, br, tri)

    counts = counts[0, :N_EXPERTS].astype(jnp.int32)
    padded = (counts + ROW_BLOCK - 1) // ROW_BLOCK * ROW_BLOCK
    pad_end = jnp.cumsum(padded)
    pad_start = pad_end - padded
    n_blocks = (2 * T + ROW_BLOCK - 1) // ROW_BLOCK + N_EXPERTS
    n_used = (pad_end[-1] // ROW_BLOCK).astype(jnp.int32)
    blk = jnp.minimum(jnp.arange(n_blocks, dtype=jnp.int32), n_used - 1) * ROW_BLOCK
    block_expert = jnp.minimum(jnp.searchsorted(pad_end, blk, side="right"), N_EXPERTS - 1).astype(jnp.int32)
    dest = (jnp.take(pad_start, meta[:, 0:2]) + meta[:, 2:4]).astype(jnp.int32).reshape(2 * T)

    xs = _dispatch_call(dest, h2, jnp.zeros((n_blocks * ROW_BLOCK, D), F32), tm)
    ys = _expert_call(block_expert, n_used[None], xs,
                      w_gate.astype(BF16), w_up.astype(BF16), w_down.astype(BF16))
    out = _combine_call(dest, x1, wts, ys, 256)
    return out.reshape(B, S, D)


def kernel(x, attn_norm_g, w_in, mla_q_lat_g, w_q_up, mla_kv_lat_g, w_kv_up, mla_q_head_g, mla_k_head_g, moba_q_head_g, moba_k_head_g, out_norm_mla_g, out_norm_moba_g, w_out, ffn_norm_g, w_router_group, b_router_group, w_router_expert, b_router_expert, w_gate, w_up, w_down):
    return _layer(x, attn_norm_g[0], w_in[0], mla_q_lat_g[0], w_q_up[0], mla_kv_lat_g[0], w_kv_up[0],
                  mla_q_head_g[0], mla_k_head_g[0], moba_q_head_g[0], moba_k_head_g[0],
                  out_norm_mla_g[0], out_norm_moba_g[0], w_out[0], ffn_norm_g[0],
                  w_router_group[0], b_router_group[0], w_router_expert[0], b_router_expert[0],
                  w_gate[0], w_up[0], w_down[0])
```

```python
import functools

import jax
import jax.numpy as jnp
import numpy as np
from jax import lax
from jax.experimental import pallas as pl
from jax.experimental.pallas import tpu as pltpu

F32 = jnp.float32
BF16 = jnp.bfloat16

D_MODEL = 1024
MLA_HEADS = 8
MLA_NOPE = 64
MLA_ROPE = 32
MLA_QK = MLA_NOPE + MLA_ROPE
MLA_V = 64
MLA_Q_LORA = 256
MLA_KV_LORA = 128
MOBA_HEADS = 8
MOBA_D = 64
MOBA_BLOCK = 256
MOBA_TOPK = 3
ROPE_THETA = 10000.0
EPS = 1e-6
N_GROUPS = 4
EPG = 8
N_EXPERTS = N_GROUPS * EPG
EXPERT_HIDDEN = 256
ROW_BLOCK = 256

LANES = 128
HEAD_SLOT = LANES
PAIR = LANES
PROJ_WIDTH = 2048
NEG_INF = float("-inf")

VMEM_LIMIT = 48 * 1024 * 1024


def _cparams(sem):
    return pltpu.CompilerParams(dimension_semantics=sem, vmem_limit_bytes=VMEM_LIMIT)


def _mla_slot_index():
    idx = -np.ones((HEAD_SLOT,), np.int64)
    idx[0:32] = np.arange(0, 32)
    idx[32:48] = MLA_NOPE + np.arange(0, 16)
    idx[64:96] = np.arange(32, 64)
    idx[96:112] = MLA_NOPE + np.arange(16, 32)
    return idx


def _moba_pair_index():
    head = np.concatenate([np.zeros(32), np.ones(32), np.zeros(32), np.ones(32)]).astype(np.int64)
    feat = np.concatenate([np.arange(32), np.arange(32), 32 + np.arange(32), 32 + np.arange(32)])
    return head, feat


def _gather_cols(w, idx):
    safe = np.where(idx < 0, 0, idx)
    out = jnp.take(w, jnp.asarray(safe), axis=-1)
    return jnp.where(jnp.asarray(idx < 0), 0.0, out)


def _prep_weights(w_in, w_q_up, w_kv_up, q_head_g, k_head_g, mq_g, mk_g):
    slot = _mla_slot_index()
    head, feat = _moba_pair_index()

    kpe_idx = np.where(slot >= MLA_NOPE, slot - MLA_NOPE, -1)
    off_kpe = MLA_Q_LORA + MLA_KV_LORA
    off_mq = off_kpe + MLA_ROPE
    off_mk = off_mq + MOBA_HEADS * MOBA_D
    off_mv = off_mk + MOBA_HEADS * MOBA_D
    moba_idx = np.concatenate([(2 * p + head) * MOBA_D + feat for p in range(MOBA_HEADS // 2)])
    cols = np.concatenate([
        np.arange(0, off_kpe),
        np.where(kpe_idx < 0, -1, off_kpe + kpe_idx),
        off_mq + moba_idx,
        off_mk + moba_idx,
        off_mv + np.arange(MOBA_HEADS * MOBA_D),
    ])
    win = _gather_cols(w_in, cols).astype(BF16)

    q_idx = np.concatenate([np.where(slot < 0, -1, h * MLA_QK + slot) for h in range(MLA_HEADS)])
    wq = _gather_cols(w_q_up, q_idx).astype(BF16)

    nope_slot = np.where((slot >= 0) & (slot < MLA_NOPE), slot, -1)
    kn_idx = np.concatenate([np.where(nope_slot < 0, -1, h * (MLA_NOPE + MLA_V) + nope_slot)
                             for h in range(MLA_HEADS)])
    v_idx = np.concatenate([h * (MLA_NOPE + MLA_V) + MLA_NOPE + np.arange(MLA_V) for h in range(MLA_HEADS)])
    wkv = _gather_cols(w_kv_up, np.concatenate([kn_idx, v_idx])).astype(BF16)

    gq = _gather_cols(q_head_g, slot) * (MLA_QK ** -0.5)
    gk = _gather_cols(k_head_g, slot)
    gmq = jnp.take(mq_g, jnp.asarray(feat)) * (MOBA_D ** -0.5)
    gmk = jnp.take(mk_g, jnp.asarray(feat))
    gains = jnp.stack([gq, gk, gmq, gmk]).astype(F32)
    return win, wq, wkv, gains


def _rope_tables(seq):
    def tab(dim):
        inv = ROPE_THETA ** (-(jnp.arange(0, dim, 2, dtype=F32) / dim))
        ang = jnp.arange(seq, dtype=F32)[:, None] * inv[None, :]
        return jnp.cos(ang), jnp.sin(ang)

    cr, sr = tab(MLA_ROPE)
    cf, sf = tab(MOBA_D)
    one = jnp.ones((seq, 16), F32)
    zero = jnp.zeros((seq, 16), F32)
    cos_a = jnp.concatenate([one, one, cr, one, one, one, cr, one], axis=1)
    sin_a = jnp.concatenate([zero, zero, -sr, zero, zero, zero, sr, zero], axis=1)
    cos_b = jnp.concatenate([cf, cf, cf, cf], axis=1)
    sin_b = jnp.concatenate([-sf, -sf, sf, sf], axis=1)
    return jnp.stack([cos_a, sin_a, cos_b, sin_b])


def _group_matrices():
    lane = np.arange(256)
    g_mla = (lane[:, None] // HEAD_SLOT == lane[None, :] // HEAD_SLOT)
    head, _ = _moba_pair_index()
    hid = np.concatenate([head, 2 + head])
    g_moba = hid[:, None] == hid[None, :]
    return jnp.asarray(np.stack([g_mla, g_moba]), BF16)


def _rms(x, width):
    return lax.rsqrt(jnp.sum(x * x, axis=-1, keepdims=True) * (1.0 / width) + EPS)


def _head_norm(t, gmat, dim):
    sq = (t * t).astype(BF16)
    parts = [jnp.dot(sq[:, c:c + 256], gmat, preferred_element_type=F32)
             for c in range(0, t.shape[1], 256)]
    ssum = jnp.concatenate(parts, axis=1)
    return t * lax.rsqrt(ssum * (1.0 / dim) + EPS)


def _rope(t, cos, sin):
    outs = []
    for c in range(0, t.shape[1], LANES):
        xc = t[:, c:c + LANES]
        outs.append(xc * cos + pltpu.roll(xc, 64, 1) * sin)
    return jnp.concatenate(outs, axis=1)


def _proj_body(x_ref, gx_ref, win_ref, glat_ref, wq_ref, wkv_ref, gains_ref, gmat_ref, rope_ref,
               qa_ref, ka_ref, va_ref, qb_ref, kb_ref, vb_ref, kmean_ref):
    x = x_ref[...]
    h = (x * _rms(x, D_MODEL) * gx_ref[...]).astype(BF16)
    proj = jnp.dot(h, win_ref[...], preferred_element_type=F32)

    q_lat = proj[:, 0:256]
    ql = (q_lat * _rms(q_lat, MLA_Q_LORA) * glat_ref[:, 0:256]).astype(BF16)
    qa = jnp.dot(ql, wq_ref[...], preferred_element_type=F32)

    kv_lat = proj[:, 256:384]
    kvl = (kv_lat * _rms(kv_lat, MLA_KV_LORA) * glat_ref[:, 256:384]).astype(BF16)
    kv = jnp.dot(kvl, wkv_ref[...], preferred_element_type=F32)
    kpe = proj[:, 384:512]
    ka = kv[:, 0:1024] + jnp.concatenate([kpe] * MLA_HEADS, axis=1)
    va_ref[...] = kv[:, 1024:1536].astype(BF16)

    gains = gains_ref[...]
    g_mla = gmat_ref[0]
    g_moba = gmat_ref[1]
    cos_a, sin_a, cos_b, sin_b = rope_ref[0], rope_ref[1], rope_ref[2], rope_ref[3]

    def tile_gain(row, n):
        return jnp.concatenate([gains[row:row + 1, :]] * n, axis=1)

    qa_ref[...] = _rope(_head_norm(qa, g_mla, MLA_QK) * tile_gain(0, 8), cos_a, sin_a).astype(BF16)
    ka_ref[...] = _rope(_head_norm(ka, g_mla, MLA_QK) * tile_gain(1, 8), cos_a, sin_a).astype(BF16)

    mq = proj[:, 512:1024]
    mk = proj[:, 1024:1536]
    qb_ref[...] = _rope(_head_norm(mq, g_moba, MOBA_D) * tile_gain(2, 4), cos_b, sin_b).astype(BF16)
    kb = _rope(_head_norm(mk, g_moba, MOBA_D) * tile_gain(3, 4), cos_b, sin_b)
    kb_ref[...] = kb.astype(BF16)
    kmean_ref[0] = jnp.sum(kb, axis=0, keepdims=True) * (1.0 / MOBA_BLOCK)
    vb_ref[...] = proj[:, 1536:2048].astype(BF16)


def _proj_call(x2, gx, win, glat, wq, wkv, gains, gmat, rope, seq):
    T = x2.shape[0]
    tm = MOBA_BLOCK
    nt = T // tm
    spb = seq // tm
    const = lambda *shape: pl.BlockSpec(shape, lambda i: (0,) * len(shape))
    rows = lambda w: pl.BlockSpec((tm, w), lambda i: (i, 0))
    return pl.pallas_call(
        _proj_body,
        grid=(nt,),
        in_specs=[
            rows(D_MODEL),
            const(1, D_MODEL),
            const(D_MODEL, PROJ_WIDTH),
            const(1, 384),
            const(MLA_Q_LORA, 1024),
            const(MLA_KV_LORA, 1536),
            const(4, LANES),
            const(2, 256, 256),
            pl.BlockSpec((4, tm, LANES), lambda i: (0, i % spb, 0)),
        ],
        out_specs=[rows(1024), rows(1024), rows(512), rows(512), rows(512), rows(512),
                   pl.BlockSpec((1, 1, 512), lambda i: (i, 0, 0))],
        out_shape=[
            jax.ShapeDtypeStruct((T, 1024), BF16),
            jax.ShapeDtypeStruct((T, 1024), BF16),
            jax.ShapeDtypeStruct((T, 512), BF16),
            jax.ShapeDtypeStruct((T, 512), BF16),
            jax.ShapeDtypeStruct((T, 512), BF16),
            jax.ShapeDtypeStruct((T, 512), BF16),
            jax.ShapeDtypeStruct((nt, 1, 512), F32),
        ],
        compiler_params=_cparams(("arbitrary",)),
        name="proj",
    )(x2, gx, win, glat, wq, wkv, gains, gmat, rope)


def _nt_dot(a, b):
    return lax.dot_general(a, b, (((1,), (1,)), ((), ())), preferred_element_type=F32)


ATT_BLOCK = MOBA_BLOCK
ATT_CHAINS = 4
M_INIT = -1e30
NOT_PICKED = -(1 << 20)


def _attn_body(moba, q_ref, k_ref, v_ref, *rest):
    if moba:
        km_ref, o_ref, qs_ref, m_ref, l_ref, acc_ref, sel_ref = rest
    else:
        o_ref, qs_ref, m_ref, l_ref, acc_ref = rest
    tb = ATT_BLOCK
    rows = 2 * tb
    nq = q_ref.shape[0] // tb
    qw = q_ref.shape[1]
    lane_q = lax.broadcasted_iota(jnp.int32, (tb, qw), 1)
    lane = lax.broadcasted_iota(jnp.int32, (rows, PAIR), 1)

    if moba:
        km = km_ref[...]
        km_hi = km.astype(BF16)
        km_lo = (km - km_hi.astype(F32)).astype(BF16)
        zpad = jnp.zeros((PAIR - nq, PAIR), BF16)
        km_hi = jnp.concatenate([km_hi, zpad], axis=0)
        km_lo = jnp.concatenate([km_lo, zpad], axis=0)

    def prep(i, c):
        q = q_ref[pl.ds(pl.multiple_of(i * tb, tb), tb), :]
        if moba:
            head_of_lane = (lane_q // 32) % 2
        else:
            head_of_lane = lane_q // HEAD_SLOT
        zero = jnp.zeros_like(q)
        qs = jnp.concatenate([jnp.where(head_of_lane == 0, q, zero),
                              jnp.where(head_of_lane == 1, q, zero)], axis=0)
        qs_ref[i] = qs
        m_ref[i] = jnp.full((rows, 1), M_INIT, F32)
        l_ref[i] = jnp.zeros((rows, 1), F32)
        acc_ref[i] = jnp.zeros((rows, PAIR), F32)
        if moba:
            gate = _nt_dot(qs, km_hi) + _nt_dot(qs, km_lo)
            cnt = jnp.zeros((rows, PAIR), F32)
            for b in range(nq - 1):
                gb = gate[:, b:b + 1]
                beats = (gb > gate) | ((gb == gate) & (lane > b))
                cnt = cnt + jnp.where(beats & (i > b), 1.0, 0.0)
            picked = ((lane < i) & (cnt < float(MOBA_TOPK))) | (lane == i)
            sel_ref[i] = jnp.where(picked, 1.0, 0.0)
        return c

    lax.fori_loop(0, nq, prep, 0)

    row = lax.broadcasted_iota(jnp.int32, (rows, tb), 0)
    col = lax.broadcasted_iota(jnp.int32, (rows, tb), 1)
    key_minus_query = col - (row % tb)

    def item(r, t):
        a, b = r, nq - 1 - r
        first = t <= r
        i = jnp.where(first, a, b)
        j = jnp.where(first, t, t - (r + 1))
        kj = k_ref[pl.ds(pl.multiple_of(j * tb, tb), tb), :]
        vj = v_ref[pl.ds(pl.multiple_of(j * tb, tb), tb), :]
        s = _nt_dot(qs_ref[i], kj)
        limit = (i - j) * tb
        if moba:
            chosen = jnp.sum(jnp.where(lane == j, sel_ref[i], 0.0), axis=1, keepdims=True) > 0.5
            limit = jnp.where(chosen, limit, NOT_PICKED)
        s = jnp.where(key_minus_query <= limit, s, NEG_INF)

        def pick(ref):
            va, vb = ref[a], ref[b]
            return va, vb, jnp.where(first, va, vb)

        def put(ref, va, vb, new):
            ref[a] = jnp.where(first, new, va)
            ref[b] = jnp.where(first, vb, new)

        ma, mb, m_prev = pick(m_ref)
        la, lb, l_prev = pick(l_ref)
        aa, ab, acc_prev = pick(acc_ref)
        m_new = jnp.maximum(m_prev, jnp.max(s, axis=1, keepdims=True))
        alpha = jnp.exp(m_prev - m_new)
        p = jnp.exp(s - m_new)
        put(l_ref, la, lb, alpha * l_prev + jnp.sum(p, axis=1, keepdims=True))
        put(acc_ref, aa, ab,
            alpha * acc_prev + jnp.dot(p.astype(BF16), vj, preferred_element_type=F32))
        put(m_ref, ma, mb, m_new)

    def step(t, c):
        for r in range(ATT_CHAINS):
            item(r, t)
        return c

    lax.fori_loop(0, nq + 1, step, 0)

    def finish(i, c):
        o = acc_ref[i] / l_ref[i]
        lane_o = lax.broadcasted_iota(jnp.int32, (tb, PAIR), 1)
        o_ref[pl.ds(pl.multiple_of(i * tb, tb), tb), :] = jnp.where(
            lane_o < PAIR // 2, o[0:tb], o[tb:rows]).astype(o_ref.dtype)
        return c

    lax.fori_loop(0, nq, finish, 0)


def _attn_call(moba, q, k, v, kmean, batch, seq, name):
    T, width = q.shape
    pairs = MLA_HEADS // 2
    qw = width // pairs
    nq = seq // ATT_BLOCK
    assert nq == 2 * ATT_CHAINS, "chain pairing (r, nq-1-r) needs nq == 2 * ATT_CHAINS"
    rows = 2 * ATT_BLOCK
    seq_block = lambda w: pl.BlockSpec((seq, w), lambda b, p: (b, p))
    in_specs = [seq_block(qw), seq_block(qw), seq_block(PAIR)]
    scratch = [pltpu.VMEM((nq, rows, qw), BF16),
               pltpu.VMEM((nq, rows, 1), F32),
               pltpu.VMEM((nq, rows, 1), F32),
               pltpu.VMEM((nq, rows, PAIR), F32)]
    args = [q, k, v]
    if moba:
        in_specs.append(pl.BlockSpec((nq, PAIR), lambda b, p: (b, p)))
        scratch.append(pltpu.VMEM((nq, rows, PAIR), F32))
        args.append(kmean)
    return pl.pallas_call(
        functools.partial(_attn_body, moba),
        grid=(batch, pairs),
        in_specs=in_specs,
        out_specs=seq_block(PAIR),
        out_shape=jax.ShapeDtypeStruct((T, pairs * PAIR), BF16),
        scratch_shapes=scratch,
        compiler_params=_cparams(("arbitrary", "arbitrary")),
        name=name,
    )(*args)


ONES_ROWS = 16


def _attnT_body(moba, q_ref, k_ref, v_ref, *rest):
    if moba:
        km_ref, o_ref, qs_ref, vt_ref, m_ref, acc_ref, sel_ref = rest
    else:
        o_ref, qs_ref, vt_ref, m_ref, acc_ref = rest
    tb = ATT_BLOCK
    cols = 2 * tb
    nq = q_ref.shape[0] // tb
    qw = q_ref.shape[1]
    lane_q = lax.broadcasted_iota(jnp.int32, (tb, qw), 1)

    if moba:
        km = km_ref[...]
        km_hi = km.astype(BF16)
        km_lo = (km - km_hi.astype(F32)).astype(BF16)
        zpad = jnp.zeros((16 - nq, PAIR), BF16)
        km_hi = jnp.concatenate([km_hi, zpad], axis=0)
        km_lo = jnp.concatenate([km_lo, zpad], axis=0)
        blk = lax.broadcasted_iota(jnp.int32, (16, cols), 0)

    def prep(i, c):
        start = pl.multiple_of(i * tb, tb)
        q = q_ref[pl.ds(start, tb), :]
        if moba:
            head_of_lane = (lane_q // 32) % 2
        else:
            head_of_lane = lane_q // HEAD_SLOT
        zero = jnp.zeros_like(q)
        qs = jnp.concatenate([jnp.where(head_of_lane == 0, q, zero),
                              jnp.where(head_of_lane == 1, q, zero)], axis=0)
        qs_ref[i] = qs
        vt = v_ref[pl.ds(start, tb), :].astype(F32).T.astype(BF16)
        vt_ref[i] = jnp.concatenate([vt, jnp.ones((ONES_ROWS, tb), BF16)], axis=0)
        m_ref[i] = jnp.full((1, cols), M_INIT, F32)
        acc_ref[i] = jnp.zeros((PAIR + ONES_ROWS, cols), F32)
        if moba:
            gate = _nt_dot(km_hi, qs) + _nt_dot(km_lo, qs)
            cnt = jnp.zeros((16, cols), F32)
            for b in range(nq - 1):
                gb = gate[b:b + 1, :]
                beats = (gb > gate) | ((gb == gate) & (blk > b))
                cnt = cnt + jnp.where(beats & (i > b), 1.0, 0.0)
            picked = ((blk < i) & (cnt < float(MOBA_TOPK))) | (blk == i)
            sel_ref[i] = jnp.where(picked, 1.0, 0.0)
        return c

    lax.fori_loop(0, nq, prep, 0)

    key = lax.broadcasted_iota(jnp.int32, (tb, cols), 0)
    qry = lax.broadcasted_iota(jnp.int32, (tb, cols), 1) % tb
    key_minus_query = key - qry

    def item(r, t):
        a, b = r, nq - 1 - r
        first = t <= r
        i = jnp.where(first, a, b)
        j = jnp.where(first, t, t - (r + 1))
        kj = k_ref[pl.ds(pl.multiple_of(j * tb, tb), tb), :]
        s = _nt_dot(kj, qs_ref[i])
        limit = (i - j) * tb
        if moba:
            chosen = sel_ref[i, pl.ds(j, 1), :] > 0.5
            limit = jnp.where(chosen, limit, NOT_PICKED)
        s = jnp.where(key_minus_query <= limit, s, NEG_INF)

        ma, mb = m_ref[a], m_ref[b]
        m_prev = jnp.where(first, ma, mb)
        m_new = jnp.maximum(m_prev, jnp.max(s, axis=0, keepdims=True))
        alpha = jnp.exp(m_prev - m_new)
        p = jnp.exp(s - m_new).astype(BF16)
        m_ref[a] = jnp.where(first, m_new, ma)
        m_ref[b] = jnp.where(first, mb, m_new)

        aa, ab = acc_ref[a], acc_ref[b]
        acc_new = alpha * jnp.where(first, aa, ab) + jnp.dot(vt_ref[j], p, preferred_element_type=F32)
        acc_ref[a] = jnp.where(first, acc_new, aa)
        acc_ref[b] = jnp.where(first, ab, acc_new)

    def step(t, c):
        for r in range(ATT_CHAINS):
            item(r, t)
        return c

    lax.fori_loop(0, nq + 1, step, 0)

    def finish(i, c):
        acc = acc_ref[i]
        o = acc[0:PAIR, :] / acc[PAIR:PAIR + 1, :]
        vrow = lax.broadcasted_iota(jnp.int32, (PAIR, tb), 0)
        ot = jnp.where(vrow < PAIR // 2, o[:, 0:tb], o[:, tb:cols])
        o_ref[pl.ds(pl.multiple_of(i * tb, tb), tb), :] = ot.T.astype(o_ref.dtype)
        return c

    lax.fori_loop(0, nq, finish, 0)


def _attnT_call(moba, q, k, v, kmean, batch, seq, name):
    T, width = q.shape
    pairs = MLA_HEADS // 2
    qw = width // pairs
    nq = seq // ATT_BLOCK
    assert nq == 2 * ATT_CHAINS, "chain pairing (r, nq-1-r) needs nq == 2 * ATT_CHAINS"
    cols = 2 * ATT_BLOCK
    seq_block = lambda w: pl.BlockSpec((seq, w), lambda b, p: (b, p))
    in_specs = [seq_block(qw), seq_block(qw), seq_block(PAIR)]
    scratch = [pltpu.VMEM((nq, cols, qw), BF16),
               pltpu.VMEM((nq, PAIR + ONES_ROWS, ATT_BLOCK), BF16),
               pltpu.VMEM((nq, 1, cols), F32),
               pltpu.VMEM((nq, PAIR + ONES_ROWS, cols), F32)]
    args = [q, k, v]
    if moba:
        in_specs.append(pl.BlockSpec((nq, PAIR), lambda b, p: (b, p)))
        scratch.append(pltpu.VMEM((nq, 16, cols), F32))
        args.append(kmean)
    return pl.pallas_call(
        functools.partial(_attnT_body, moba),
        grid=(batch, pairs),
        in_specs=in_specs,
        out_specs=seq_block(PAIR),
        out_shape=jax.ShapeDtypeStruct((T, pairs * PAIR), BF16),
        scratch_shapes=scratch,
        compiler_params=_cparams(("arbitrary", "arbitrary")),
        name=name,
    )(*args)


def _merge_body(oa_ref, ob_ref, x_ref, gout_ref, wout_ref, gffn_ref, wr_ref, br_ref, tri_ref,
                x1_ref, h2_ref, meta_ref, wts_ref, counts_ref):
    tm = x_ref.shape[0]

    @pl.when(pl.program_id(0) == 0)
    def _():
        counts_ref[...] = jnp.zeros_like(counts_ref)

    oa = oa_ref[...].astype(F32)
    ob = ob_ref[...].astype(F32)
    na = oa * _rms(oa, 512) * gout_ref[:, 0:512]
    nb = ob * _rms(ob, 512) * gout_ref[:, 512:1024]
    mixed = jnp.concatenate([na, nb], axis=1).astype(BF16)
    x1 = x_ref[...] + jnp.dot(mixed, wout_ref[...], preferred_element_type=F32)
    x1_ref[...] = x1
    h2 = x1 * _rms(x1, D_MODEL) * gffn_ref[...]
    h2_ref[...] = h2

    logits = jnp.dot(h2.astype(BF16), wr_ref[...], preferred_element_type=F32) + br_ref[...]
    lane = lax.broadcasted_iota(jnp.int32, (tm, LANES), 1)
    lane_f = lane.astype(F32)
    big = float(LANES)

    def first_max(vals):
        mx = jnp.max(vals, axis=1, keepdims=True)
        idx = jnp.min(jnp.where(vals == mx, lane_f, big), axis=1, keepdims=True)
        return mx, idx

    is_group = (lane >= N_EXPERTS) & (lane < N_EXPERTS + N_GROUPS)
    gl = jnp.where(is_group, logits, NEG_INF)
    gmax, gidx = first_max(gl)
    g_w = 1.0 / jnp.sum(jnp.exp(gl - gmax), axis=1, keepdims=True)
    g_sel = gidx - float(N_EXPERTS)

    in_group = (lane < N_EXPERTS) & ((lane // EPG).astype(F32) == g_sel)
    el = jnp.where(in_group, logits, NEG_INF)
    v1, i1 = first_max(el)
    el2 = jnp.where(lane_f == i1, NEG_INF, el)
    v2, i2 = first_max(el2)
    t = jnp.exp(v2 - v1)
    w1 = g_w * (1.0 / (1.0 + t))
    w2 = g_w * (t / (1.0 + t))

    oh1 = lane_f == i1
    oh2 = lane_f == i2
    onehot = jnp.where(oh1 | oh2, 1.0, 0.0)
    before = jnp.dot(tri_ref[...], onehot.astype(BF16), preferred_element_type=F32) + counts_ref[0:1, :]
    pos1 = jnp.sum(jnp.where(oh1, before, 0.0), axis=1, keepdims=True)
    pos2 = jnp.sum(jnp.where(oh2, before, 0.0), axis=1, keepdims=True)
    counts_ref[...] = counts_ref[...] + jnp.sum(onehot, axis=0, keepdims=True)

    meta = jnp.where(lane == 0, i1, jnp.where(lane == 1, i2, jnp.where(lane == 2, pos1, jnp.where(lane == 3, pos2, 0.0))))
    meta_ref[...] = meta.astype(jnp.int32)
    wts_ref[...] = jnp.where(lane == 0, w1, jnp.where(lane == 1, w2, 0.0))


def _merge_call(oa, ob, x2, gout, wout, gffn, wr, br, tri):
    T = x2.shape[0]
    tm = tri.shape[0]
    const = lambda *shape: pl.BlockSpec(shape, lambda i: (0,) * len(shape))
    rows = lambda w: pl.BlockSpec((tm, w), lambda i: (i, 0))
    return pl.pallas_call(
        _merge_body,
        grid=(T // tm,),
        in_specs=[rows(512), rows(512), rows(D_MODEL), const(1, D_MODEL), const(D_MODEL, D_MODEL),
                  const(1, D_MODEL), const(D_MODEL, LANES), const(1, LANES), const(tm, tm)],
        out_specs=[rows(D_MODEL), rows(D_MODEL), rows(LANES), rows(LANES), const(8, LANES)],
        out_shape=[
            jax.ShapeDtypeStruct((T, D_MODEL), F32),
            jax.ShapeDtypeStruct((T, D_MODEL), F32),
            jax.ShapeDtypeStruct((T, LANES), jnp.int32),
            jax.ShapeDtypeStruct((T, LANES), F32),
            jax.ShapeDtypeStruct((8, LANES), F32),
        ],
        compiler_params=_cparams(("arbitrary",)),
        name="merge",
    )(oa, ob, x2, gout, wout, gffn, wr, br, tri)


def _dispatch_body(dest_ref, h2_ref, xs_in_ref, xs_ref, sem):
    del xs_in_ref
    n_tok = h2_ref.shape[0]

    def start(t, c):
        for k in range(2):
            pltpu.make_async_copy(h2_ref.at[pl.ds(t, 1)],
                                  xs_ref.at[pl.ds(dest_ref[2 * t + k], 1)], sem).start()
        return c

    lax.fori_loop(0, n_tok, start, 0, unroll=8)
    for _ in range(2):
        pltpu.make_async_copy(h2_ref, xs_ref.at[pl.ds(0, n_tok)], sem).wait()


def _dispatch_call(dest, h2, xs_zero, tok_per_step):
    T = h2.shape[0]
    return pl.pallas_call(
        _dispatch_body,
        grid=(T // tok_per_step,),
        in_specs=[
            pl.BlockSpec((2 * tok_per_step,), lambda i: (i,), memory_space=pltpu.SMEM),
            pl.BlockSpec((tok_per_step, D_MODEL), lambda i: (i, 0)),
            pl.BlockSpec(memory_space=pl.ANY),
        ],
        out_specs=pl.BlockSpec(memory_space=pl.ANY),
        out_shape=jax.ShapeDtypeStruct(xs_zero.shape, xs_zero.dtype),
        scratch_shapes=[pltpu.SemaphoreType.DMA(())],
        input_output_aliases={2: 0},
        compiler_params=_cparams(("arbitrary",)),
        name="dispatch",
    )(dest, h2, xs_zero)


def _expert_body(bexp_ref, nused_ref, xs_ref, wg_ref, wu_ref, wd_ref, y_ref):
    del bexp_ref
    used = pl.program_id(0) < nused_ref[0]

    @pl.when(used)
    def _():
        xb = xs_ref[...].astype(BF16)
        a = jnp.dot(xb, wg_ref[...], preferred_element_type=F32)
        u = jnp.dot(xb, wu_ref[...], preferred_element_type=F32)
        act = (a * (1.0 / (1.0 + jnp.exp(-a))) * u).astype(BF16)
        y_ref[...] = jnp.dot(act, wd_ref[...], preferred_element_type=F32)

    @pl.when(jnp.logical_not(used))
    def _():
        y_ref[...] = jnp.zeros_like(y_ref)


def _expert_call(block_expert, n_used, xs, wg, wu, wd):
    P = xs.shape[0]
    n_blocks = P // ROW_BLOCK

    def row_map(i, be, nu):
        return (jnp.minimum(i, nu[0] - 1), 0)

    def w_map(i, be, nu):
        return (be[jnp.minimum(i, nu[0] - 1)], 0, 0)

    grid_spec = pltpu.PrefetchScalarGridSpec(
        num_scalar_prefetch=2,
        grid=(n_blocks,),
        in_specs=[
            pl.BlockSpec((ROW_BLOCK, D_MODEL), row_map),
            pl.BlockSpec((None, D_MODEL, EXPERT_HIDDEN), w_map),
            pl.BlockSpec((None, D_MODEL, EXPERT_HIDDEN), w_map),
            pl.BlockSpec((None, EXPERT_HIDDEN, D_MODEL), w_map),
        ],
        out_specs=pl.BlockSpec((ROW_BLOCK, D_MODEL), lambda i, be, nu: (i, 0)),
    )
    return pl.pallas_call(
        _expert_body,
        grid_spec=grid_spec,
        out_shape=jax.ShapeDtypeStruct((P, D_MODEL), F32),
        compiler_params=_cparams(("arbitrary",)),
        name="experts",
    )(block_expert, n_used, xs, wg, wu, wd)


def _combine_body(dest_ref, x1_ref, wts_ref, ys_ref, o_ref, y0_buf, y1_buf, sem):
    n_tok = x1_ref.shape[0]

    def start(t, c):
        for k, buf in enumerate((y0_buf, y1_buf)):
            pltpu.make_async_copy(ys_ref.at[pl.ds(dest_ref[2 * t + k], 1)],
                                  buf.at[pl.ds(t, 1)], sem).start()
        return c

    lax.fori_loop(0, n_tok, start, 0, unroll=8)
    for buf in (y0_buf, y1_buf):
        pltpu.make_async_copy(ys_ref.at[pl.ds(0, n_tok)], buf, sem).wait()
    w = wts_ref[...]
    o_ref[...] = x1_ref[...] + (w[:, 0:1] * y0_buf[...] + w[:, 1:2] * y1_buf[...])


def _combine_call(dest, x1, wts, ys, tok_per_step):
    T = x1.shape[0]
    return pl.pallas_call(
        _combine_body,
        grid=(T // tok_per_step,),
        in_specs=[
            pl.BlockSpec((2 * tok_per_step,), lambda i: (i,), memory_space=pltpu.SMEM),
            pl.BlockSpec((tok_per_step, D_MODEL), lambda i: (i, 0)),
            pl.BlockSpec((tok_per_step, LANES), lambda i: (i, 0)),
            pl.BlockSpec(memory_space=pl.ANY),
        ],
        out_specs=pl.BlockSpec((tok_per_step, D_MODEL), lambda i: (i, 0)),
        out_shape=jax.ShapeDtypeStruct((T, D_MODEL), F32),
        scratch_shapes=[pltpu.VMEM((tok_per_step, D_MODEL), F32),
                        pltpu.VMEM((tok_per_step, D_MODEL), F32),
                        pltpu.SemaphoreType.DMA(())],
        compiler_params=_cparams(("arbitrary",)),
        name="combine",
    )(dest, x1, wts, ys)


def _layer(x, attn_norm_g, w_in, mla_q_lat_g, w_q_up, mla_kv_lat_g, w_kv_up, mla_q_head_g,
           mla_k_head_g, moba_q_head_g, moba_k_head_g, out_norm_mla_g, out_norm_moba_g, w_out,
           ffn_norm_g, w_router_group, b_router_group, w_router_expert, b_router_expert,
           w_gate, w_up, w_down):
    B, S, D = x.shape
    T = B * S
    x2 = x.reshape(T, D)

    win, wq, wkv, gains = _prep_weights(w_in, w_q_up, w_kv_up, mla_q_head_g, mla_k_head_g,
                                        moba_q_head_g, moba_k_head_g)
    glat = jnp.concatenate([mla_q_lat_g, mla_kv_lat_g])[None, :]
    qa, ka, va, qb, kb, vb, kmean = _proj_call(
        x2, attn_norm_g[None, :], win, glat, wq, wkv, gains, _group_matrices(), _rope_tables(S), S)

    oa = _attnT_call(False, qa, ka, va, None, B, S, "mla")
    ob = _attnT_call(True, qb, kb, vb, kmean.reshape(T // MOBA_BLOCK, 512), B, S, "moba")

    tm = 512
    wr = jnp.zeros((D, LANES), F32).at[:, :N_EXPERTS].set(w_router_expert)
    wr = wr.at[:, N_EXPERTS:N_EXPERTS + N_GROUPS].set(w_router_group).astype(BF16)
    br = jnp.zeros((1, LANES), F32).at[0, :N_EXPERTS].set(b_router_expert)
    br = br.at[0, N_EXPERTS:N_EXPERTS + N_GROUPS].set(b_router_group)
    tri = jnp.asarray(np.tril(np.ones((tm, tm), np.float32), -1), BF16)
    gout = jnp.concatenate([out_norm_mla_g, out_norm_moba_g])[None, :]
    x1, h2, meta, wts, counts = _merge_call(oa, ob, x2, gout, w_out.astype(BF16), ffn_norm_g[None, :],
                                            wr, br, tri)

    counts = counts[0, :N_EXPERTS].astype(jnp.int32)
    padded = (counts + ROW_BLOCK - 1) // ROW_BLOCK * ROW_BLOCK
    pad_end = jnp.cumsum(padded)
    pad_start = pad_end - padded
    n_blocks = (2 * T + ROW_BLOCK - 1) // ROW_BLOCK + N_EXPERTS
    n_used = (pad_end[-1] // ROW_BLOCK).astype(jnp.int32)
    blk = jnp.minimum(jnp.arange(n_blocks, dtype=jnp.int32), n_used - 1) * ROW_BLOCK
    block_expert = jnp.minimum(jnp.sum(pad_end[None, :] <= blk[:, None], axis=1), N_EXPERTS - 1).astype(jnp.int32)
    dest = (jnp.take(pad_start, meta[:, 0:2]) + meta[:, 2:4]).astype(jnp.int32).reshape(2 * T)

    xs = _dispatch_call(dest, h2, jnp.zeros((n_blocks * ROW_BLOCK, D), F32), tm)
    ys = _expert_call(block_expert, n_used[None], xs,
                      w_gate.astype(BF16), w_up.astype(BF16), w_down.astype(BF16))
    out = _combine_call(dest, x1, wts, ys, 256)
    return out.reshape(B, S, D)


def kernel(x, attn_norm_g, w_in, mla_q_lat_g, w_q_up, mla_kv_lat_g, w_kv_up, mla_q_head_g, mla_k_head_g, moba_q_head_g, moba_k_head_g, out_norm_mla_g, out_norm_moba_g, w_out, ffn_norm_g, w_router_group, b_router_group, w_router_expert, b_router_expert, w_gate, w_up, w_down):
    return _layer(x, attn_norm_g[0], w_in[0], mla_q_lat_g[0], w_q_up[0], mla_kv_lat_g[0], w_kv_up[0],
                  mla_q_head_g[0], mla_k_head_g[0], moba_q_head_g[0], moba_k_head_g[0],
                  out_norm_mla_g[0], out_norm_moba_g[0], w_out[0], ffn_norm_g[0],
                  w_router_group[0], b_router_group[0], w_router_expert[0], b_router_expert[0],
                  w_gate[0], w_up[0], w_down[0])
```

```python
import functools

import jax
import jax.numpy as jnp
import numpy as np
from jax import lax
from jax.experimental import pallas as pl
from jax.experimental.pallas import tpu as pltpu

F32 = jnp.float32
BF16 = jnp.bfloat16

D_MODEL = 1024
MLA_HEADS = 8
MLA_NOPE = 64
MLA_ROPE = 32
MLA_QK = MLA_NOPE + MLA_ROPE
MLA_V = 64
MLA_Q_LORA = 256
MLA_KV_LORA = 128
MOBA_HEADS = 8
MOBA_D = 64
MOBA_BLOCK = 256
MOBA_TOPK = 3
ROPE_THETA = 10000.0
EPS = 1e-6
N_GROUPS = 4
EPG = 8
N_EXPERTS = N_GROUPS * EPG
EXPERT_HIDDEN = 256
ROW_BLOCK = 256

LANES = 128
HEAD_SLOT = LANES
PAIR = LANES
PROJ_WIDTH = 2048
NEG_INF = float("-inf")

VMEM_LIMIT = 48 * 1024 * 1024


def _cparams(sem):
    return pltpu.CompilerParams(dimension_semantics=sem, vmem_limit_bytes=VMEM_LIMIT)


def _mla_slot_index():
    idx = -np.ones((HEAD_SLOT,), np.int64)
    idx[0:32] = np.arange(0, 32)
    idx[32:48] = MLA_NOPE + np.arange(0, 16)
    idx[64:96] = np.arange(32, 64)
    idx[96:112] = MLA_NOPE + np.arange(16, 32)
    return idx


def _moba_pair_index():
    head = np.concatenate([np.zeros(32), np.ones(32), np.zeros(32), np.ones(32)]).astype(np.int64)
    feat = np.concatenate([np.arange(32), np.arange(32), 32 + np.arange(32), 32 + np.arange(32)])
    return head, feat


def _gather_cols(w, idx):
    safe = np.where(idx < 0, 0, idx)
    out = jnp.take(w, jnp.asarray(safe), axis=-1)
    return jnp.where(jnp.asarray(idx < 0), 0.0, out)


def _prep_weights(w_in, w_q_up, w_kv_up, q_head_g, k_head_g, mq_g, mk_g):
    slot = _mla_slot_index()
    head, feat = _moba_pair_index()

    kpe_idx = np.where(slot >= MLA_NOPE, slot - MLA_NOPE, -1)
    off_kpe = MLA_Q_LORA + MLA_KV_LORA
    off_mq = off_kpe + MLA_ROPE
    off_mk = off_mq + MOBA_HEADS * MOBA_D
    off_mv = off_mk + MOBA_HEADS * MOBA_D
    moba_idx = np.concatenate([(2 * p + head) * MOBA_D + feat for p in range(MOBA_HEADS // 2)])
    cols = np.concatenate([
        np.arange(0, off_kpe),
        np.where(kpe_idx < 0, -1, off_kpe + kpe_idx),
        off_mq + moba_idx,
        off_mk + moba_idx,
        off_mv + np.arange(MOBA_HEADS * MOBA_D),
    ])
    win = _gather_cols(w_in, cols).astype(BF16)

    q_idx = np.concatenate([np.where(slot < 0, -1, h * MLA_QK + slot) for h in range(MLA_HEADS)])
    wq = _gather_cols(w_q_up, q_idx).astype(BF16)

    nope_slot = np.where((slot >= 0) & (slot < MLA_NOPE), slot, -1)
    kn_idx = np.concatenate([np.where(nope_slot < 0, -1, h * (MLA_NOPE + MLA_V) + nope_slot)
                             for h in range(MLA_HEADS)])
    v_idx = np.concatenate([h * (MLA_NOPE + MLA_V) + MLA_NOPE + np.arange(MLA_V) for h in range(MLA_HEADS)])
    wkv = _gather_cols(w_kv_up, np.concatenate([kn_idx, v_idx])).astype(BF16)

    gq = _gather_cols(q_head_g, slot) * (MLA_QK ** -0.5 * LOG2E)
    gk = _gather_cols(k_head_g, slot)
    gmq = jnp.take(mq_g, jnp.asarray(feat)) * (MOBA_D ** -0.5 * LOG2E)
    gmk = jnp.take(mk_g, jnp.asarray(feat))
    gains = jnp.stack([gq, gk, gmq, gmk]).astype(F32)
    return win, wq, wkv, gains


def _rope_tables(seq):
    def tab(dim):
        inv = ROPE_THETA ** (-(jnp.arange(0, dim, 2, dtype=F32) / dim))
        ang = jnp.arange(seq, dtype=F32)[:, None] * inv[None, :]
        return jnp.cos(ang), jnp.sin(ang)

    cr, sr = tab(MLA_ROPE)
    cf, sf = tab(MOBA_D)
    one = jnp.ones((seq, 16), F32)
    zero = jnp.zeros((seq, 16), F32)
    cos_a = jnp.concatenate([one, one, cr, one, one, one, cr, one], axis=1)
    sin_a = jnp.concatenate([zero, zero, -sr, zero, zero, zero, sr, zero], axis=1)
    cos_b = jnp.concatenate([cf, cf, cf, cf], axis=1)
    sin_b = jnp.concatenate([-sf, -sf, sf, sf], axis=1)
    return jnp.stack([cos_a, sin_a, cos_b, sin_b])


def _group_matrices():
    lane = np.arange(256)
    g_mla = (lane[:, None] // HEAD_SLOT == lane[None, :] // HEAD_SLOT)
    head, _ = _moba_pair_index()
    hid = np.concatenate([head, 2 + head])
    g_moba = hid[:, None] == hid[None, :]
    return jnp.asarray(np.stack([g_mla, g_moba]), BF16)


def _rms(x, width):
    return lax.rsqrt(jnp.sum(x * x, axis=-1, keepdims=True) * (1.0 / width) + EPS)


def _head_norm(t, gmat, dim):
    sq = (t * t).astype(BF16)
    parts = [jnp.dot(sq[:, c:c + 256], gmat, preferred_element_type=F32)
             for c in range(0, t.shape[1], 256)]
    ssum = jnp.concatenate(parts, axis=1)
    return t * lax.rsqrt(ssum * (1.0 / dim) + EPS)


def _rope(t, cos, sin):
    outs = []
    for c in range(0, t.shape[1], LANES):
        xc = t[:, c:c + LANES]
        outs.append(xc * cos + pltpu.roll(xc, 64, 1) * sin)
    return jnp.concatenate(outs, axis=1)


def _proj_body(x_ref, gx_ref, win_ref, glat_ref, wq_ref, wkv_ref, gains_ref, gmat_ref, rope_ref,
               qa_ref, ka_ref, va_ref, qb_ref, kb_ref, vb_ref, kmean_ref):
    x = x_ref[...]
    h = (x * _rms(x, D_MODEL) * gx_ref[...]).astype(BF16)
    proj = jnp.dot(h, win_ref[...], preferred_element_type=F32)

    q_lat = proj[:, 0:256]
    ql = (q_lat * _rms(q_lat, MLA_Q_LORA) * glat_ref[:, 0:256]).astype(BF16)
    qa = jnp.dot(ql, wq_ref[...], preferred_element_type=F32)

    kv_lat = proj[:, 256:384]
    kvl = (kv_lat * _rms(kv_lat, MLA_KV_LORA) * glat_ref[:, 256:384]).astype(BF16)
    kv = jnp.dot(kvl, wkv_ref[...], preferred_element_type=F32)
    kpe = proj[:, 384:512]
    ka = kv[:, 0:1024] + jnp.concatenate([kpe] * MLA_HEADS, axis=1)
    va_ref[...] = kv[:, 1024:1536].astype(BF16)

    gains = gains_ref[...]
    g_mla = gmat_ref[0]
    g_moba = gmat_ref[1]
    cos_a, sin_a, cos_b, sin_b = rope_ref[0], rope_ref[1], rope_ref[2], rope_ref[3]

    def tile_gain(row, n):
        return jnp.concatenate([gains[row:row + 1, :]] * n, axis=1)

    qa_ref[...] = _rope(_head_norm(qa, g_mla, MLA_QK) * tile_gain(0, 8), cos_a, sin_a).astype(BF16)
    ka_ref[...] = _rope(_head_norm(ka, g_mla, MLA_QK) * tile_gain(1, 8), cos_a, sin_a).astype(BF16)

    mq = proj[:, 512:1024]
    mk = proj[:, 1024:1536]
    qb_ref[...] = _rope(_head_norm(mq, g_moba, MOBA_D) * tile_gain(2, 4), cos_b, sin_b).astype(BF16)
    kb = _rope(_head_norm(mk, g_moba, MOBA_D) * tile_gain(3, 4), cos_b, sin_b)
    kb_ref[...] = kb.astype(BF16)
    kmean_ref[0] = jnp.sum(kb, axis=0, keepdims=True) * (1.0 / MOBA_BLOCK)
    vb_ref[...] = proj[:, 1536:2048].astype(BF16)


def _proj_call(x2, gx, win, glat, wq, wkv, gains, gmat, rope, seq):
    T = x2.shape[0]
    tm = MOBA_BLOCK
    nt = T // tm
    spb = seq // tm
    const = lambda *shape: pl.BlockSpec(shape, lambda i: (0,) * len(shape))
    rows = lambda w: pl.BlockSpec((tm, w), lambda i: (i, 0))
    return pl.pallas_call(
        _proj_body,
        grid=(nt,),
        in_specs=[
            rows(D_MODEL),
            const(1, D_MODEL),
            const(D_MODEL, PROJ_WIDTH),
            const(1, 384),
            const(MLA_Q_LORA, 1024),
            const(MLA_KV_LORA, 1536),
            const(4, LANES),
            const(2, 256, 256),
            pl.BlockSpec((4, tm, LANES), lambda i: (0, i % spb, 0)),
        ],
        out_specs=[rows(1024), rows(1024), rows(512), rows(512), rows(512), rows(512),
                   pl.BlockSpec((1, 1, 512), lambda i: (i, 0, 0))],
        out_shape=[
            jax.ShapeDtypeStruct((T, 1024), BF16),
            jax.ShapeDtypeStruct((T, 1024), BF16),
            jax.ShapeDtypeStruct((T, 512), BF16),
            jax.ShapeDtypeStruct((T, 512), BF16),
            jax.ShapeDtypeStruct((T, 512), BF16),
            jax.ShapeDtypeStruct((T, 512), BF16),
            jax.ShapeDtypeStruct((nt, 1, 512), F32),
        ],
        compiler_params=_cparams(("arbitrary",)),
        name="proj",
    )(x2, gx, win, glat, wq, wkv, gains, gmat, rope)


def _nt_dot(a, b):
    return lax.dot_general(a, b, (((1,), (1,)), ((), ())), preferred_element_type=F32)


ATT_BLOCK = MOBA_BLOCK
ATT_CHAINS = 4
ATT_AHEAD = 2
M_INIT = -1e30
ONES_ROWS = 16
LOG2E = 1.4426950408889634


def _attnT_body(moba, q_ref, k_ref, v_ref, *rest):
    if moba:
        km_ref, o_ref, qt_ref, vt_ref, m_ref, acc_ref, sel_ref = rest
    else:
        o_ref, qt_ref, vt_ref, m_ref, acc_ref = rest
    tb = ATT_BLOCK
    cols = 2 * tb
    nq = q_ref.shape[0] // tb
    qw = q_ref.shape[1]
    lane_q = lax.broadcasted_iota(jnp.int32, (tb, qw), 1)

    if moba:
        km = km_ref[...]
        km_hi = km.astype(BF16)
        km_lo = (km - km_hi.astype(F32)).astype(BF16)
        zpad = jnp.zeros((16 - nq, PAIR), BF16)
        km_hi = jnp.concatenate([km_hi, zpad], axis=0)
        km_lo = jnp.concatenate([km_lo, zpad], axis=0)
        blk = lax.broadcasted_iota(jnp.int32, (16, cols), 0)

    def prep(i):
        start = i * tb
        q = q_ref[pl.ds(start, tb), :]
        if moba:
            head_of_lane = (lane_q // 32) % 2
        else:
            head_of_lane = lane_q // HEAD_SLOT
        zero = jnp.zeros_like(q)
        qs = jnp.concatenate([jnp.where(head_of_lane == 0, q, zero),
                              jnp.where(head_of_lane == 1, q, zero)], axis=0)
        qt = qs.astype(F32).T.astype(BF16)
        qt_ref[i] = qt
        vt = v_ref[pl.ds(start, tb), :].astype(F32).T.astype(BF16)
        vt_ref[i] = jnp.concatenate([vt, jnp.ones((ONES_ROWS, tb), BF16)], axis=0)
        m_ref[i] = jnp.full((1, cols), M_INIT, F32)
        acc_ref[i] = jnp.zeros((PAIR + ONES_ROWS, cols), F32)
        if moba:
            gate = (jnp.dot(km_hi, qt, preferred_element_type=F32)
                    + jnp.dot(km_lo, qt, preferred_element_type=F32))
            cnt = jnp.zeros((16, cols), F32)
            for b in range(i):
                gb = gate[b:b + 1, :]
                beats = (gb > gate) | ((gb == gate) & (blk > b))
                cnt = cnt + jnp.where(beats, 1.0, 0.0)
            picked = (blk < i) & (cnt < float(MOBA_TOPK))
            sel_ref[i] = jnp.where(picked, 1.0, 0.0)

    key = lax.broadcasted_iota(jnp.int32, (tb, cols), 0)
    qry = lax.broadcasted_iota(jnp.int32, (tb, cols), 1) % tb
    causal = key <= qry

    def scores(i, j):
        return jnp.dot(k_ref[pl.ds(j * tb, tb), :], qt_ref[i],
                       preferred_element_type=F32)

    def item(i, j, s):
        m_prev = m_ref[i]
        if i == j:
            s = jnp.where(causal, s, NEG_INF)
            m_new = jnp.maximum(m_prev, jnp.max(s, axis=0, keepdims=True))
            m_sub = m_new
        else:
            m_new = jnp.maximum(m_prev, jnp.max(s, axis=0, keepdims=True))
            m_sub = m_new
            if moba:
                chosen = sel_ref[i, pl.ds(j, 1), :] > 0.5
                m_new = jnp.where(chosen, m_new, m_prev)
                m_sub = jnp.where(chosen, m_new, -M_INIT)
        alpha = jnp.exp2(m_prev - m_new)
        p = jnp.exp2(s - m_sub).astype(BF16)
        m_ref[i] = m_new
        acc_ref[i] = alpha * acc_ref[i] + jnp.dot(vt_ref[j], p, preferred_element_type=F32)

    def finish(i):
        acc = acc_ref[i]
        o = acc[0:PAIR, :] / acc[PAIR:PAIR + 1, :]
        vrow = lax.broadcasted_iota(jnp.int32, (PAIR, tb), 0)
        ot = jnp.where(vrow < PAIR // 2, o[:, 0:tb], o[:, tb:cols])
        o_ref[pl.ds(i * tb, tb), :] = ot.T.astype(o_ref.dtype)

    for i in range(nq):
        prep(i)
    order = [(r, t) if t <= r else (nq - 1 - r, t - (r + 1))
             for t in range(nq + 1) for r in range(ATT_CHAINS)]
    pending = [scores(*order[n]) for n in range(ATT_AHEAD)]
    for n, (i, j) in enumerate(order):
        if n + ATT_AHEAD < len(order):
            pending.append(scores(*order[n + ATT_AHEAD]))
        item(i, j, pending.pop(0))
        if i == j:
            finish(i)


def _attnT_call(moba, q, k, v, kmean, batch, seq, name):
    T, width = q.shape
    pairs = MLA_HEADS // 2
    qw = width // pairs
    nq = seq // ATT_BLOCK
    assert nq == 2 * ATT_CHAINS, "chain pairing (r, nq-1-r) needs nq == 2 * ATT_CHAINS"
    cols = 2 * ATT_BLOCK
    seq_block = lambda w: pl.BlockSpec((seq, w), lambda b, p: (b, p))
    in_specs = [seq_block(qw), seq_block(qw), seq_block(PAIR)]
    scratch = [pltpu.VMEM((nq, qw, cols), BF16),
               pltpu.VMEM((nq, PAIR + ONES_ROWS, ATT_BLOCK), BF16),
               pltpu.VMEM((nq, 1, cols), F32),
               pltpu.VMEM((nq, PAIR + ONES_ROWS, cols), F32)]
    args = [q, k, v]
    if moba:
        in_specs.append(pl.BlockSpec((nq, PAIR), lambda b, p: (b, p)))
        scratch.append(pltpu.VMEM((nq, 16, cols), F32))
        args.append(kmean)
    return pl.pallas_call(
        functools.partial(_attnT_body, moba),
        grid=(batch, pairs),
        in_specs=in_specs,
        out_specs=seq_block(PAIR),
        out_shape=jax.ShapeDtypeStruct((T, pairs * PAIR), BF16),
        scratch_shapes=scratch,
        compiler_params=_cparams(("arbitrary", "arbitrary")),
        name=name,
    )(*args)


def _merge_body(oa_ref, ob_ref, x_ref, gout_ref, wout_ref, gffn_ref, wr_ref, br_ref, tri_ref,
                x1_ref, h2_ref, meta_ref, wts_ref, counts_ref):
    tm = x_ref.shape[0]

    @pl.when(pl.program_id(0) == 0)
    def _():
        counts_ref[...] = jnp.zeros_like(counts_ref)

    oa = oa_ref[...].astype(F32)
    ob = ob_ref[...].astype(F32)
    na = oa * _rms(oa, 512) * gout_ref[:, 0:512]
    nb = ob * _rms(ob, 512) * gout_ref[:, 512:1024]
    mixed = jnp.concatenate([na, nb], axis=1).astype(BF16)
    x1 = x_ref[...] + jnp.dot(mixed, wout_ref[...], preferred_element_type=F32)
    x1_ref[...] = x1
    h2 = x1 * _rms(x1, D_MODEL) * gffn_ref[...]
    half = D_MODEL // 2
    lo = pltpu.bitcast(h2[:, 0:half].astype(BF16).astype(F32), jnp.uint32)
    hi = pltpu.bitcast(h2[:, half:D_MODEL].astype(BF16).astype(F32), jnp.uint32)
    h2_ref[...] = (hi & jnp.uint32(0xFFFF0000)) | (lo >> 16)

    logits = jnp.dot(h2.astype(BF16), wr_ref[...], preferred_element_type=F32) + br_ref[...]
    lane = lax.broadcasted_iota(jnp.int32, (tm, LANES), 1)
    lane_f = lane.astype(F32)
    big = float(LANES)

    def first_max(vals):
        mx = jnp.max(vals, axis=1, keepdims=True)
        idx = jnp.min(jnp.where(vals == mx, lane_f, big), axis=1, keepdims=True)
        return mx, idx

    is_group = (lane >= N_EXPERTS) & (lane < N_EXPERTS + N_GROUPS)
    gl = jnp.where(is_group, logits, NEG_INF)
    gmax, gidx = first_max(gl)
    g_w = 1.0 / jnp.sum(jnp.exp(gl - gmax), axis=1, keepdims=True)
    g_sel = gidx - float(N_EXPERTS)

    in_group = (lane < N_EXPERTS) & ((lane // EPG).astype(F32) == g_sel)
    el = jnp.where(in_group, logits, NEG_INF)
    v1, i1 = first_max(el)
    el2 = jnp.where(lane_f == i1, NEG_INF, el)
    v2, i2 = first_max(el2)
    t = jnp.exp(v2 - v1)
    w1 = g_w * (1.0 / (1.0 + t))
    w2 = g_w * (t / (1.0 + t))

    oh1 = lane_f == i1
    oh2 = lane_f == i2
    onehot = jnp.where(oh1 | oh2, 1.0, 0.0)
    before = jnp.dot(tri_ref[...], onehot.astype(BF16), preferred_element_type=F32) + counts_ref[0:1, :]
    pos1 = jnp.sum(jnp.where(oh1, before, 0.0), axis=1, keepdims=True)
    pos2 = jnp.sum(jnp.where(oh2, before, 0.0), axis=1, keepdims=True)
    counts_ref[...] = counts_ref[...] + jnp.sum(onehot, axis=0, keepdims=True)

    meta = jnp.where(lane == 0, i1, jnp.where(lane == 1, i2, jnp.where(lane == 2, pos1, jnp.where(lane == 3, pos2, 0.0))))
    meta_ref[...] = meta.astype(jnp.int32)
    wts_ref[...] = jnp.where(lane == 0, w1, jnp.where(lane == 1, w2, 0.0))


def _merge_call(oa, ob, x2, gout, wout, gffn, wr, br, tri):
    T = x2.shape[0]
    tm = tri.shape[0]
    const = lambda *shape: pl.BlockSpec(shape, lambda i: (0,) * len(shape))
    rows = lambda w: pl.BlockSpec((tm, w), lambda i: (i, 0))
    return pl.pallas_call(
        _merge_body,
        grid=(T // tm,),
        in_specs=[rows(512), rows(512), rows(D_MODEL), const(1, D_MODEL), const(D_MODEL, D_MODEL),
                  const(1, D_MODEL), const(D_MODEL, LANES), const(1, LANES), const(tm, tm)],
        out_specs=[rows(D_MODEL), rows(D_MODEL // 2), rows(LANES), rows(LANES), const(8, LANES)],
        out_shape=[
            jax.ShapeDtypeStruct((T, D_MODEL), F32),
            jax.ShapeDtypeStruct((T, D_MODEL // 2), jnp.uint32),
            jax.ShapeDtypeStruct((T, LANES), jnp.int32),
            jax.ShapeDtypeStruct((T, LANES), F32),
            jax.ShapeDtypeStruct((8, LANES), F32),
        ],
        compiler_params=_cparams(("arbitrary",)),
        name="merge",
    )(oa, ob, x2, gout, wout, gffn, wr, br, tri)


def _dispatch_body(dest_ref, h2_ref, xs_in_ref, xs_ref, sem):
    del xs_in_ref
    n_tok = h2_ref.shape[0]

    def start(t, c):
        for k in range(2):
            pltpu.make_async_copy(h2_ref.at[pl.ds(t, 1)],
                                  xs_ref.at[pl.ds(dest_ref[2 * t + k], 1)], sem).start()
        return c

    lax.fori_loop(0, n_tok, start, 0, unroll=8)
    for _ in range(2):
        pltpu.make_async_copy(h2_ref, xs_ref.at[pl.ds(0, n_tok)], sem).wait()


def _dispatch_call(dest, h2, xs_zero, tok_per_step):
    T = h2.shape[0]
    return pl.pallas_call(
        _dispatch_body,
        grid=(T // tok_per_step,),
        in_specs=[
            pl.BlockSpec((2 * tok_per_step,), lambda i: (i,), memory_space=pltpu.SMEM),
            pl.BlockSpec((tok_per_step, h2.shape[1]), lambda i: (i, 0)),
            pl.BlockSpec(memory_space=pl.ANY),
        ],
        out_specs=pl.BlockSpec(memory_space=pl.ANY),
        out_shape=jax.ShapeDtypeStruct(xs_zero.shape, xs_zero.dtype),
        scratch_shapes=[pltpu.SemaphoreType.DMA(())],
        input_output_aliases={2: 0},
        compiler_params=_cparams(("arbitrary",)),
        name="dispatch",
    )(dest, h2, xs_zero)


def _expert_body(bexp_ref, nused_ref, xs_ref, wg_ref, wu_ref, wd_ref, y_ref):
    del bexp_ref
    used = pl.program_id(0) < nused_ref[0]

    @pl.when(used)
    def _():
        w = xs_ref[...]
        lo = pltpu.bitcast(w << 16, F32)
        hi = pltpu.bitcast(w & jnp.uint32(0xFFFF0000), F32)
        xb = jnp.concatenate([lo, hi], axis=1).astype(BF16)
        a = jnp.dot(xb, wg_ref[...], preferred_element_type=F32)
        u = jnp.dot(xb, wu_ref[...], preferred_element_type=F32)
        act = (a * (1.0 / (1.0 + jnp.exp(-a))) * u).astype(BF16)
        y_ref[...] = jnp.dot(act, wd_ref[...], preferred_element_type=F32)

    @pl.when(jnp.logical_not(used))
    def _():
        y_ref[...] = jnp.zeros_like(y_ref)


def _expert_call(block_expert, n_used, xs, wg, wu, wd):
    P = xs.shape[0]
    n_blocks = P // ROW_BLOCK

    def row_map(i, be, nu):
        return (jnp.minimum(i, nu[0] - 1), 0)

    def w_map(i, be, nu):
        return (be[jnp.minimum(i, nu[0] - 1)], 0, 0)

    grid_spec = pltpu.PrefetchScalarGridSpec(
        num_scalar_prefetch=2,
        grid=(n_blocks,),
        in_specs=[
            pl.BlockSpec((ROW_BLOCK, D_MODEL // 2), row_map),
            pl.BlockSpec((None, D_MODEL, EXPERT_HIDDEN), w_map),
            pl.BlockSpec((None, D_MODEL, EXPERT_HIDDEN), w_map),
            pl.BlockSpec((None, EXPERT_HIDDEN, D_MODEL), w_map),
        ],
        out_specs=pl.BlockSpec((ROW_BLOCK, D_MODEL), lambda i, be, nu: (i, 0)),
    )
    return pl.pallas_call(
        _expert_body,
        grid_spec=grid_spec,
        out_shape=jax.ShapeDtypeStruct((P, D_MODEL), F32),
        compiler_params=_cparams(("arbitrary",)),
        name="experts",
    )(block_expert, n_used, xs, wg, wu, wd)


def _combine_body(dest_ref, dest_next_ref, x1_ref, wts_ref, ys_ref, o_ref, y_buf, sems):
    n_tok = x1_ref.shape[0]
    step = pl.program_id(0)
    slot = step % 2

    def gather(idx_ref, to_slot):
        def start(t, c):
            for k in range(2):
                pltpu.make_async_copy(ys_ref.at[pl.ds(idx_ref[2 * t + k], 1)],
                                      y_buf.at[to_slot, k, pl.ds(t, 1)], sems.at[to_slot]).start()
            return c
        lax.fori_loop(0, n_tok, start, 0, unroll=8)

    @pl.when(step == 0)
    def _():
        gather(dest_ref, 0)

    @pl.when(step + 1 < pl.num_programs(0))
    def _():
        gather(dest_next_ref, 1 - slot)

    for k in range(2):
        pltpu.make_async_copy(ys_ref.at[pl.ds(0, n_tok)], y_buf.at[slot, k], sems.at[slot]).wait()
    w = wts_ref[...]
    o_ref[...] = x1_ref[...] + (w[:, 0:1] * y_buf[slot, 0] + w[:, 1:2] * y_buf[slot, 1])


def _combine_call(dest, x1, wts, ys, tok_per_step):
    T = x1.shape[0]
    n_steps = T // tok_per_step
    return pl.pallas_call(
        _combine_body,
        grid=(n_steps,),
        in_specs=[
            pl.BlockSpec((2 * tok_per_step,), lambda i: (i,), memory_space=pltpu.SMEM),
            pl.BlockSpec((2 * tok_per_step,), lambda i: (jnp.minimum(i + 1, n_steps - 1),),
                         memory_space=pltpu.SMEM),
            pl.BlockSpec((tok_per_step, D_MODEL), lambda i: (i, 0)),
            pl.BlockSpec((tok_per_step, LANES), lambda i: (i, 0)),
            pl.BlockSpec(memory_space=pl.ANY),
        ],
        out_specs=pl.BlockSpec((tok_per_step, D_MODEL), lambda i: (i, 0)),
        out_shape=jax.ShapeDtypeStruct((T, D_MODEL), F32),
        scratch_shapes=[pltpu.VMEM((2, 2, tok_per_step, D_MODEL), F32),
                        pltpu.SemaphoreType.DMA((2,))],
        compiler_params=_cparams(("arbitrary",)),
        name="combine",
    )(dest, dest, x1, wts, ys)


def _layer(x, attn_norm_g, w_in, mla_q_lat_g, w_q_up, mla_kv_lat_g, w_kv_up, mla_q_head_g,
           mla_k_head_g, moba_q_head_g, moba_k_head_g, out_norm_mla_g, out_norm_moba_g, w_out,
           ffn_norm_g, w_router_group, b_router_group, w_router_expert, b_router_expert,
           w_gate, w_up, w_down):
    B, S, D = x.shape
    T = B * S
    x2 = x.reshape(T, D)

    win, wq, wkv, gains = _prep_weights(w_in, w_q_up, w_kv_up, mla_q_head_g, mla_k_head_g,
                                        moba_q_head_g, moba_k_head_g)
    glat = jnp.concatenate([mla_q_lat_g, mla_kv_lat_g])[None, :]
    qa, ka, va, qb, kb, vb, kmean = _proj_call(
        x2, attn_norm_g[None, :], win, glat, wq, wkv, gains, _group_matrices(), _rope_tables(S), S)

    oa = _attnT_call(False, qa, ka, va, None, B, S, "mla")
    ob = _attnT_call(True, qb, kb, vb, kmean.reshape(T // MOBA_BLOCK, 512), B, S, "moba")

    tm = 512
    wr = jnp.zeros((D, LANES), F32).at[:, :N_EXPERTS].set(w_router_expert)
    wr = wr.at[:, N_EXPERTS:N_EXPERTS + N_GROUPS].set(w_router_group).astype(BF16)
    br = jnp.zeros((1, LANES), F32).at[0, :N_EXPERTS].set(b_router_expert)
    br = br.at[0, N_EXPERTS:N_EXPERTS + N_GROUPS].set(b_router_group)
    tri = jnp.asarray(np.tril(np.ones((tm, tm), np.float32), -1), BF16)
    gout = jnp.concatenate([out_norm_mla_g, out_norm_moba_g])[None, :]
    x1, h2, meta, wts, counts = _merge_call(oa, ob, x2, gout, w_out.astype(BF16), ffn_norm_g[None, :],
                                            wr, br, tri)

    counts = counts[0, :N_EXPERTS].astype(jnp.int32)
    padded = (counts + ROW_BLOCK - 1) // ROW_BLOCK * ROW_BLOCK
    pad_end = jnp.cumsum(padded)
    pad_start = pad_end - padded
    n_blocks = (2 * T + ROW_BLOCK - 1) // ROW_BLOCK + N_EXPERTS
    n_used = (pad_end[-1] // ROW_BLOCK).astype(jnp.int32)
    blk = jnp.minimum(jnp.arange(n_blocks, dtype=jnp.int32), n_used - 1) * ROW_BLOCK
    block_expert = jnp.minimum(jnp.sum(pad_end[None, :] <= blk[:, None], axis=1), N_EXPERTS - 1).astype(jnp.int32)
    dest = (jnp.take(pad_start, meta[:, 0:2]) + meta[:, 2:4]).astype(jnp.int32).reshape(2 * T)

    xs = _dispatch_call(dest, h2, jnp.zeros((n_blocks * ROW_BLOCK, D // 2), jnp.uint32), tm)
    ys = _expert_call(block_expert, n_used[None], xs,
                      w_gate.astype(BF16), w_up.astype(BF16), w_down.astype(BF16))
    out = _combine_call(dest, x1, wts, ys, 256)
    return out.reshape(B, S, D)


def kernel(x, attn_norm_g, w_in, mla_q_lat_g, w_q_up, mla_kv_lat_g, w_kv_up, mla_q_head_g, mla_k_head_g, moba_q_head_g, moba_k_head_g, out_norm_mla_g, out_norm_moba_g, w_out, ffn_norm_g, w_router_group, b_router_group, w_router_expert, b_router_expert, w_gate, w_up, w_down):
    return _layer(x, attn_norm_g[0], w_in[0], mla_q_lat_g[0], w_q_up[0], mla_kv_lat_g[0], w_kv_up[0],
                  mla_q_head_g[0], mla_k_head_g[0], moba_q_head_g[0], moba_k_head_g[0],
                  out_norm_mla_g[0], out_norm_moba_g[0], w_out[0], ffn_norm_g[0],
                  w_router_group[0], b_router_group[0], w_router_expert[0], b_router_expert[0],
                  w_gate[0], w_up[0], w_down[0])
```

```python
import functools

import jax
import jax.numpy as jnp
import numpy as np
from jax import lax
from jax.experimental import pallas as pl
from jax.experimental.pallas import tpu as pltpu

F32 = jnp.float32
BF16 = jnp.bfloat16

D_MODEL = 1024
MLA_HEADS = 8
MLA_NOPE = 64
MLA_ROPE = 32
MLA_QK = MLA_NOPE + MLA_ROPE
MLA_V = 64
MLA_Q_LORA = 256
MLA_KV_LORA = 128
MOBA_HEADS = 8
MOBA_D = 64
MOBA_BLOCK = 256
MOBA_TOPK = 3
ROPE_THETA = 10000.0
EPS = 1e-6
N_GROUPS = 4
EPG = 8
N_EXPERTS = N_GROUPS * EPG
EXPERT_HIDDEN = 256
ROW_BLOCK = 256
PAIRS_PER_GROUP = EPG * (EPG - 1) // 2
N_CLASSES = N_GROUPS * PAIRS_PER_GROUP
ROW_WORDS = D_MODEL // 2 + 128

LANES = 128
HEAD_SLOT = LANES
PAIR = LANES
PROJ_WIDTH = 2048
PROJ_ROWS = 1024
NEG_INF = float("-inf")

VMEM_LIMIT = 48 * 1024 * 1024


def _cparams(sem):
    return pltpu.CompilerParams(dimension_semantics=sem, vmem_limit_bytes=VMEM_LIMIT)


def _mla_slot_index():
    idx = -np.ones((HEAD_SLOT,), np.int64)
    idx[0:32] = np.arange(0, 32)
    idx[32:48] = MLA_NOPE + np.arange(0, 16)
    idx[64:96] = np.arange(32, 64)
    idx[96:112] = MLA_NOPE + np.arange(16, 32)
    return idx


def _moba_pair_index():
    head = np.concatenate([np.zeros(32), np.ones(32), np.zeros(32), np.ones(32)]).astype(np.int64)
    feat = np.concatenate([np.arange(32), np.arange(32), 32 + np.arange(32), 32 + np.arange(32)])
    return head, feat


def _gather_cols(w, idx):
    safe = np.where(idx < 0, 0, idx)
    out = jnp.take(w, jnp.asarray(safe), axis=-1)
    return jnp.where(jnp.asarray(idx < 0), 0.0, out)


def _prep_weights(w_in, w_q_up, w_kv_up, q_head_g, k_head_g, mq_g, mk_g):
    slot = _mla_slot_index()
    head, feat = _moba_pair_index()

    kpe_idx = np.where(slot >= MLA_NOPE, slot - MLA_NOPE, -1)
    off_kpe = MLA_Q_LORA + MLA_KV_LORA
    off_mq = off_kpe + MLA_ROPE
    off_mk = off_mq + MOBA_HEADS * MOBA_D
    off_mv = off_mk + MOBA_HEADS * MOBA_D
    moba_idx = np.concatenate([(2 * p + head) * MOBA_D + feat for p in range(MOBA_HEADS // 2)])
    cols = np.concatenate([
        np.arange(0, off_kpe),
        np.where(kpe_idx < 0, -1, off_kpe + kpe_idx),
        off_mq + moba_idx,
        off_mk + moba_idx,
        off_mv + np.arange(MOBA_HEADS * MOBA_D),
    ])
    win = _gather_cols(w_in, cols).astype(BF16)

    q_idx = np.concatenate([np.where(slot < 0, -1, h * MLA_QK + slot) for h in range(MLA_HEADS)])
    wq = _gather_cols(w_q_up, q_idx).astype(BF16)

    nope_slot = np.where((slot >= 0) & (slot < MLA_NOPE), slot, -1)
    kn_idx = np.concatenate([np.where(nope_slot < 0, -1, h * (MLA_NOPE + MLA_V) + nope_slot)
                             for h in range(MLA_HEADS)])
    v_idx = np.concatenate([h * (MLA_NOPE + MLA_V) + MLA_NOPE + np.arange(MLA_V) for h in range(MLA_HEADS)])
    wkv = _gather_cols(w_kv_up, np.concatenate([kn_idx, v_idx])).astype(BF16)

    gq = _gather_cols(q_head_g, slot) * (MLA_QK ** -0.5 * LOG2E)
    gk = _gather_cols(k_head_g, slot)
    gmq = jnp.take(mq_g, jnp.asarray(feat)) * (MOBA_D ** -0.5 * LOG2E)
    gmk = jnp.take(mk_g, jnp.asarray(feat))
    gains = jnp.stack([gq, gk, gmq, gmk]).astype(F32)
    return win, wq, wkv, gains


def _rope_tables(seq):
    def tab(dim):
        inv = ROPE_THETA ** (-(jnp.arange(0, dim, 2, dtype=F32) / dim))
        ang = jnp.arange(seq, dtype=F32)[:, None] * inv[None, :]
        return jnp.cos(ang), jnp.sin(ang)

    cr, sr = tab(MLA_ROPE)
    cf, sf = tab(MOBA_D)
    one = jnp.ones((seq, 16), F32)
    zero = jnp.zeros((seq, 16), F32)
    cos_a = jnp.concatenate([one, one, cr, one, one, one, cr, one], axis=1)
    sin_a = jnp.concatenate([zero, zero, -sr, zero, zero, zero, sr, zero], axis=1)
    cos_b = jnp.concatenate([cf, cf, cf, cf], axis=1)
    sin_b = jnp.concatenate([-sf, -sf, sf, sf], axis=1)
    return jnp.stack([cos_a, sin_a, cos_b, sin_b])


def _group_matrices():
    lane = np.arange(256)
    g_mla = (lane[:, None] // HEAD_SLOT == lane[None, :] // HEAD_SLOT)
    head, _ = _moba_pair_index()
    hid = np.concatenate([head, 2 + head])
    g_moba = hid[:, None] == hid[None, :]
    return jnp.asarray(np.stack([g_mla, g_moba]), BF16)


def _rms(x, width):
    return lax.rsqrt(jnp.sum(x * x, axis=-1, keepdims=True) * (1.0 / width) + EPS)


def _rope(t, cos, sin):
    outs = []
    for c in range(0, t.shape[1], LANES):
        xc = t[:, c:c + LANES]
        outs.append(xc * cos + pltpu.roll(xc, 64, 1) * sin)
    return jnp.concatenate(outs, axis=1)


def _proj_body(x_ref, gx_ref, win_ref, glat_ref, wq_ref, wkv_ref, gains_ref, gmat_ref, rope_ref,
               qa_ref, ka_ref, va_ref, qb_ref, kb_ref, vb_ref, kmean_ref):
    n_chunks = x_ref.shape[0] // MOBA_BLOCK
    gains = gains_ref[...]
    g_mla = gmat_ref[0]
    g_moba = gmat_ref[1]

    def tile_gain(row, n):
        return jnp.concatenate([gains[row:row + 1, :]] * n, axis=1)

    def rows(c):
        return pl.ds(c * MOBA_BLOCK, MOBA_BLOCK)

    def group_sums(sq, gmat):
        return jnp.concatenate([jnp.dot(sq[:, c:c + 256], gmat, preferred_element_type=F32)
                                for c in range(0, sq.shape[1], 256)], axis=1)

    def finish(t, ssum, dim, gain, cos, sin):
        return _rope(t * lax.rsqrt(ssum * (1.0 / dim) + EPS) * gain, cos, sin)

    st = [dict() for _ in range(n_chunks)]

    def stage_a(c):
        x = x_ref[rows(c), :]
        st[c]["h"] = (x * _rms(x, D_MODEL) * gx_ref[...]).astype(BF16)

    def stage_b(c):
        st[c]["proj"] = jnp.dot(st[c].pop("h"), win_ref[...], preferred_element_type=F32)

    def stage_c(c):
        proj = st[c].pop("proj")
        q_lat = proj[:, 0:256]
        kv_lat = proj[:, 256:384]
        st[c]["ql"] = (q_lat * _rms(q_lat, MLA_Q_LORA) * glat_ref[:, 0:256]).astype(BF16)
        st[c]["kvl"] = (kv_lat * _rms(kv_lat, MLA_KV_LORA) * glat_ref[:, 256:384]).astype(BF16)
        st[c]["kpe"] = proj[:, 384:512]
        st[c]["mq"] = proj[:, 512:1024]
        st[c]["mk"] = proj[:, 1024:1536]
        st[c]["mq2"] = (st[c]["mq"] * st[c]["mq"]).astype(BF16)
        st[c]["mk2"] = (st[c]["mk"] * st[c]["mk"]).astype(BF16)
        vb_ref[rows(c), :] = proj[:, 1536:2048].astype(BF16)

    def stage_d(c):
        st[c]["qa"] = jnp.dot(st[c].pop("ql"), wq_ref[...], preferred_element_type=F32)
        st[c]["kv"] = jnp.dot(st[c].pop("kvl"), wkv_ref[...], preferred_element_type=F32)
        st[c]["mq_ss"] = group_sums(st[c].pop("mq2"), g_moba)
        st[c]["mk_ss"] = group_sums(st[c].pop("mk2"), g_moba)

    def stage_e(c):
        cos_b, sin_b = rope_ref[2, rows(c), :], rope_ref[3, rows(c), :]
        qb_ref[rows(c), :] = finish(st[c].pop("mq"), st[c].pop("mq_ss"), MOBA_D,
                                    tile_gain(2, 4), cos_b, sin_b).astype(BF16)
        kb = finish(st[c].pop("mk"), st[c].pop("mk_ss"), MOBA_D, tile_gain(3, 4), cos_b, sin_b)
        kb_ref[rows(c), :] = kb.astype(BF16)
        kmean_ref[c] = jnp.sum(kb, axis=0, keepdims=True) * (1.0 / MOBA_BLOCK)
        kv = st[c].pop("kv")
        va_ref[rows(c), :] = kv[:, 1024:1536].astype(BF16)
        st[c]["ka"] = kv[:, 0:1024] + jnp.concatenate([st[c].pop("kpe")] * MLA_HEADS, axis=1)
        st[c]["qa2"] = (st[c]["qa"] * st[c]["qa"]).astype(BF16)
        st[c]["ka2"] = (st[c]["ka"] * st[c]["ka"]).astype(BF16)

    def stage_f(c):
        st[c]["qa_ss"] = group_sums(st[c].pop("qa2"), g_mla)
        st[c]["ka_ss"] = group_sums(st[c].pop("ka2"), g_mla)

    def stage_g(c):
        cos_a, sin_a = rope_ref[0, rows(c), :], rope_ref[1, rows(c), :]
        qa_ref[rows(c), :] = finish(st[c].pop("qa"), st[c].pop("qa_ss"), MLA_QK,
                                    tile_gain(0, 8), cos_a, sin_a).astype(BF16)
        ka_ref[rows(c), :] = finish(st[c].pop("ka"), st[c].pop("ka_ss"), MLA_QK,
                                    tile_gain(1, 8), cos_a, sin_a).astype(BF16)

    def run(stage, c):
        if 0 <= c < n_chunks:
            stage(c)

    stage_a(0)
    for s in range(n_chunks + 2):
        run(stage_f, s - 2)
        run(stage_c, s - 1)
        run(stage_d, s - 1)
        run(stage_b, s)
        run(stage_e, s - 1)
        run(stage_g, s - 2)
        run(stage_a, s + 1)


def _proj_call(x2, gx, win, glat, wq, wkv, gains, gmat, rope, seq):
    T = x2.shape[0]
    tm = PROJ_ROWS
    nt = T // tm
    spb = seq // tm
    bpt = tm // MOBA_BLOCK
    const = lambda *shape: pl.BlockSpec(shape, lambda i: (0,) * len(shape))
    rows = lambda w: pl.BlockSpec((tm, w), lambda i: (i, 0))
    return pl.pallas_call(
        _proj_body,
        grid=(nt,),
        in_specs=[
            rows(D_MODEL),
            const(1, D_MODEL),
            const(D_MODEL, PROJ_WIDTH),
            const(1, 384),
            const(MLA_Q_LORA, 1024),
            const(MLA_KV_LORA, 1536),
            const(4, LANES),
            const(2, 256, 256),
            pl.BlockSpec((4, tm, LANES), lambda i: (0, i % spb, 0)),
        ],
        out_specs=[rows(1024), rows(1024), rows(512), rows(512), rows(512), rows(512),
                   pl.BlockSpec((bpt, 1, 512), lambda i: (i, 0, 0))],
        out_shape=[
            jax.ShapeDtypeStruct((T, 1024), BF16),
            jax.ShapeDtypeStruct((T, 1024), BF16),
            jax.ShapeDtypeStruct((T, 512), BF16),
            jax.ShapeDtypeStruct((T, 512), BF16),
            jax.ShapeDtypeStruct((T, 512), BF16),
            jax.ShapeDtypeStruct((T, 512), BF16),
            jax.ShapeDtypeStruct((nt * bpt, 1, 512), F32),
        ],
        compiler_params=_cparams(("arbitrary",)),
        name="proj",
    )(x2, gx, win, glat, wq, wkv, gains, gmat, rope)


def _nt_dot(a, b):
    return lax.dot_general(a, b, (((1,), (1,)), ((), ())), preferred_element_type=F32)


ATT_BLOCK = MOBA_BLOCK
ATT_CHAINS = 4
ATT_AHEAD = 2
M_INIT = -1e30
ONES_ROWS = 16
LOG2E = 1.4426950408889634


def _attnT_body(moba, q_ref, k_ref, v_ref, *rest):
    if moba:
        km_ref, o_ref, qt_ref, vt_ref, m_ref, acc_ref, sel_ref = rest
    else:
        o_ref, qt_ref, vt_ref, m_ref, acc_ref = rest
    tb = ATT_BLOCK
    cols = 2 * tb
    nq = q_ref.shape[0] // tb
    qw = q_ref.shape[1]
    lane_q = lax.broadcasted_iota(jnp.int32, (tb, qw), 1)

    if moba:
        km = km_ref[...]
        km_hi = km.astype(BF16)
        km_lo = (km - km_hi.astype(F32)).astype(BF16)
        zpad = jnp.zeros((16 - nq, PAIR), BF16)
        km_hi = jnp.concatenate([km_hi, zpad], axis=0)
        km_lo = jnp.concatenate([km_lo, zpad], axis=0)
        blk = lax.broadcasted_iota(jnp.int32, (16, cols), 0)

    def prep(i):
        start = i * tb
        q = q_ref[pl.ds(start, tb), :]
        if moba:
            head_of_lane = (lane_q // 32) % 2
            zero = jnp.zeros_like(q)
            per_head = [jnp.where(head_of_lane == h, q, zero) for h in range(2)]
        else:
            per_head = [q[:, h * HEAD_SLOT:(h + 1) * HEAD_SLOT] for h in range(2)]
        qts = [qh.astype(F32).T.astype(BF16) for qh in per_head]
        for h in range(2):
            qt_ref[i, h] = qts[h]
        vt = v_ref[pl.ds(start, tb), :].astype(F32).T.astype(BF16)
        vt_ref[i] = jnp.concatenate([vt, jnp.ones((ONES_ROWS, tb), BF16)], axis=0)
        m_ref[i] = jnp.full((1, cols), M_INIT, F32)
        acc_ref[i] = jnp.zeros((PAIR + ONES_ROWS, cols), F32)
        if moba:
            qt = jnp.concatenate(qts, axis=1)
            gate = (jnp.dot(km_hi, qt, preferred_element_type=F32)
                    + jnp.dot(km_lo, qt, preferred_element_type=F32))
            cnt = jnp.zeros((16, cols), F32)
            for b in range(i):
                gb = gate[b:b + 1, :]
                beats = (gb > gate) | ((gb == gate) & (blk > b))
                cnt = cnt + jnp.where(beats, 1.0, 0.0)
            picked = (blk < i) & (cnt < float(MOBA_TOPK))
            sel_ref[i] = jnp.where(picked, 1.0, 0.0)

    key = lax.broadcasted_iota(jnp.int32, (tb, cols), 0)
    qry = lax.broadcasted_iota(jnp.int32, (tb, cols), 1) % tb
    causal = key <= qry

    def scores(i, j):
        halves = []
        for h in range(2):
            lanes = slice(0, PAIR) if moba else slice(h * HEAD_SLOT, (h + 1) * HEAD_SLOT)
            halves.append(jnp.dot(k_ref[pl.ds(j * tb, tb), lanes], qt_ref[i, h],
                                  preferred_element_type=F32))
        return jnp.concatenate(halves, axis=1)

    def item(i, j, s):
        m_prev = m_ref[i]
        if i == j:
            s = jnp.where(causal, s, NEG_INF)
            m_new = jnp.maximum(m_prev, jnp.max(s, axis=0, keepdims=True))
            m_sub = m_new
        else:
            m_new = jnp.maximum(m_prev, jnp.max(s, axis=0, keepdims=True))
            m_sub = m_new
            if moba:
                chosen = sel_ref[i, pl.ds(j, 1), :] > 0.5
                m_new = jnp.where(chosen, m_new, m_prev)
                m_sub = jnp.where(chosen, m_new, -M_INIT)
        alpha = jnp.exp2(m_prev - m_new)
        p = jnp.exp2(s - m_sub).astype(BF16)
        m_ref[i] = m_new
        acc_ref[i] = alpha * acc_ref[i] + jnp.dot(vt_ref[j], p, preferred_element_type=F32)

    def finish(i):
        acc = acc_ref[i]
        o = acc[0:PAIR, :] / acc[PAIR:PAIR + 1, :]
        vrow = lax.broadcasted_iota(jnp.int32, (PAIR, tb), 0)
        ot = jnp.where(vrow < PAIR // 2, o[:, 0:tb], o[:, tb:cols])
        o_ref[pl.ds(i * tb, tb), :] = ot.T.astype(o_ref.dtype)

    for i in range(nq):
        prep(i)
    order = [(r, t) if t <= r else (nq - 1 - r, t - (r + 1))
             for t in range(nq + 1) for r in range(ATT_CHAINS)]
    pending = [scores(*order[n]) for n in range(ATT_AHEAD)]
    for n, (i, j) in enumerate(order):
        if n + ATT_AHEAD < len(order):
            pending.append(scores(*order[n + ATT_AHEAD]))
        item(i, j, pending.pop(0))
        if i == j:
            finish(i)


def _attnT_call(moba, q, k, v, kmean, batch, seq, name):
    T, width = q.shape
    pairs = MLA_HEADS // 2
    qw = width // pairs
    nq = seq // ATT_BLOCK
    assert nq == 2 * ATT_CHAINS, "chain pairing (r, nq-1-r) needs nq == 2 * ATT_CHAINS"
    cols = 2 * ATT_BLOCK
    seq_block = lambda w: pl.BlockSpec((seq, w), lambda b, p: (b, p))
    in_specs = [seq_block(qw), seq_block(qw), seq_block(PAIR)]
    scratch = [pltpu.VMEM((nq, 2, PAIR, ATT_BLOCK), BF16),
               pltpu.VMEM((nq, PAIR + ONES_ROWS, ATT_BLOCK), BF16),
               pltpu.VMEM((nq, 1, cols), F32),
               pltpu.VMEM((nq, PAIR + ONES_ROWS, cols), F32)]
    args = [q, k, v]
    if moba:
        in_specs.append(pl.BlockSpec((nq, PAIR), lambda b, p: (b, p)))
        scratch.append(pltpu.VMEM((nq, 16, cols), F32))
        args.append(kmean)
    return pl.pallas_call(
        functools.partial(_attnT_body, moba),
        grid=(batch, pairs),
        in_specs=in_specs,
        out_specs=seq_block(PAIR),
        out_shape=jax.ShapeDtypeStruct((T, pairs * PAIR), BF16),
        scratch_shapes=scratch,
        compiler_params=_cparams(("arbitrary", "arbitrary")),
        name=name,
    )(*args)


def _merge_body(oa_ref, ob_ref, x_ref, gout_ref, wout_ref, gffn_ref, wr_ref, br_ref, tri_ref,
                x1_ref, row_ref, meta_ref, counts_ref):
    tm = x_ref.shape[0]

    @pl.when(pl.program_id(0) == 0)
    def _():
        counts_ref[...] = jnp.zeros_like(counts_ref)

    oa = oa_ref[...].astype(F32)
    ob = ob_ref[...].astype(F32)
    na = oa * _rms(oa, 512) * gout_ref[:, 0:512]
    nb = ob * _rms(ob, 512) * gout_ref[:, 512:1024]
    mixed = jnp.concatenate([na, nb], axis=1).astype(BF16)
    x1 = x_ref[...] + jnp.dot(mixed, wout_ref[...], preferred_element_type=F32)
    x1_ref[...] = x1
    h2 = x1 * _rms(x1, D_MODEL) * gffn_ref[...]
    half = D_MODEL // 2
    lo = pltpu.bitcast(h2[:, 0:half].astype(BF16).astype(F32), jnp.uint32)
    hi = pltpu.bitcast(h2[:, half:D_MODEL].astype(BF16).astype(F32), jnp.uint32)
    packed = (hi & jnp.uint32(0xFFFF0000)) | (lo >> 16)

    logits = jnp.dot(h2.astype(BF16), wr_ref[...], preferred_element_type=F32) + br_ref[...]
    lane = lax.broadcasted_iota(jnp.int32, (tm, LANES), 1)
    lane_f = lane.astype(F32)
    big = float(LANES)

    def first_max(vals):
        mx = jnp.max(vals, axis=1, keepdims=True)
        idx = jnp.min(jnp.where(vals == mx, lane_f, big), axis=1, keepdims=True)
        return mx, idx

    is_group = (lane >= N_EXPERTS) & (lane < N_EXPERTS + N_GROUPS)
    gl = jnp.where(is_group, logits, NEG_INF)
    gmax, gidx = first_max(gl)
    g_w = 1.0 / jnp.sum(jnp.exp(gl - gmax), axis=1, keepdims=True)
    g_sel = gidx - float(N_EXPERTS)

    in_group = (lane < N_EXPERTS) & ((lane // EPG).astype(F32) == g_sel)
    el = jnp.where(in_group, logits, NEG_INF)
    v1, i1 = first_max(el)
    el2 = jnp.where(lane_f == i1, NEG_INF, el)
    v2, i2 = first_max(el2)
    t = jnp.exp(v2 - v1)
    w1 = g_w * (1.0 / (1.0 + t))
    w2 = g_w * (t / (1.0 + t))

    swap = i2 < i1
    ea = jnp.where(swap, i2, i1)
    eb = jnp.where(swap, i1, i2)
    wa = jnp.where(swap, w2, w1)
    wb = jnp.where(swap, w1, w2)
    la = ea - EPG * g_sel
    lb = eb - EPG * g_sel
    cls = PAIRS_PER_GROUP * g_sel + la * (2 * EPG - 1 - la) * 0.5 + (lb - la - 1.0)

    onehot = lane_f == cls
    ones = jnp.where(onehot, 1.0, 0.0)
    before = jnp.dot(tri_ref[...], ones.astype(BF16), preferred_element_type=F32) + counts_ref[0:1, :]
    pos = jnp.sum(jnp.where(onehot, before, 0.0), axis=1, keepdims=True)
    counts_ref[...] = counts_ref[...] + jnp.sum(ones, axis=0, keepdims=True)

    meta_ref[...] = jnp.where(lane == 0, cls, jnp.where(lane == 1, pos, 0.0)).astype(jnp.int32)
    wbits = pltpu.bitcast(jnp.where(lane == 0, wa, jnp.where(lane == 1, wb, 0.0)), jnp.uint32)
    row_ref[...] = jnp.concatenate([packed, wbits], axis=1)


def _merge_call(oa, ob, x2, gout, wout, gffn, wr, br, tri):
    T = x2.shape[0]
    tm = tri.shape[0]
    const = lambda *shape: pl.BlockSpec(shape, lambda i: (0,) * len(shape))
    rows = lambda w: pl.BlockSpec((tm, w), lambda i: (i, 0))
    return pl.pallas_call(
        _merge_body,
        grid=(T // tm,),
        in_specs=[rows(512), rows(512), rows(D_MODEL), const(1, D_MODEL), const(D_MODEL, D_MODEL),
                  const(1, D_MODEL), const(D_MODEL, LANES), const(1, LANES), const(tm, tm)],
        out_specs=[rows(D_MODEL), rows(ROW_WORDS), rows(LANES), const(8, LANES)],
        out_shape=[
            jax.ShapeDtypeStruct((T, D_MODEL), F32),
            jax.ShapeDtypeStruct((T, ROW_WORDS), jnp.uint32),
            jax.ShapeDtypeStruct((T, LANES), jnp.int32),
            jax.ShapeDtypeStruct((8, LANES), F32),
        ],
        compiler_params=_cparams(("arbitrary",)),
        name="merge",
    )(oa, ob, x2, gout, wout, gffn, wr, br, tri)


def _dispatch_body(dest_ref, row_ref, xs_in_ref, xs_ref, sem):
    del xs_in_ref
    n_tok = row_ref.shape[0]

    def start(t, c):
        pltpu.make_async_copy(row_ref.at[pl.ds(t, 1)], xs_ref.at[pl.ds(dest_ref[t], 1)], sem).start()
        return c

    lax.fori_loop(0, n_tok, start, 0, unroll=16)
    pltpu.make_async_copy(row_ref, xs_ref.at[pl.ds(0, n_tok)], sem).wait()


def _dispatch_call(dest, h2, xs_zero, tok_per_step):
    T = h2.shape[0]
    return pl.pallas_call(
        _dispatch_body,
        grid=(T // tok_per_step,),
        in_specs=[
            pl.BlockSpec((tok_per_step,), lambda i: (i,), memory_space=pltpu.SMEM),
            pl.BlockSpec((tok_per_step, h2.shape[1]), lambda i: (i, 0)),
            pl.BlockSpec(memory_space=pl.ANY),
        ],
        out_specs=pl.BlockSpec(memory_space=pl.ANY),
        out_shape=jax.ShapeDtypeStruct(xs_zero.shape, xs_zero.dtype),
        scratch_shapes=[pltpu.SemaphoreType.DMA(())],
        input_output_aliases={2: 0},
        compiler_params=_cparams(("arbitrary",)),
        name="dispatch",
    )(dest, h2, xs_zero)


def _expert_body(bea_ref, beb_ref, nused_ref, xs_ref, wga_ref, wua_ref, wda_ref,
                 wgb_ref, wub_ref, wdb_ref, y_ref):
    del bea_ref, beb_ref
    used = pl.program_id(0) < nused_ref[0]
    half = D_MODEL // 2

    @pl.when(used)
    def _():
        row = xs_ref[...]
        w = row[:, 0:half]
        lo = pltpu.bitcast(w << 16, F32)
        hi = pltpu.bitcast(w & jnp.uint32(0xFFFF0000), F32)
        xb = jnp.concatenate([lo, hi], axis=1).astype(BF16)
        wts = pltpu.bitcast(row[:, half:ROW_WORDS], F32)

        def expert(wg_ref, wu_ref, wd_ref):
            a = jnp.dot(xb, wg_ref[...], preferred_element_type=F32)
            u = jnp.dot(xb, wu_ref[...], preferred_element_type=F32)
            act = (a * (1.0 / (1.0 + jnp.exp(-a))) * u).astype(BF16)
            return jnp.dot(act, wd_ref[...], preferred_element_type=F32)

        y_ref[...] = (wts[:, 0:1] * expert(wga_ref, wua_ref, wda_ref)
                      + wts[:, 1:2] * expert(wgb_ref, wub_ref, wdb_ref))

    @pl.when(jnp.logical_not(used))
    def _():
        y_ref[...] = jnp.zeros_like(y_ref)


def _expert_call(block_ea, block_eb, n_used, xs, wg, wu, wd):
    P = xs.shape[0]
    n_blocks = P // ROW_BLOCK

    def row_map(i, bea, beb, nu):
        return (jnp.minimum(i, nu[0] - 1), 0)

    def wa_map(i, bea, beb, nu):
        return (bea[jnp.minimum(i, nu[0] - 1)], 0, 0)

    def wb_map(i, bea, beb, nu):
        return (beb[jnp.minimum(i, nu[0] - 1)], 0, 0)

    up = (None, D_MODEL, EXPERT_HIDDEN)
    down = (None, EXPERT_HIDDEN, D_MODEL)
    grid_spec = pltpu.PrefetchScalarGridSpec(
        num_scalar_prefetch=3,
        grid=(n_blocks,),
        in_specs=[
            pl.BlockSpec((ROW_BLOCK, ROW_WORDS), row_map),
            pl.BlockSpec(up, wa_map), pl.BlockSpec(up, wa_map), pl.BlockSpec(down, wa_map),
            pl.BlockSpec(up, wb_map), pl.BlockSpec(up, wb_map), pl.BlockSpec(down, wb_map),
        ],
        out_specs=pl.BlockSpec((ROW_BLOCK, D_MODEL), lambda i, bea, beb, nu: (i, 0)),
    )
    return pl.pallas_call(
        _expert_body,
        grid_spec=grid_spec,
        out_shape=jax.ShapeDtypeStruct((P, D_MODEL), F32),
        compiler_params=_cparams(("arbitrary",)),
        name="experts",
    )(block_ea, block_eb, n_used, xs, wg, wu, wd, wg, wu, wd)


def _combine_body(dest_ref, dest_next_ref, x1_ref, ys_ref, o_ref, y_buf, sems):
    n_tok = x1_ref.shape[0]
    step = pl.program_id(0)
    slot = step % 2

    def gather(idx_ref, to_slot):
        def start(t, c):
            pltpu.make_async_copy(ys_ref.at[pl.ds(idx_ref[t], 1)],
                                  y_buf.at[to_slot, pl.ds(t, 1)], sems.at[to_slot]).start()
            return c
        lax.fori_loop(0, n_tok, start, 0, unroll=16)

    @pl.when(step == 0)
    def _():
        gather(dest_ref, 0)

    @pl.when(step + 1 < pl.num_programs(0))
    def _():
        gather(dest_next_ref, 1 - slot)

    pltpu.make_async_copy(ys_ref.at[pl.ds(0, n_tok)], y_buf.at[slot], sems.at[slot]).wait()
    o_ref[...] = x1_ref[...] + y_buf[slot]


def _combine_call(dest, x1, ys, tok_per_step):
    T = x1.shape[0]
    n_steps = T // tok_per_step
    return pl.pallas_call(
        _combine_body,
        grid=(n_steps,),
        in_specs=[
            pl.BlockSpec((tok_per_step,), lambda i: (i,), memory_space=pltpu.SMEM),
            pl.BlockSpec((tok_per_step,), lambda i: (jnp.minimum(i + 1, n_steps - 1),),
                         memory_space=pltpu.SMEM),
            pl.BlockSpec((tok_per_step, D_MODEL), lambda i: (i, 0)),
            pl.BlockSpec(memory_space=pl.ANY),
        ],
        out_specs=pl.BlockSpec((tok_per_step, D_MODEL), lambda i: (i, 0)),
        out_shape=jax.ShapeDtypeStruct((T, D_MODEL), F32),
        scratch_shapes=[pltpu.VMEM((2, tok_per_step, D_MODEL), F32),
                        pltpu.SemaphoreType.DMA((2,))],
        compiler_params=_cparams(("arbitrary",)),
        name="combine",
    )(dest, dest, x1, ys)


def _layer(x, attn_norm_g, w_in, mla_q_lat_g, w_q_up, mla_kv_lat_g, w_kv_up, mla_q_head_g,
           mla_k_head_g, moba_q_head_g, moba_k_head_g, out_norm_mla_g, out_norm_moba_g, w_out,
           ffn_norm_g, w_router_group, b_router_group, w_router_expert, b_router_expert,
           w_gate, w_up, w_down):
    B, S, D = x.shape
    T = B * S
    x2 = x.reshape(T, D)

    win, wq, wkv, gains = _prep_weights(w_in, w_q_up, w_kv_up, mla_q_head_g, mla_k_head_g,
                                        moba_q_head_g, moba_k_head_g)
    glat = jnp.concatenate([mla_q_lat_g, mla_kv_lat_g])[None, :]
    qa, ka, va, qb, kb, vb, kmean = _proj_call(
        x2, attn_norm_g[None, :], win, glat, wq, wkv, gains, _group_matrices(), _rope_tables(S), S)

    oa = _attnT_call(False, qa, ka, va, None, B, S, "mla")
    ob = _attnT_call(True, qb, kb, vb, kmean.reshape(T // MOBA_BLOCK, 512), B, S, "moba")

    tm = 512
    wr = jnp.zeros((D, LANES), F32).at[:, :N_EXPERTS].set(w_router_expert)
    wr = wr.at[:, N_EXPERTS:N_EXPERTS + N_GROUPS].set(w_router_group).astype(BF16)
    br = jnp.zeros((1, LANES), F32).at[0, :N_EXPERTS].set(b_router_expert)
    br = br.at[0, N_EXPERTS:N_EXPERTS + N_GROUPS].set(b_router_group)
    tri = jnp.asarray(np.tril(np.ones((tm, tm), np.float32), -1), BF16)
    gout = jnp.concatenate([out_norm_mla_g, out_norm_moba_g])[None, :]
    x1, rows, meta, counts = _merge_call(oa, ob, x2, gout, w_out.astype(BF16), ffn_norm_g[None, :],
                                         wr, br, tri)

    counts = counts[0, :N_CLASSES].astype(jnp.int32)
    padded = (counts + ROW_BLOCK - 1) // ROW_BLOCK * ROW_BLOCK
    pad_end = jnp.cumsum(padded)
    pad_start = pad_end - padded
    n_blocks = (T + ROW_BLOCK - 1) // ROW_BLOCK + N_CLASSES
    n_used = (pad_end[-1] // ROW_BLOCK).astype(jnp.int32)
    blk = jnp.minimum(jnp.arange(n_blocks, dtype=jnp.int32), n_used - 1) * ROW_BLOCK
    block_class = jnp.minimum(jnp.sum(pad_end[None, :] <= blk[:, None], axis=1), N_CLASSES - 1)
    pairs = [(a, b) for a in range(EPG) for b in range(a + 1, EPG)]
    class_ea = np.array([g * EPG + a for g in range(N_GROUPS) for a, _ in pairs], np.int32)
    class_eb = np.array([g * EPG + b for g in range(N_GROUPS) for _, b in pairs], np.int32)
    block_ea = jnp.take(jnp.asarray(class_ea), block_class)
    block_eb = jnp.take(jnp.asarray(class_eb), block_class)
    dest = (jnp.take(pad_start, meta[:, 0]) + meta[:, 1]).astype(jnp.int32)

    xs = _dispatch_call(dest, rows, jnp.zeros((n_blocks * ROW_BLOCK, ROW_WORDS), jnp.uint32), 1024)
    ys = _expert_call(block_ea, block_eb, n_used[None], xs,
                      w_gate.astype(BF16), w_up.astype(BF16), w_down.astype(BF16))
    out = _combine_call(dest, x1, ys, 256)
    return out.reshape(B, S, D)


def kernel(x, attn_norm_g, w_in, mla_q_lat_g, w_q_up, mla_kv_lat_g, w_kv_up, mla_q_head_g, mla_k_head_g, moba_q_head_g, moba_k_head_g, out_norm_mla_g, out_norm_moba_g, w_out, ffn_norm_g, w_router_group, b_router_group, w_router_expert, b_router_expert, w_gate, w_up, w_down):
    return _layer(x, attn_norm_g[0], w_in[0], mla_q_lat_g[0], w_q_up[0], mla_kv_lat_g[0], w_kv_up[0],
                  mla_q_head_g[0], mla_k_head_g[0], moba_q_head_g[0], moba_k_head_g[0],
                  out_norm_mla_g[0], out_norm_moba_g[0], w_out[0], ffn_norm_g[0],
                  w_router_group[0], b_router_group[0], w_router_expert[0], b_router_expert[0],
                  w_gate[0], w_up[0], w_down[0])
```

```python
import functools

import jax
import jax.numpy as jnp
import numpy as np
from jax import lax
from jax.experimental import pallas as pl
from jax.experimental.pallas import tpu as pltpu

F32 = jnp.float32
BF16 = jnp.bfloat16

D_MODEL = 1024
MLA_HEADS = 8
MLA_NOPE = 64
MLA_ROPE = 32
MLA_QK = MLA_NOPE + MLA_ROPE
MLA_V = 64
MLA_Q_LORA = 256
MLA_KV_LORA = 128
MOBA_HEADS = 8
MOBA_D = 64
MOBA_BLOCK = 256
MOBA_TOPK = 3
ROPE_THETA = 10000.0
EPS = 1e-6
N_GROUPS = 4
EPG = 8
N_EXPERTS = N_GROUPS * EPG
EXPERT_HIDDEN = 256
ROW_BLOCK = 256
PAIRS_PER_GROUP = EPG * (EPG - 1) // 2
N_CLASSES = N_GROUPS * PAIRS_PER_GROUP
ROW_WORDS = D_MODEL // 2 + 128

LANES = 128
HEAD_SLOT = LANES
PAIR = LANES
PROJ_WIDTH = 2048
MERGE_ROWS = 1024
MERGE_CHUNK = 256
PROJ_ROWS = 1024
NEG_INF = float("-inf")

VMEM_LIMIT = 48 * 1024 * 1024


def _cparams(sem):
    return pltpu.CompilerParams(dimension_semantics=sem, vmem_limit_bytes=VMEM_LIMIT)


def _mla_slot_index():
    idx = -np.ones((HEAD_SLOT,), np.int64)
    idx[0:32] = np.arange(0, 32)
    idx[32:48] = MLA_NOPE + np.arange(0, 16)
    idx[64:96] = np.arange(32, 64)
    idx[96:112] = MLA_NOPE + np.arange(16, 32)
    return idx


def _moba_pair_index():
    head = np.concatenate([np.zeros(32), np.ones(32), np.zeros(32), np.ones(32)]).astype(np.int64)
    feat = np.concatenate([np.arange(32), np.arange(32), 32 + np.arange(32), 32 + np.arange(32)])
    return head, feat


def _gather_cols(w, idx):
    safe = np.where(idx < 0, 0, idx)
    out = jnp.take(w, jnp.asarray(safe), axis=-1)
    return jnp.where(jnp.asarray(idx < 0), 0.0, out)


def _prep_weights(w_in, w_q_up, w_kv_up, q_head_g, k_head_g, mq_g, mk_g):
    slot = _mla_slot_index()
    head, feat = _moba_pair_index()

    kpe_idx = np.where(slot >= MLA_NOPE, slot - MLA_NOPE, -1)
    off_kpe = MLA_Q_LORA + MLA_KV_LORA
    off_mq = off_kpe + MLA_ROPE
    off_mk = off_mq + MOBA_HEADS * MOBA_D
    off_mv = off_mk + MOBA_HEADS * MOBA_D
    moba_idx = np.concatenate([(2 * p + head) * MOBA_D + feat for p in range(MOBA_HEADS // 2)])
    cols = np.concatenate([
        np.arange(0, off_kpe),
        np.where(kpe_idx < 0, -1, off_kpe + kpe_idx),
        off_mq + moba_idx,
        off_mk + moba_idx,
        off_mv + np.arange(MOBA_HEADS * MOBA_D),
    ])
    win = _gather_cols(w_in, cols).astype(BF16)

    q_idx = np.concatenate([np.where(slot < 0, -1, h * MLA_QK + slot) for h in range(MLA_HEADS)])
    wq = _gather_cols(w_q_up, q_idx).astype(BF16)

    nope_slot = np.where((slot >= 0) & (slot < MLA_NOPE), slot, -1)
    kn_idx = np.concatenate([np.where(nope_slot < 0, -1, h * (MLA_NOPE + MLA_V) + nope_slot)
                             for h in range(MLA_HEADS)])
    v_idx = np.concatenate([h * (MLA_NOPE + MLA_V) + MLA_NOPE + np.arange(MLA_V) for h in range(MLA_HEADS)])
    wkv = _gather_cols(w_kv_up, np.concatenate([kn_idx, v_idx])).astype(BF16)

    gq = _gather_cols(q_head_g, slot) * (MLA_QK ** -0.5 * LOG2E)
    gk = _gather_cols(k_head_g, slot)
    gmq = jnp.take(mq_g, jnp.asarray(feat)) * (MOBA_D ** -0.5 * LOG2E)
    gmk = jnp.take(mk_g, jnp.asarray(feat))
    gains = jnp.stack([gq, gk, gmq, gmk]).astype(F32)
    return win, wq, wkv, gains


def _rope_tables(seq):
    def tab(dim):
        inv = ROPE_THETA ** (-(jnp.arange(0, dim, 2, dtype=F32) / dim))
        ang = jnp.arange(seq, dtype=F32)[:, None] * inv[None, :]
        return jnp.cos(ang), jnp.sin(ang)

    cr, sr = tab(MLA_ROPE)
    cf, sf = tab(MOBA_D)
    one = jnp.ones((seq, 16), F32)
    zero = jnp.zeros((seq, 16), F32)
    cos_a = jnp.concatenate([one, one, cr, one, one, one, cr, one], axis=1)
    sin_a = jnp.concatenate([zero, zero, -sr, zero, zero, zero, sr, zero], axis=1)
    cos_b = jnp.concatenate([cf, cf, cf, cf], axis=1)
    sin_b = jnp.concatenate([-sf, -sf, sf, sf], axis=1)
    return jnp.stack([cos_a, sin_a, cos_b, sin_b])


def _group_matrices():
    lane = np.arange(256)
    g_mla = (lane[:, None] // HEAD_SLOT == lane[None, :] // HEAD_SLOT)
    head, _ = _moba_pair_index()
    hid = np.concatenate([head, 2 + head])
    g_moba = hid[:, None] == hid[None, :]
    return jnp.asarray(np.stack([g_mla, g_moba]), BF16)


def _rms(x, width):
    return lax.rsqrt(jnp.sum(x * x, axis=-1, keepdims=True) * (1.0 / width) + EPS)


def _rope(t, cos, sin):
    outs = []
    for c in range(0, t.shape[1], LANES):
        xc = t[:, c:c + LANES]
        outs.append(xc * cos + pltpu.roll(xc, 64, 1) * sin)
    return jnp.concatenate(outs, axis=1)


def _proj_body(x_ref, gx_ref, win_ref, glat_ref, wq_ref, wkv_ref, gains_ref, gmat_ref, rope_ref,
               qa_ref, ka_ref, va_ref, qb_ref, kb_ref, vb_ref, kmean_ref):
    n_chunks = x_ref.shape[0] // MOBA_BLOCK
    gains = gains_ref[...]
    g_mla = gmat_ref[0]
    g_moba = gmat_ref[1]

    def tile_gain(row, n):
        return jnp.concatenate([gains[row:row + 1, :]] * n, axis=1)

    def rows(c):
        return pl.ds(c * MOBA_BLOCK, MOBA_BLOCK)

    def group_sums(sq, gmat):
        return jnp.concatenate([jnp.dot(sq[:, c:c + 256], gmat, preferred_element_type=F32)
                                for c in range(0, sq.shape[1], 256)], axis=1)

    def finish(t, ssum, dim, gain, cos, sin):
        return _rope(t * lax.rsqrt(ssum * (1.0 / dim) + EPS) * gain, cos, sin)

    st = [dict() for _ in range(n_chunks)]

    def stage_a(c):
        x = x_ref[rows(c), :]
        st[c]["h"] = (x * _rms(x, D_MODEL) * gx_ref[...]).astype(BF16)

    def stage_b(c):
        st[c]["proj"] = jnp.dot(st[c].pop("h"), win_ref[...], preferred_element_type=F32)

    def stage_c(c):
        proj = st[c].pop("proj")
        q_lat = proj[:, 0:256]
        kv_lat = proj[:, 256:384]
        st[c]["ql"] = (q_lat * _rms(q_lat, MLA_Q_LORA) * glat_ref[:, 0:256]).astype(BF16)
        st[c]["kvl"] = (kv_lat * _rms(kv_lat, MLA_KV_LORA) * glat_ref[:, 256:384]).astype(BF16)
        st[c]["kpe"] = proj[:, 384:512]
        st[c]["mq"] = proj[:, 512:1024]
        st[c]["mk"] = proj[:, 1024:1536]
        st[c]["mq2"] = (st[c]["mq"] * st[c]["mq"]).astype(BF16)
        st[c]["mk2"] = (st[c]["mk"] * st[c]["mk"]).astype(BF16)
        vb_ref[rows(c), :] = proj[:, 1536:2048].astype(BF16)

    def stage_d(c):
        st[c]["qa"] = jnp.dot(st[c].pop("ql"), wq_ref[...], preferred_element_type=F32)
        st[c]["kv"] = jnp.dot(st[c].pop("kvl"), wkv_ref[...], preferred_element_type=F32)
        st[c]["mq_ss"] = group_sums(st[c].pop("mq2"), g_moba)
        st[c]["mk_ss"] = group_sums(st[c].pop("mk2"), g_moba)

    def stage_e(c):
        cos_b, sin_b = rope_ref[2, rows(c), :], rope_ref[3, rows(c), :]
        qb_ref[rows(c), :] = finish(st[c].pop("mq"), st[c].pop("mq_ss"), MOBA_D,
                                    tile_gain(2, 4), cos_b, sin_b).astype(BF16)
        kb = finish(st[c].pop("mk"), st[c].pop("mk_ss"), MOBA_D, tile_gain(3, 4), cos_b, sin_b)
        kb_ref[rows(c), :] = kb.astype(BF16)
        kmean_ref[c] = jnp.sum(kb, axis=0, keepdims=True) * (1.0 / MOBA_BLOCK)
        kv = st[c].pop("kv")
        va_ref[rows(c), :] = kv[:, 1024:1536].astype(BF16)
        st[c]["ka"] = kv[:, 0:1024] + jnp.concatenate([st[c].pop("kpe")] * MLA_HEADS, axis=1)
        st[c]["qa2"] = (st[c]["qa"] * st[c]["qa"]).astype(BF16)
        st[c]["ka2"] = (st[c]["ka"] * st[c]["ka"]).astype(BF16)

    def stage_f(c):
        st[c]["qa_ss"] = group_sums(st[c].pop("qa2"), g_mla)
        st[c]["ka_ss"] = group_sums(st[c].pop("ka2"), g_mla)

    def stage_g(c):
        cos_a, sin_a = rope_ref[0, rows(c), :], rope_ref[1, rows(c), :]
        qa_ref[rows(c), :] = finish(st[c].pop("qa"), st[c].pop("qa_ss"), MLA_QK,
                                    tile_gain(0, 8), cos_a, sin_a).astype(BF16)
        ka_ref[rows(c), :] = finish(st[c].pop("ka"), st[c].pop("ka_ss"), MLA_QK,
                                    tile_gain(1, 8), cos_a, sin_a).astype(BF16)

    def run(stage, c):
        if 0 <= c < n_chunks:
            stage(c)

    stage_a(0)
    for s in range(n_chunks + 2):
        run(stage_f, s - 2)
        run(stage_c, s - 1)
        run(stage_d, s - 1)
        run(stage_b, s)
        run(stage_e, s - 1)
        run(stage_g, s - 2)
        run(stage_a, s + 1)


def _proj_call(x2, gx, win, glat, wq, wkv, gains, gmat, rope, seq):
    T = x2.shape[0]
    tm = PROJ_ROWS
    nt = T // tm
    spb = seq // tm
    bpt = tm // MOBA_BLOCK
    const = lambda *shape: pl.BlockSpec(shape, lambda i: (0,) * len(shape))
    rows = lambda w: pl.BlockSpec((tm, w), lambda i: (i, 0))
    return pl.pallas_call(
        _proj_body,
        grid=(nt,),
        in_specs=[
            rows(D_MODEL),
            const(1, D_MODEL),
            const(D_MODEL, PROJ_WIDTH),
            const(1, 384),
            const(MLA_Q_LORA, 1024),
            const(MLA_KV_LORA, 1536),
            const(4, LANES),
            const(2, 256, 256),
            pl.BlockSpec((4, tm, LANES), lambda i: (0, i % spb, 0)),
        ],
        out_specs=[rows(1024), rows(1024), rows(512), rows(512), rows(512), rows(512),
                   pl.BlockSpec((bpt, 1, 512), lambda i: (i, 0, 0))],
        out_shape=[
            jax.ShapeDtypeStruct((T, 1024), BF16),
            jax.ShapeDtypeStruct((T, 1024), BF16),
            jax.ShapeDtypeStruct((T, 512), BF16),
            jax.ShapeDtypeStruct((T, 512), BF16),
            jax.ShapeDtypeStruct((T, 512), BF16),
            jax.ShapeDtypeStruct((T, 512), BF16),
            jax.ShapeDtypeStruct((nt * bpt, 1, 512), F32),
        ],
        compiler_params=_cparams(("arbitrary",)),
        name="proj",
    )(x2, gx, win, glat, wq, wkv, gains, gmat, rope)


def _nt_dot(a, b):
    return lax.dot_general(a, b, (((1,), (1,)), ((), ())), preferred_element_type=F32)


ATT_BLOCK = MOBA_BLOCK
ATT_CHAINS = 4
ATT_AHEAD = 2
M_INIT = -1e30
ONES_ROWS = 16
LOG2E = 1.4426950408889634


def _attnT_body(moba, q_ref, k_ref, v_ref, *rest):
    if moba:
        km_ref, o_ref, qt_ref, vt_ref, m_ref, acc_ref, sel_ref = rest
    else:
        o_ref, qt_ref, vt_ref, m_ref, acc_ref = rest
    tb = ATT_BLOCK
    cols = 2 * tb
    nq = q_ref.shape[0] // tb
    qw = q_ref.shape[1]
    lane_q = lax.broadcasted_iota(jnp.int32, (tb, qw), 1)

    if moba:
        km = km_ref[...]
        km_hi = km.astype(BF16)
        km_lo = (km - km_hi.astype(F32)).astype(BF16)
        zpad = jnp.zeros((16 - nq, PAIR), BF16)
        km_hi = jnp.concatenate([km_hi, zpad], axis=0)
        km_lo = jnp.concatenate([km_lo, zpad], axis=0)
        blk = lax.broadcasted_iota(jnp.int32, (16, cols), 0)

    def prep(i):
        start = i * tb
        q = q_ref[pl.ds(start, tb), :]
        if moba:
            head_of_lane = (lane_q // 32) % 2
            zero = jnp.zeros_like(q)
            per_head = [jnp.where(head_of_lane == h, q, zero) for h in range(2)]
        else:
            per_head = [q[:, h * HEAD_SLOT:(h + 1) * HEAD_SLOT] for h in range(2)]
        qts = [qh.astype(F32).T.astype(BF16) for qh in per_head]
        for h in range(2):
            qt_ref[i, h] = qts[h]
        vt = v_ref[pl.ds(start, tb), :].astype(F32).T.astype(BF16)
        vt_ref[i] = jnp.concatenate([vt, jnp.ones((ONES_ROWS, tb), BF16)], axis=0)
        m_ref[i] = jnp.full((1, cols), M_INIT, F32)
        acc_ref[i] = jnp.zeros((PAIR + ONES_ROWS, cols), F32)
        if moba:
            qt = jnp.concatenate(qts, axis=1)
            gate = (jnp.dot(km_hi, qt, preferred_element_type=F32)
                    + jnp.dot(km_lo, qt, preferred_element_type=F32))
            cnt = jnp.zeros((16, cols), F32)
            for b in range(i):
                gb = gate[b:b + 1, :]
                beats = (gb > gate) | ((gb == gate) & (blk > b))
                cnt = cnt + jnp.where(beats, 1.0, 0.0)
            picked = (blk < i) & (cnt < float(MOBA_TOPK))
            sel_ref[i] = jnp.where(picked, 1.0, 0.0)

    key = lax.broadcasted_iota(jnp.int32, (tb, cols), 0)
    qry = lax.broadcasted_iota(jnp.int32, (tb, cols), 1) % tb
    causal = key <= qry

    def scores(i, j):
        halves = []
        for h in range(2):
            lanes = slice(0, PAIR) if moba else slice(h * HEAD_SLOT, (h + 1) * HEAD_SLOT)
            halves.append(jnp.dot(k_ref[pl.ds(j * tb, tb), lanes], qt_ref[i, h],
                                  preferred_element_type=F32))
        return jnp.concatenate(halves, axis=1)

    def item(i, j, s):
        m_prev = m_ref[i]
        if i == j:
            s = jnp.where(causal, s, NEG_INF)
            m_new = jnp.maximum(m_prev, jnp.max(s, axis=0, keepdims=True))
            m_sub = m_new
        else:
            m_new = jnp.maximum(m_prev, jnp.max(s, axis=0, keepdims=True))
            m_sub = m_new
            if moba:
                chosen = sel_ref[i, pl.ds(j, 1), :] > 0.5
                m_new = jnp.where(chosen, m_new, m_prev)
                m_sub = jnp.where(chosen, m_new, -M_INIT)
        alpha = jnp.exp2(m_prev - m_new)
        p = jnp.exp2(s - m_sub).astype(BF16)
        m_ref[i] = m_new
        acc_ref[i] = alpha * acc_ref[i] + jnp.dot(vt_ref[j], p, preferred_element_type=F32)

    def finish(i):
        acc = acc_ref[i]
        o = acc[0:PAIR, :] / acc[PAIR:PAIR + 1, :]
        vrow = lax.broadcasted_iota(jnp.int32, (PAIR, tb), 0)
        ot = jnp.where(vrow < PAIR // 2, o[:, 0:tb], o[:, tb:cols])
        o_ref[pl.ds(i * tb, tb), :] = ot.T.astype(o_ref.dtype)

    for i in range(nq):
        prep(i)
    order = [(r, t) if t <= r else (nq - 1 - r, t - (r + 1))
             for t in range(nq + 1) for r in range(ATT_CHAINS)]
    pending = [scores(*order[n]) for n in range(ATT_AHEAD)]
    for n, (i, j) in enumerate(order):
        if n + ATT_AHEAD < len(order):
            pending.append(scores(*order[n + ATT_AHEAD]))
        item(i, j, pending.pop(0))
        if i == j:
            finish(i)


def _attnT_call(moba, q, k, v, kmean, batch, seq, name):
    T, width = q.shape
    pairs = MLA_HEADS // 2
    qw = width // pairs
    nq = seq // ATT_BLOCK
    assert nq == 2 * ATT_CHAINS, "chain pairing (r, nq-1-r) needs nq == 2 * ATT_CHAINS"
    cols = 2 * ATT_BLOCK
    seq_block = lambda w: pl.BlockSpec((seq, w), lambda b, p: (b, p))
    in_specs = [seq_block(qw), seq_block(qw), seq_block(PAIR)]
    scratch = [pltpu.VMEM((nq, 2, PAIR, ATT_BLOCK), BF16),
               pltpu.VMEM((nq, PAIR + ONES_ROWS, ATT_BLOCK), BF16),
               pltpu.VMEM((nq, 1, cols), F32),
               pltpu.VMEM((nq, PAIR + ONES_ROWS, cols), F32)]
    args = [q, k, v]
    if moba:
        in_specs.append(pl.BlockSpec((nq, PAIR), lambda b, p: (b, p)))
        scratch.append(pltpu.VMEM((nq, 16, cols), F32))
        args.append(kmean)
    return pl.pallas_call(
        functools.partial(_attnT_body, moba),
        grid=(batch, pairs),
        in_specs=in_specs,
        out_specs=seq_block(PAIR),
        out_shape=jax.ShapeDtypeStruct((T, pairs * PAIR), BF16),
        scratch_shapes=scratch,
        compiler_params=_cparams(("arbitrary", "arbitrary")),
        name=name,
    )(*args)


def _merge_body(oa_ref, ob_ref, x_ref, gout_ref, wout_ref, gffn_ref, wr_ref, br_ref, tri_ref,
                x1_ref, row_ref, meta_ref, counts_ref):
    ck = tri_ref.shape[0]
    n_chunks = x_ref.shape[0] // ck

    @pl.when(pl.program_id(0) == 0)
    def _():
        counts_ref[...] = jnp.zeros_like(counts_ref)

    ridx = lax.broadcasted_iota(jnp.int32, (LANES, ck), 0)
    ridx_f = ridx.astype(F32)
    st = [dict() for _ in range(n_chunks)]

    def rows(c):
        return pl.ds(c * ck, ck)

    def first_max(vals):
        mx = jnp.max(vals, axis=0, keepdims=True)
        idx = jnp.min(jnp.where(vals == mx, ridx_f, float(LANES)), axis=0, keepdims=True)
        return mx, idx

    def stage_a(c):
        oa = oa_ref[rows(c), :].astype(F32)
        ob = ob_ref[rows(c), :].astype(F32)
        na = oa * _rms(oa, 512) * gout_ref[:, 0:512]
        nb = ob * _rms(ob, 512) * gout_ref[:, 512:1024]
        st[c]["mixed"] = jnp.concatenate([na, nb], axis=1).astype(BF16)

    def stage_b(c):
        st[c]["proj"] = jnp.dot(st[c].pop("mixed"), wout_ref[...], preferred_element_type=F32)

    def stage_c(c):
        x1 = x_ref[rows(c), :] + st[c].pop("proj")
        x1_ref[rows(c), :] = x1
        h2 = x1 * _rms(x1, D_MODEL) * gffn_ref[...]
        st[c]["h2b"] = h2.astype(BF16)
        half = D_MODEL // 2
        lo = pltpu.bitcast(h2[:, 0:half].astype(BF16).astype(F32), jnp.uint32)
        hi = pltpu.bitcast(h2[:, half:D_MODEL].astype(BF16).astype(F32), jnp.uint32)
        row_ref[rows(c), 0:half] = (hi & jnp.uint32(0xFFFF0000)) | (lo >> 16)

    def stage_d(c):
        st[c]["logits"] = _nt_dot(wr_ref[...], st[c].pop("h2b")) + br_ref[...]

    def stage_e(c):
        logits = st[c].pop("logits")
        is_group = (ridx >= N_EXPERTS) & (ridx < N_EXPERTS + N_GROUPS)
        gl = jnp.where(is_group, logits, NEG_INF)
        gmax, gidx = first_max(gl)
        g_w = 1.0 / jnp.sum(jnp.exp(gl - gmax), axis=0, keepdims=True)
        g_sel = gidx - float(N_EXPERTS)
        in_group = (ridx < N_EXPERTS) & ((ridx // EPG).astype(F32) == g_sel)
        el = jnp.where(in_group, logits, NEG_INF)
        v1, i1 = first_max(el)
        v2, i2 = first_max(jnp.where(ridx_f == i1, NEG_INF, el))
        t = jnp.exp(v2 - v1)
        w1 = g_w * (1.0 / (1.0 + t))
        w2 = g_w * (t / (1.0 + t))
        swap = i2 < i1
        la = jnp.where(swap, i2, i1) - EPG * g_sel
        lb = jnp.where(swap, i1, i2) - EPG * g_sel
        cls = PAIRS_PER_GROUP * g_sel + la * (2 * EPG - 1 - la) * 0.5 + (lb - la - 1.0)
        st[c]["cls"] = cls
        st[c]["onehot"] = ridx_f == cls
        wa = jnp.where(swap, w2, w1)
        wb = jnp.where(swap, w1, w2)
        wt = jnp.where(ridx == 0, wa, jnp.where(ridx == 1, wb, 0.0))
        row_ref[rows(c), D_MODEL // 2:ROW_WORDS] = pltpu.bitcast(wt.T, jnp.uint32)

    def stage_f(c):
        ones = jnp.where(st[c]["onehot"], 1.0, 0.0).astype(BF16)
        st[c]["ones"] = ones
        st[c]["before"] = jnp.dot(ones, tri_ref[...], preferred_element_type=F32)

    def stage_g(c):
        before = st[c].pop("before") + counts_ref[...][:, 0:1]
        pos = jnp.sum(jnp.where(st[c].pop("onehot"), before, 0.0), axis=0, keepdims=True)
        total = jnp.sum(st[c].pop("ones").astype(F32), axis=1, keepdims=True)
        counts_ref[...] = counts_ref[...] + total
        srow = lax.broadcasted_iota(jnp.int32, (8, ck), 0)
        meta = jnp.where(srow == 0, st[c].pop("cls"), jnp.where(srow == 1, pos, 0.0))
        meta_ref[:, rows(c)] = meta.astype(jnp.int32)

    def run(stage, c):
        if 0 <= c < n_chunks:
            stage(c)

    stage_a(0)
    for s in range(n_chunks + 2):
        run(stage_f, s - 2)
        run(stage_c, s - 1)
        run(stage_d, s - 1)
        run(stage_b, s)
        run(stage_e, s - 1)
        run(stage_g, s - 2)
        run(stage_a, s + 1)


def _merge_call(oa, ob, x2, gout, wout, gffn, wr, br, tri):
    T = x2.shape[0]
    tm = MERGE_ROWS
    ck = tri.shape[0]
    const = lambda *shape: pl.BlockSpec(shape, lambda i: (0,) * len(shape))
    rows = lambda w: pl.BlockSpec((tm, w), lambda i: (i, 0))
    return pl.pallas_call(
        _merge_body,
        grid=(T // tm,),
        in_specs=[rows(512), rows(512), rows(D_MODEL), const(1, D_MODEL), const(D_MODEL, D_MODEL),
                  const(1, D_MODEL), const(LANES, D_MODEL), const(LANES, 1), const(ck, ck)],
        out_specs=[rows(D_MODEL), rows(ROW_WORDS), pl.BlockSpec((8, tm), lambda i: (0, i)),
                   const(LANES, LANES)],
        out_shape=[
            jax.ShapeDtypeStruct((T, D_MODEL), F32),
            jax.ShapeDtypeStruct((T, ROW_WORDS), jnp.uint32),
            jax.ShapeDtypeStruct((8, T), jnp.int32),
            jax.ShapeDtypeStruct((LANES, LANES), F32),
        ],
        compiler_params=_cparams(("arbitrary",)),
        name="merge",
    )(oa, ob, x2, gout, wout, gffn, wr, br, tri)


def _dispatch_body(dest_ref, row_ref, xs_in_ref, xs_ref, sem):
    del xs_in_ref
    n_tok = row_ref.shape[0]

    def start(t, c):
        pltpu.make_async_copy(row_ref.at[pl.ds(t, 1)], xs_ref.at[pl.ds(dest_ref[t], 1)], sem).start()
        return c

    lax.fori_loop(0, n_tok, start, 0, unroll=16)
    pltpu.make_async_copy(row_ref, xs_ref.at[pl.ds(0, n_tok)], sem).wait()


def _dispatch_call(dest, rows, xs_zero, tok_per_step):
    T, width = rows.shape
    return pl.pallas_call(
        _dispatch_body,
        grid=(T // tok_per_step,),
        in_specs=[
            pl.BlockSpec((tok_per_step,), lambda i: (i,), memory_space=pltpu.SMEM),
            pl.BlockSpec((tok_per_step, width), lambda i: (i, 0)),
            pl.BlockSpec(memory_space=pl.ANY),
        ],
        out_specs=pl.BlockSpec(memory_space=pl.ANY),
        out_shape=jax.ShapeDtypeStruct(xs_zero.shape, xs_zero.dtype),
        scratch_shapes=[pltpu.SemaphoreType.DMA(())],
        input_output_aliases={2: 0},
        compiler_params=_cparams(("arbitrary",)),
        name="dispatch",
    )(dest, rows, xs_zero)


def _expert_body(bea_ref, beb_ref, nused_ref, xs_ref, wga_ref, wua_ref, wda_ref,
                 wgb_ref, wub_ref, wdb_ref, y_ref):
    del bea_ref, beb_ref
    used = pl.program_id(0) < nused_ref[0]
    half = D_MODEL // 2

    @pl.when(used)
    def _():
        row = xs_ref[...]
        w = row[:, 0:half]
        lo = pltpu.bitcast(w << 16, F32)
        hi = pltpu.bitcast(w & jnp.uint32(0xFFFF0000), F32)
        xb = jnp.concatenate([lo, hi], axis=1).astype(BF16)
        wts = pltpu.bitcast(row[:, half:ROW_WORDS], F32)

        def hidden(wg_ref, wu_ref):
            a = jnp.dot(xb, wg_ref[...], preferred_element_type=F32)
            u = jnp.dot(xb, wu_ref[...], preferred_element_type=F32)
            return (a * (1.0 / (1.0 + jnp.exp(-a))) * u).astype(BF16)

        act_a = hidden(wga_ref, wua_ref)
        act_b = hidden(wgb_ref, wub_ref)
        y_ref[...] = (wts[:, 0:1] * jnp.dot(act_a, wda_ref[...], preferred_element_type=F32)
                      + wts[:, 1:2] * jnp.dot(act_b, wdb_ref[...], preferred_element_type=F32))

    @pl.when(jnp.logical_not(used))
    def _():
        y_ref[...] = jnp.zeros_like(y_ref)


def _expert_call(block_ea, block_eb, n_used, xs, wg, wu, wd):
    P = xs.shape[0]
    n_blocks = P // ROW_BLOCK

    def row_map(i, bea, beb, nu):
        return (jnp.minimum(i, nu[0] - 1), 0)

    def wa_map(i, bea, beb, nu):
        return (bea[jnp.minimum(i, nu[0] - 1)], 0, 0)

    def wb_map(i, bea, beb, nu):
        return (beb[jnp.minimum(i, nu[0] - 1)], 0, 0)

    up = (None, D_MODEL, EXPERT_HIDDEN)
    down = (None, EXPERT_HIDDEN, D_MODEL)
    grid_spec = pltpu.PrefetchScalarGridSpec(
        num_scalar_prefetch=3,
        grid=(n_blocks,),
        in_specs=[
            pl.BlockSpec((ROW_BLOCK, ROW_WORDS), row_map),
            pl.BlockSpec(up, wa_map), pl.BlockSpec(up, wa_map), pl.BlockSpec(down, wa_map),
            pl.BlockSpec(up, wb_map), pl.BlockSpec(up, wb_map), pl.BlockSpec(down, wb_map),
        ],
        out_specs=pl.BlockSpec((ROW_BLOCK, D_MODEL), lambda i, bea, beb, nu: (i, 0)),
    )
    return pl.pallas_call(
        _expert_body,
        grid_spec=grid_spec,
        out_shape=jax.ShapeDtypeStruct((P, D_MODEL), F32),
        compiler_params=_cparams(("arbitrary",)),
        name="experts",
    )(block_ea, block_eb, n_used, xs, wg, wu, wd, wg, wu, wd)


def _combine_body(dest_ref, dest_next_ref, x1_ref, ys_ref, o_ref, y_buf, sems):
    n_tok = x1_ref.shape[0]
    step = pl.program_id(0)
    slot = step % 2

    def gather(idx_ref, to_slot):
        def start(t, c):
            pltpu.make_async_copy(ys_ref.at[pl.ds(idx_ref[t], 1)],
                                  y_buf.at[to_slot, pl.ds(t, 1)], sems.at[to_slot]).start()
            return c
        lax.fori_loop(0, n_tok, start, 0, unroll=16)

    @pl.when(step == 0)
    def _():
        gather(dest_ref, 0)

    @pl.when(step + 1 < pl.num_programs(0))
    def _():
        gather(dest_next_ref, 1 - slot)

    pltpu.make_async_copy(ys_ref.at[pl.ds(0, n_tok)], y_buf.at[slot], sems.at[slot]).wait()
    o_ref[...] = x1_ref[...] + y_buf[slot]


def _combine_call(dest, x1, ys, tok_per_step):
    T = x1.shape[0]
    n_steps = T // tok_per_step
    return pl.pallas_call(
        _combine_body,
        grid=(n_steps,),
        in_specs=[
            pl.BlockSpec((tok_per_step,), lambda i: (i,), memory_space=pltpu.SMEM),
            pl.BlockSpec((tok_per_step,), lambda i: (jnp.minimum(i + 1, n_steps - 1),),
                         memory_space=pltpu.SMEM),
            pl.BlockSpec((tok_per_step, D_MODEL), lambda i: (i, 0)),
            pl.BlockSpec(memory_space=pl.ANY),
        ],
        out_specs=pl.BlockSpec((tok_per_step, D_MODEL), lambda i: (i, 0)),
        out_shape=jax.ShapeDtypeStruct((T, D_MODEL), F32),
        scratch_shapes=[pltpu.VMEM((2, tok_per_step, D_MODEL), F32),
                        pltpu.SemaphoreType.DMA((2,))],
        compiler_params=_cparams(("arbitrary",)),
        name="combine",
    )(dest, dest, x1, ys)


def _layer(x, attn_norm_g, w_in, mla_q_lat_g, w_q_up, mla_kv_lat_g, w_kv_up, mla_q_head_g,
           mla_k_head_g, moba_q_head_g, moba_k_head_g, out_norm_mla_g, out_norm_moba_g, w_out,
           ffn_norm_g, w_router_group, b_router_group, w_router_expert, b_router_expert,
           w_gate, w_up, w_down):
    B, S, D = x.shape
    T = B * S
    x2 = x.reshape(T, D)

    win, wq, wkv, gains = _prep_weights(w_in, w_q_up, w_kv_up, mla_q_head_g, mla_k_head_g,
                                        moba_q_head_g, moba_k_head_g)
    glat = jnp.concatenate([mla_q_lat_g, mla_kv_lat_g])[None, :]
    qa, ka, va, qb, kb, vb, kmean = _proj_call(
        x2, attn_norm_g[None, :], win, glat, wq, wkv, gains, _group_matrices(), _rope_tables(S), S)

    oa = _attnT_call(False, qa, ka, va, None, B, S, "mla")
    ob = _attnT_call(True, qb, kb, vb, kmean.reshape(T // MOBA_BLOCK, 512), B, S, "moba")

    wr = jnp.zeros((LANES, D), F32).at[:N_EXPERTS].set(w_router_expert.T)
    wr = wr.at[N_EXPERTS:N_EXPERTS + N_GROUPS].set(w_router_group.T).astype(BF16)
    br = jnp.zeros((LANES, 1), F32).at[:N_EXPERTS, 0].set(b_router_expert)
    br = br.at[N_EXPERTS:N_EXPERTS + N_GROUPS, 0].set(b_router_group)
    tri = jnp.asarray(np.triu(np.ones((MERGE_CHUNK, MERGE_CHUNK), np.float32), 1), BF16)
    gout = jnp.concatenate([out_norm_mla_g, out_norm_moba_g])[None, :]
    x1, rows, meta, counts = _merge_call(oa, ob, x2, gout, w_out.astype(BF16), ffn_norm_g[None, :],
                                         wr, br, tri)

    counts = counts[:N_CLASSES, 0].astype(jnp.int32)
    padded = (counts + ROW_BLOCK - 1) // ROW_BLOCK * ROW_BLOCK
    pad_end = jnp.cumsum(padded)
    pad_start = pad_end - padded
    n_blocks = (T + ROW_BLOCK - 1) // ROW_BLOCK + N_CLASSES
    n_used = (pad_end[-1] // ROW_BLOCK).astype(jnp.int32)
    blk = jnp.minimum(jnp.arange(n_blocks, dtype=jnp.int32), n_used - 1) * ROW_BLOCK
    block_class = jnp.minimum(jnp.sum(pad_end[None, :] <= blk[:, None], axis=1), N_CLASSES - 1)
    pairs = [(a, b) for a in range(EPG) for b in range(a + 1, EPG)]
    class_ea = np.array([g * EPG + a for g in range(N_GROUPS) for a, _ in pairs], np.int32)
    class_eb = np.array([g * EPG + b for g in range(N_GROUPS) for _, b in pairs], np.int32)
    block_ea = jnp.take(jnp.asarray(class_ea), block_class)
    block_eb = jnp.take(jnp.asarray(class_eb), block_class)
    is_class = meta[0][:, None] == jnp.arange(N_CLASSES, dtype=jnp.int32)[None, :]
    dest = (jnp.sum(jnp.where(is_class, pad_start[None, :], 0), axis=1) + meta[1]).astype(jnp.int32)

    xs = _dispatch_call(dest, rows, jnp.zeros((n_blocks * ROW_BLOCK, ROW_WORDS), jnp.uint32), 1024)
    ys = _expert_call(block_ea, block_eb, n_used[None], xs,
                      w_gate.astype(BF16), w_up.astype(BF16), w_down.astype(BF16))
    out = _combine_call(dest, x1, ys, 256)
    return out.reshape(B, S, D)


def kernel(x, attn_norm_g, w_in, mla_q_lat_g, w_q_up, mla_kv_lat_g, w_kv_up, mla_q_head_g, mla_k_head_g, moba_q_head_g, moba_k_head_g, out_norm_mla_g, out_norm_moba_g, w_out, ffn_norm_g, w_router_group, b_router_group, w_router_expert, b_router_expert, w_gate, w_up, w_down):
    return _layer(x, attn_norm_g[0], w_in[0], mla_q_lat_g[0], w_q_up[0], mla_kv_lat_g[0], w_kv_up[0],
                  mla_q_head_g[0], mla_k_head_g[0], moba_q_head_g[0], moba_k_head_g[0],
                  out_norm_mla_g[0], out_norm_moba_g[0], w_out[0], ffn_norm_g[0],
                  w_router_group[0], b_router_group[0], w_router_expert[0], b_router_expert[0],
                  w_gate[0], w_up[0], w_down[0])
```

```python
import functools

import jax
import jax.numpy as jnp
import numpy as np
from jax import lax
from jax.experimental import pallas as pl
from jax.experimental.pallas import tpu as pltpu

F32 = jnp.float32
BF16 = jnp.bfloat16

D_MODEL = 1024
MLA_HEADS = 8
MLA_NOPE = 64
MLA_ROPE = 32
MLA_QK = MLA_NOPE + MLA_ROPE
MLA_V = 64
MLA_Q_LORA = 256
MLA_KV_LORA = 128
MOBA_HEADS = 8
MOBA_D = 64
MOBA_BLOCK = 256
MOBA_TOPK = 3
ROPE_THETA = 10000.0
EPS = 1e-6
N_GROUPS = 4
EPG = 8
N_EXPERTS = N_GROUPS * EPG
EXPERT_HIDDEN = 256
ROW_BLOCK = 256
PAIRS_PER_GROUP = EPG * (EPG - 1) // 2
N_CLASSES = N_GROUPS * PAIRS_PER_GROUP
ROW_WORDS = D_MODEL // 2 + 128

LANES = 128
HEAD_SLOT = LANES
PAIR = LANES
PROJ_WIDTH = 2048
MERGE_ROWS = 1024
MERGE_CHUNK = 256
PROJ_CHUNK = 256
PROJ_ROWS = 1024
NEG_INF = float("-inf")

VMEM_LIMIT = 48 * 1024 * 1024


def _cparams(sem):
    return pltpu.CompilerParams(dimension_semantics=sem, vmem_limit_bytes=VMEM_LIMIT)


def _mla_slot_index():
    idx = -np.ones((HEAD_SLOT,), np.int64)
    idx[0:32] = np.arange(0, 32)
    idx[32:48] = MLA_NOPE + np.arange(0, 16)
    idx[64:96] = np.arange(32, 64)
    idx[96:112] = MLA_NOPE + np.arange(16, 32)
    return idx


def _moba_pair_index():
    head = np.concatenate([np.zeros(32), np.ones(32), np.zeros(32), np.ones(32)]).astype(np.int64)
    feat = np.concatenate([np.arange(32), np.arange(32), 32 + np.arange(32), 32 + np.arange(32)])
    return head, feat


def _gather_cols(w, idx):
    safe = np.where(idx < 0, 0, idx)
    out = jnp.take(w, jnp.asarray(safe), axis=-1)
    return jnp.where(jnp.asarray(idx < 0), 0.0, out)


def _prep_weights(w_in, w_q_up, w_kv_up, q_head_g, k_head_g, mq_g, mk_g):
    slot = _mla_slot_index()
    head, feat = _moba_pair_index()

    kpe_idx = np.where(slot >= MLA_NOPE, slot - MLA_NOPE, -1)
    off_kpe = MLA_Q_LORA + MLA_KV_LORA
    off_mq = off_kpe + MLA_ROPE
    off_mk = off_mq + MOBA_HEADS * MOBA_D
    off_mv = off_mk + MOBA_HEADS * MOBA_D
    moba_idx = np.concatenate([(2 * p + head) * MOBA_D + feat for p in range(MOBA_HEADS // 2)])
    cols = np.concatenate([
        np.arange(0, off_kpe),
        np.where(kpe_idx < 0, -1, off_kpe + kpe_idx),
        off_mq + moba_idx,
        off_mk + moba_idx,
        off_mv + np.arange(MOBA_HEADS * MOBA_D),
    ])
    win = _gather_cols(w_in, cols).astype(BF16)

    q_idx = np.concatenate([np.where(slot < 0, -1, h * MLA_QK + slot) for h in range(MLA_HEADS)])
    wq = _gather_cols(w_q_up, q_idx).astype(BF16)

    nope_slot = np.where((slot >= 0) & (slot < MLA_NOPE), slot, -1)
    kn_idx = np.concatenate([np.where(nope_slot < 0, -1, h * (MLA_NOPE + MLA_V) + nope_slot)
                             for h in range(MLA_HEADS)])
    v_idx = np.concatenate([h * (MLA_NOPE + MLA_V) + MLA_NOPE + np.arange(MLA_V) for h in range(MLA_HEADS)])
    wkv = _gather_cols(w_kv_up, np.concatenate([kn_idx, v_idx])).astype(BF16)

    gq = _gather_cols(q_head_g, slot) * (MLA_QK ** -0.5 * LOG2E)
    gk = _gather_cols(k_head_g, slot)
    gmq = jnp.take(mq_g, jnp.asarray(feat)) * (MOBA_D ** -0.5 * LOG2E)
    gmk = jnp.take(mk_g, jnp.asarray(feat))
    gains = jnp.stack([gq, gk, gmq, gmk]).astype(F32)
    return win, wq, wkv, gains


def _rope_tables(seq):
    def tab(dim):
        inv = ROPE_THETA ** (-(jnp.arange(0, dim, 2, dtype=F32) / dim))
        ang = jnp.arange(seq, dtype=F32)[:, None] * inv[None, :]
        return jnp.cos(ang), jnp.sin(ang)

    cr, sr = tab(MLA_ROPE)
    cf, sf = tab(MOBA_D)
    one = jnp.ones((seq, 16), F32)
    zero = jnp.zeros((seq, 16), F32)
    cos_a = jnp.concatenate([one, one, cr, one, one, one, cr, one], axis=1)
    sin_a = jnp.concatenate([zero, zero, -sr, zero, zero, zero, sr, zero], axis=1)
    cos_b = jnp.concatenate([cf, cf, cf, cf], axis=1)
    sin_b = jnp.concatenate([-sf, -sf, sf, sf], axis=1)
    return jnp.stack([cos_a, sin_a, cos_b, sin_b])


def _group_matrices():
    lane = np.arange(256)
    g_mla = (lane[:, None] // HEAD_SLOT == lane[None, :] // HEAD_SLOT)
    head, _ = _moba_pair_index()
    hid = np.concatenate([head, 2 + head])
    g_moba = hid[:, None] == hid[None, :]
    return jnp.asarray(np.stack([g_mla, g_moba]), BF16)


def _rms(x, width):
    return lax.rsqrt(jnp.sum(x * x, axis=-1, keepdims=True) * (1.0 / width) + EPS)


def _rope(t, cos, sin):
    outs = []
    for c in range(0, t.shape[1], LANES):
        xc = t[:, c:c + LANES]
        outs.append(xc * cos + pltpu.roll(xc, 64, 1) * sin)
    return jnp.concatenate(outs, axis=1)


def _proj_body(x_ref, gx_ref, win_ref, glat_ref, wq_ref, wkv_ref, gains_ref, gmat_ref, rope_ref,
               qa_ref, ka_ref, va_ref, qb_ref, kb_ref, vb_ref, kmean_ref):
    n_chunks = x_ref.shape[0] // PROJ_CHUNK
    gains = gains_ref[...]
    g_mla = gmat_ref[0]
    g_moba = gmat_ref[1]

    def tile_gain(row, n):
        return jnp.concatenate([gains[row:row + 1, :]] * n, axis=1)

    def rows(c):
        return pl.ds(c * PROJ_CHUNK, PROJ_CHUNK)

    def group_sums(sq, gmat):
        return jnp.concatenate([jnp.dot(sq[:, c:c + 256], gmat, preferred_element_type=F32)
                                for c in range(0, sq.shape[1], 256)], axis=1)

    def finish(t, ssum, dim, gain, cos, sin):
        return _rope(t * lax.rsqrt(ssum * (1.0 / dim) + EPS) * gain, cos, sin)

    st = [dict() for _ in range(n_chunks)]

    def stage_a(c):
        x = x_ref[rows(c), :]
        st[c]["h"] = (x * _rms(x, D_MODEL) * gx_ref[...]).astype(BF16)

    def stage_b(c):
        st[c]["proj"] = jnp.dot(st[c].pop("h"), win_ref[...], preferred_element_type=F32)

    def stage_c(c):
        proj = st[c].pop("proj")
        q_lat = proj[:, 0:256]
        kv_lat = proj[:, 256:384]
        st[c]["ql"] = (q_lat * _rms(q_lat, MLA_Q_LORA) * glat_ref[:, 0:256]).astype(BF16)
        st[c]["kvl"] = (kv_lat * _rms(kv_lat, MLA_KV_LORA) * glat_ref[:, 256:384]).astype(BF16)
        st[c]["kpe"] = proj[:, 384:512]
        st[c]["mq"] = proj[:, 512:1024]
        st[c]["mk"] = proj[:, 1024:1536]
        st[c]["mq2"] = (st[c]["mq"] * st[c]["mq"]).astype(BF16)
        st[c]["mk2"] = (st[c]["mk"] * st[c]["mk"]).astype(BF16)
        vb_ref[rows(c), :] = proj[:, 1536:2048].astype(BF16)

    def stage_d(c):
        st[c]["qa"] = jnp.dot(st[c].pop("ql"), wq_ref[...], preferred_element_type=F32)
        st[c]["kv"] = jnp.dot(st[c].pop("kvl"), wkv_ref[...], preferred_element_type=F32)
        st[c]["mq_ss"] = group_sums(st[c].pop("mq2"), g_moba)
        st[c]["mk_ss"] = group_sums(st[c].pop("mk2"), g_moba)

    def stage_e(c):
        cos_b, sin_b = rope_ref[2, rows(c), :], rope_ref[3, rows(c), :]
        qb_ref[rows(c), :] = finish(st[c].pop("mq"), st[c].pop("mq_ss"), MOBA_D,
                                    tile_gain(2, 4), cos_b, sin_b).astype(BF16)
        kb = finish(st[c].pop("mk"), st[c].pop("mk_ss"), MOBA_D, tile_gain(3, 4), cos_b, sin_b)
        kb_ref[rows(c), :] = kb.astype(BF16)
        part = jnp.sum(kb, axis=0, keepdims=True) * (1.0 / MOBA_BLOCK)
        per_block = MOBA_BLOCK // PROJ_CHUNK
        if c % per_block == 0:
            kmean_ref[c // per_block] = part
        else:
            kmean_ref[c // per_block] = kmean_ref[c // per_block] + part
        kv = st[c].pop("kv")
        va_ref[rows(c), :] = kv[:, 1024:1536].astype(BF16)
        st[c]["ka"] = kv[:, 0:1024] + jnp.concatenate([st[c].pop("kpe")] * MLA_HEADS, axis=1)
        st[c]["qa2"] = (st[c]["qa"] * st[c]["qa"]).astype(BF16)
        st[c]["ka2"] = (st[c]["ka"] * st[c]["ka"]).astype(BF16)

    def stage_f(c):
        st[c]["qa_ss"] = group_sums(st[c].pop("qa2"), g_mla)
        st[c]["ka_ss"] = group_sums(st[c].pop("ka2"), g_mla)

    def stage_g(c):
        cos_a, sin_a = rope_ref[0, rows(c), :], rope_ref[1, rows(c), :]
        qa_ref[rows(c), :] = finish(st[c].pop("qa"), st[c].pop("qa_ss"), MLA_QK,
                                    tile_gain(0, 8), cos_a, sin_a).astype(BF16)
        ka_ref[rows(c), :] = finish(st[c].pop("ka"), st[c].pop("ka_ss"), MLA_QK,
                                    tile_gain(1, 8), cos_a, sin_a).astype(BF16)

    def run(stage, c):
        if 0 <= c < n_chunks:
            stage(c)

    stage_a(0)
    for s in range(n_chunks + 2):
        run(stage_f, s - 2)
        run(stage_c, s - 1)
        run(stage_d, s - 1)
        run(stage_b, s)
        run(stage_e, s - 1)
        run(stage_g, s - 2)
        run(stage_a, s + 1)


def _proj_call(x2, gx, win, glat, wq, wkv, gains, gmat, rope, seq):
    T = x2.shape[0]
    tm = PROJ_ROWS
    nt = T // tm
    spb = seq // tm
    bpt = tm // MOBA_BLOCK
    const = lambda *shape: pl.BlockSpec(shape, lambda i: (0,) * len(shape))
    rows = lambda w: pl.BlockSpec((tm, w), lambda i: (i, 0))
    return pl.pallas_call(
        _proj_body,
        grid=(nt,),
        in_specs=[
            rows(D_MODEL),
            const(1, D_MODEL),
            const(D_MODEL, PROJ_WIDTH),
            const(1, 384),
            const(MLA_Q_LORA, 1024),
            const(MLA_KV_LORA, 1536),
            const(4, LANES),
            const(2, 256, 256),
            pl.BlockSpec((4, tm, LANES), lambda i: (0, i % spb, 0)),
        ],
        out_specs=[rows(1024), rows(1024), rows(512), rows(512), rows(512), rows(512),
                   pl.BlockSpec((bpt, 1, 512), lambda i: (i, 0, 0))],
        out_shape=[
            jax.ShapeDtypeStruct((T, 1024), BF16),
            jax.ShapeDtypeStruct((T, 1024), BF16),
            jax.ShapeDtypeStruct((T, 512), BF16),
            jax.ShapeDtypeStruct((T, 512), BF16),
            jax.ShapeDtypeStruct((T, 512), BF16),
            jax.ShapeDtypeStruct((T, 512), BF16),
            jax.ShapeDtypeStruct((nt * bpt, 1, 512), F32),
        ],
        compiler_params=_cparams(("arbitrary",)),
        name="proj",
    )(x2, gx, win, glat, wq, wkv, gains, gmat, rope)


def _nt_dot(a, b):
    return lax.dot_general(a, b, (((1,), (1,)), ((), ())), preferred_element_type=F32)


ATT_BLOCK = MOBA_BLOCK
ATT_CHAINS = 4
ATT_AHEAD = 2
M_INIT = -1e30
ONES_ROWS = 16
LOG2E = 1.4426950408889634


def _attnT_body(moba, q_ref, k_ref, v_ref, *rest):
    if moba:
        km_ref, o_ref, qt_ref, vt_ref, m_ref, acc_ref, sel_ref = rest
    else:
        o_ref, qt_ref, vt_ref, m_ref, acc_ref = rest
    tb = ATT_BLOCK
    cols = 2 * tb
    nq = q_ref.shape[0] // tb
    qw = q_ref.shape[1]
    lane_q = lax.broadcasted_iota(jnp.int32, (tb, qw), 1)

    if moba:
        km = km_ref[...]
        km_hi = km.astype(BF16)
        km_lo = (km - km_hi.astype(F32)).astype(BF16)
        zpad = jnp.zeros((16 - nq, PAIR), BF16)
        km_hi = jnp.concatenate([km_hi, zpad], axis=0)
        km_lo = jnp.concatenate([km_lo, zpad], axis=0)
        blk = lax.broadcasted_iota(jnp.int32, (16, cols), 0)

    def prep(i):
        start = i * tb
        q = q_ref[pl.ds(start, tb), :]
        if moba:
            head_of_lane = (lane_q // 32) % 2
            zero = jnp.zeros_like(q)
            per_head = [jnp.where(head_of_lane == h, q, zero) for h in range(2)]
        else:
            per_head = [q[:, h * HEAD_SLOT:(h + 1) * HEAD_SLOT] for h in range(2)]
        qts = [qh.astype(F32).T.astype(BF16) for qh in per_head]
        for h in range(2):
            qt_ref[i, h] = qts[h]
        vt = v_ref[pl.ds(start, tb), :].astype(F32).T.astype(BF16)
        vt_ref[i] = jnp.concatenate([vt, jnp.ones((ONES_ROWS, tb), BF16)], axis=0)
        m_ref[i] = jnp.full((1, cols), M_INIT, F32)
        acc_ref[i] = jnp.zeros((PAIR + ONES_ROWS, cols), F32)
        if moba:
            qt = jnp.concatenate(qts, axis=1)
            gate = (jnp.dot(km_hi, qt, preferred_element_type=F32)
                    + jnp.dot(km_lo, qt, preferred_element_type=F32))
            cnt = jnp.zeros((16, cols), F32)
            for b in range(i):
                gb = gate[b:b + 1, :]
                beats = (gb > gate) | ((gb == gate) & (blk > b))
                cnt = cnt + jnp.where(beats, 1.0, 0.0)
            picked = (blk < i) & (cnt < float(MOBA_TOPK))
            sel_ref[i] = jnp.where(picked, 1.0, 0.0)

    key = lax.broadcasted_iota(jnp.int32, (tb, cols), 0)
    qry = lax.broadcasted_iota(jnp.int32, (tb, cols), 1) % tb
    causal = key <= qry

    def scores(i, j):
        halves = []
        for h in range(2):
            lanes = slice(0, PAIR) if moba else slice(h * HEAD_SLOT, (h + 1) * HEAD_SLOT)
            halves.append(jnp.dot(k_ref[pl.ds(j * tb, tb), lanes], qt_ref[i, h],
                                  preferred_element_type=F32))
        return jnp.concatenate(halves, axis=1)

    def item(i, j, s):
        m_prev = m_ref[i]
        if i == j:
            s = jnp.where(causal, s, NEG_INF)
            m_new = jnp.maximum(m_prev, jnp.max(s, axis=0, keepdims=True))
            m_sub = m_new
        else:
            m_new = jnp.maximum(m_prev, jnp.max(s, axis=0, keepdims=True))
            m_sub = m_new
            if moba:
                chosen = sel_ref[i, pl.ds(j, 1), :] > 0.5
                m_new = jnp.where(chosen, m_new, m_prev)
                m_sub = jnp.where(chosen, m_new, -M_INIT)
        alpha = jnp.exp2(m_prev - m_new)
        p = jnp.exp2(s - m_sub).astype(BF16)
        m_ref[i] = m_new
        acc_ref[i] = alpha * acc_ref[i] + jnp.dot(vt_ref[j], p, preferred_element_type=F32)

    def finish(i):
        acc = acc_ref[i]
        o = acc[0:PAIR, :] / acc[PAIR:PAIR + 1, :]
        vrow = lax.broadcasted_iota(jnp.int32, (PAIR, tb), 0)
        ot = jnp.where(vrow < PAIR // 2, o[:, 0:tb], o[:, tb:cols])
        o_ref[pl.ds(i * tb, tb), :] = ot.T.astype(o_ref.dtype)

    for i in range(nq):
        prep(i)
    order = [(r, t) if t <= r else (nq - 1 - r, t - (r + 1))
             for t in range(nq + 1) for r in range(ATT_CHAINS)]
    pending = [scores(*order[n]) for n in range(ATT_AHEAD)]
    for n, (i, j) in enumerate(order):
        if n + ATT_AHEAD < len(order):
            pending.append(scores(*order[n + ATT_AHEAD]))
        item(i, j, pending.pop(0))
        if i == j:
            finish(i)


def _attnT_call(moba, q, k, v, kmean, batch, seq, name):
    T, width = q.shape
    pairs = MLA_HEADS // 2
    qw = width // pairs
    nq = seq // ATT_BLOCK
    assert nq == 2 * ATT_CHAINS, "chain pairing (r, nq-1-r) needs nq == 2 * ATT_CHAINS"
    cols = 2 * ATT_BLOCK
    seq_block = lambda w: pl.BlockSpec((seq, w), lambda b, p: (b, p))
    in_specs = [seq_block(qw), seq_block(qw), seq_block(PAIR)]
    scratch = [pltpu.VMEM((nq, 2, PAIR, ATT_BLOCK), BF16),
               pltpu.VMEM((nq, PAIR + ONES_ROWS, ATT_BLOCK), BF16),
               pltpu.VMEM((nq, 1, cols), F32),
               pltpu.VMEM((nq, PAIR + ONES_ROWS, cols), F32)]
    args = [q, k, v]
    if moba:
        in_specs.append(pl.BlockSpec((nq, PAIR), lambda b, p: (b, p)))
        scratch.append(pltpu.VMEM((nq, 16, cols), F32))
        args.append(kmean)
    return pl.pallas_call(
        functools.partial(_attnT_body, moba),
        grid=(batch, pairs),
        in_specs=in_specs,
        out_specs=seq_block(PAIR),
        out_shape=jax.ShapeDtypeStruct((T, pairs * PAIR), BF16),
        scratch_shapes=scratch,
        compiler_params=_cparams(("arbitrary", "arbitrary")),
        name=name,
    )(*args)


def _merge_body(oa_ref, ob_ref, x_ref, gout_ref, wout_ref, gffn_ref, wr_ref, br_ref, tri_ref,
                x1_ref, row_ref, meta_ref, counts_ref):
    ck = tri_ref.shape[0]
    n_chunks = x_ref.shape[0] // ck

    @pl.when(pl.program_id(0) == 0)
    def _():
        counts_ref[...] = jnp.zeros_like(counts_ref)

    ridx = lax.broadcasted_iota(jnp.int32, (LANES, ck), 0)
    ridx_f = ridx.astype(F32)
    st = [dict() for _ in range(n_chunks)]

    def rows(c):
        return pl.ds(c * ck, ck)

    def first_max(vals):
        mx = jnp.max(vals, axis=0, keepdims=True)
        idx = jnp.min(jnp.where(vals == mx, ridx_f, float(LANES)), axis=0, keepdims=True)
        return mx, idx

    def stage_a(c):
        oa = oa_ref[rows(c), :].astype(F32)
        ob = ob_ref[rows(c), :].astype(F32)
        na = oa * _rms(oa, 512) * gout_ref[:, 0:512]
        nb = ob * _rms(ob, 512) * gout_ref[:, 512:1024]
        st[c]["mixed"] = jnp.concatenate([na, nb], axis=1).astype(BF16)

    def stage_b(c):
        st[c]["proj"] = jnp.dot(st[c].pop("mixed"), wout_ref[...], preferred_element_type=F32)

    def stage_c(c):
        x1 = x_ref[rows(c), :] + st[c].pop("proj")
        x1_ref[rows(c), :] = x1
        h2 = x1 * _rms(x1, D_MODEL) * gffn_ref[...]
        st[c]["h2b"] = h2.astype(BF16)
        half = D_MODEL // 2
        lo = pltpu.bitcast(h2[:, 0:half].astype(BF16).astype(F32), jnp.uint32)
        hi = pltpu.bitcast(h2[:, half:D_MODEL].astype(BF16).astype(F32), jnp.uint32)
        row_ref[rows(c), 0:half] = (hi & jnp.uint32(0xFFFF0000)) | (lo >> 16)

    def stage_d(c):
        st[c]["logits"] = _nt_dot(wr_ref[...], st[c].pop("h2b")) + br_ref[...]

    def stage_e(c):
        logits = st[c].pop("logits")
        is_group = (ridx >= N_EXPERTS) & (ridx < N_EXPERTS + N_GROUPS)
        gl = jnp.where(is_group, logits, NEG_INF)
        gmax, gidx = first_max(gl)
        g_w = 1.0 / jnp.sum(jnp.exp(gl - gmax), axis=0, keepdims=True)
        g_sel = gidx - float(N_EXPERTS)
        in_group = (ridx < N_EXPERTS) & ((ridx // EPG).astype(F32) == g_sel)
        el = jnp.where(in_group, logits, NEG_INF)
        v1, i1 = first_max(el)
        v2, i2 = first_max(jnp.where(ridx_f == i1, NEG_INF, el))
        t = jnp.exp(v2 - v1)
        w1 = g_w * (1.0 / (1.0 + t))
        w2 = g_w * (t / (1.0 + t))
        swap = i2 < i1
        la = jnp.where(swap, i2, i1) - EPG * g_sel
        lb = jnp.where(swap, i1, i2) - EPG * g_sel
        cls = PAIRS_PER_GROUP * g_sel + la * (2 * EPG - 1 - la) * 0.5 + (lb - la - 1.0)
        st[c]["cls"] = cls
        st[c]["onehot"] = ridx_f == cls
        wa = jnp.where(swap, w2, w1)
        wb = jnp.where(swap, w1, w2)
        wt = jnp.where(ridx == 0, wa, jnp.where(ridx == 1, wb, 0.0))
        row_ref[rows(c), D_MODEL // 2:ROW_WORDS] = pltpu.bitcast(wt.T, jnp.uint32)

    def stage_f(c):
        ones = jnp.where(st[c]["onehot"], 1.0, 0.0).astype(BF16)
        st[c]["ones"] = ones
        st[c]["before"] = jnp.dot(ones, tri_ref[...], preferred_element_type=F32)

    def stage_g(c):
        before = st[c].pop("before") + counts_ref[...][:, 0:1]
        pos = jnp.sum(jnp.where(st[c].pop("onehot"), before, 0.0), axis=0, keepdims=True)
        total = jnp.sum(st[c].pop("ones").astype(F32), axis=1, keepdims=True)
        counts_ref[...] = counts_ref[...] + total
        srow = lax.broadcasted_iota(jnp.int32, (8, ck), 0)
        meta = jnp.where(srow == 0, st[c].pop("cls"), jnp.where(srow == 1, pos, 0.0))
        meta_ref[:, rows(c)] = meta.astype(jnp.int32)

    def run(stage, c):
        if 0 <= c < n_chunks:
            stage(c)

    stage_a(0)
    for s in range(n_chunks + 2):
        run(stage_f, s - 2)
        run(stage_c, s - 1)
        run(stage_d, s - 1)
        run(stage_b, s)
        run(stage_e, s - 1)
        run(stage_g, s - 2)
        run(stage_a, s + 1)


def _merge_call(oa, ob, x2, gout, wout, gffn, wr, br, tri):
    T = x2.shape[0]
    tm = MERGE_ROWS
    ck = tri.shape[0]
    const = lambda *shape: pl.BlockSpec(shape, lambda i: (0,) * len(shape))
    rows = lambda w: pl.BlockSpec((tm, w), lambda i: (i, 0))
    return pl.pallas_call(
        _merge_body,
        grid=(T // tm,),
        in_specs=[rows(512), rows(512), rows(D_MODEL), const(1, D_MODEL), const(D_MODEL, D_MODEL),
                  const(1, D_MODEL), const(LANES, D_MODEL), const(LANES, 1), const(ck, ck)],
        out_specs=[rows(D_MODEL), rows(ROW_WORDS), pl.BlockSpec((8, tm), lambda i: (0, i)),
                   const(LANES, LANES)],
        out_shape=[
            jax.ShapeDtypeStruct((T, D_MODEL), F32),
            jax.ShapeDtypeStruct((T, ROW_WORDS), jnp.uint32),
            jax.ShapeDtypeStruct((8, T), jnp.int32),
            jax.ShapeDtypeStruct((LANES, LANES), F32),
        ],
        compiler_params=_cparams(("arbitrary",)),
        name="merge",
    )(oa, ob, x2, gout, wout, gffn, wr, br, tri)


def _dispatch_body(pad_end_ref, nused_ref, dest_ref, row_ref, xs_ref, zero_buf, sem, zero_sem):
    n_tok = row_ref.shape[0]
    n_blocks = xs_ref.shape[0] // ROW_BLOCK

    @pl.when(pl.program_id(0) == 0)
    def _():
        zero_buf[...] = jnp.zeros_like(zero_buf)

        def block_copy(first_row):
            return pltpu.make_async_copy(
                zero_buf, xs_ref.at[pl.ds(pl.multiple_of(first_row, ROW_BLOCK), ROW_BLOCK)], zero_sem)

        def nonempty(c):
            return pad_end_ref[c] > jnp.where(c > 0, pad_end_ref[jnp.maximum(c - 1, 0)], 0)

        def start_class(c, carry):
            @pl.when(nonempty(c))
            def _():
                block_copy(pad_end_ref[c] - ROW_BLOCK).start()
            return carry

        def wait_class(c, carry):
            @pl.when(nonempty(c))
            def _():
                block_copy(pad_end_ref[c] - ROW_BLOCK).wait()
            return carry

        def start_tail(b, carry):
            block_copy(b * ROW_BLOCK).start()
            return carry

        def wait_tail(b, carry):
            block_copy(b * ROW_BLOCK).wait()
            return carry

        lax.fori_loop(0, N_CLASSES, start_class, 0)
        lax.fori_loop(nused_ref[0], n_blocks, start_tail, 0)
        lax.fori_loop(0, N_CLASSES, wait_class, 0)
        lax.fori_loop(nused_ref[0], n_blocks, wait_tail, 0)

    def start(t, c):
        pltpu.make_async_copy(row_ref.at[pl.ds(t, 1)], xs_ref.at[pl.ds(dest_ref[t], 1)], sem).start()
        return c

    lax.fori_loop(0, n_tok, start, 0, unroll=16)
    pltpu.make_async_copy(row_ref, xs_ref.at[pl.ds(0, n_tok)], sem).wait()


def _dispatch_call(pad_end, n_used, dest, rows, n_blocks, tok_per_step):
    T, width = rows.shape
    grid_spec = pltpu.PrefetchScalarGridSpec(
        num_scalar_prefetch=2,
        grid=(T // tok_per_step,),
        in_specs=[
            pl.BlockSpec((tok_per_step,), lambda i, pe, nu: (i,), memory_space=pltpu.SMEM),
            pl.BlockSpec((tok_per_step, width), lambda i, pe, nu: (i, 0)),
        ],
        out_specs=pl.BlockSpec(memory_space=pl.ANY),
        scratch_shapes=[pltpu.VMEM((ROW_BLOCK, width), rows.dtype),
                        pltpu.SemaphoreType.DMA(()),
                        pltpu.SemaphoreType.DMA(())],
    )
    return pl.pallas_call(
        _dispatch_body,
        grid_spec=grid_spec,
        out_shape=jax.ShapeDtypeStruct((n_blocks * ROW_BLOCK, width), rows.dtype),
        compiler_params=_cparams(("arbitrary",)),
        name="dispatch",
    )(pad_end, n_used, dest, rows)


def _expert_body(bea_ref, beb_ref, nused_ref, xs_ref, wga_ref, wua_ref, wda_ref,
                 wgb_ref, wub_ref, wdb_ref, y_ref):
    del bea_ref, beb_ref
    used = pl.program_id(0) < nused_ref[0]
    half = D_MODEL // 2

    @pl.when(used)
    def _():
        row = xs_ref[...]
        w = row[:, 0:half]
        lo = pltpu.bitcast(w << 16, F32)
        hi = pltpu.bitcast(w & jnp.uint32(0xFFFF0000), F32)
        xb = jnp.concatenate([lo, hi], axis=1).astype(BF16)
        wts = pltpu.bitcast(row[:, half:ROW_WORDS], F32)

        def hidden(wg_ref, wu_ref):
            a = jnp.dot(xb, wg_ref[...], preferred_element_type=F32)
            u = jnp.dot(xb, wu_ref[...], preferred_element_type=F32)
            return (a * (1.0 / (1.0 + jnp.exp(-a))) * u).astype(BF16)

        act_a = hidden(wga_ref, wua_ref)
        act_b = hidden(wgb_ref, wub_ref)
        y_ref[...] = (wts[:, 0:1] * jnp.dot(act_a, wda_ref[...], preferred_element_type=F32)
                      + wts[:, 1:2] * jnp.dot(act_b, wdb_ref[...], preferred_element_type=F32))

    @pl.when(jnp.logical_not(used))
    def _():
        y_ref[...] = jnp.zeros_like(y_ref)


def _expert_call(block_ea, block_eb, n_used, xs, wg, wu, wd):
    P = xs.shape[0]
    n_blocks = P // ROW_BLOCK

    def row_map(i, bea, beb, nu):
        return (jnp.minimum(i, nu[0] - 1), 0)

    def wa_map(i, bea, beb, nu):
        return (bea[jnp.minimum(i, nu[0] - 1)], 0, 0)

    def wb_map(i, bea, beb, nu):
        return (beb[jnp.minimum(i, nu[0] - 1)], 0, 0)

    up = (None, D_MODEL, EXPERT_HIDDEN)
    down = (None, EXPERT_HIDDEN, D_MODEL)
    grid_spec = pltpu.PrefetchScalarGridSpec(
        num_scalar_prefetch=3,
        grid=(n_blocks,),
        in_specs=[
            pl.BlockSpec((ROW_BLOCK, ROW_WORDS), row_map),
            pl.BlockSpec(up, wa_map), pl.BlockSpec(up, wa_map), pl.BlockSpec(down, wa_map),
            pl.BlockSpec(up, wb_map), pl.BlockSpec(up, wb_map), pl.BlockSpec(down, wb_map),
        ],
        out_specs=pl.BlockSpec((ROW_BLOCK, D_MODEL), lambda i, bea, beb, nu: (i, 0)),
    )
    return pl.pallas_call(
        _expert_body,
        grid_spec=grid_spec,
        out_shape=jax.ShapeDtypeStruct((P, D_MODEL), F32),
        compiler_params=_cparams(("arbitrary",)),
        name="experts",
    )(block_ea, block_eb, n_used, xs, wg, wu, wd, wg, wu, wd)


def _combine_body(dest_ref, dest_next_ref, x1_ref, ys_ref, o_ref, y_buf, sems):
    n_tok = x1_ref.shape[0]
    step = pl.program_id(0)
    slot = step % 2

    def gather(idx_ref, to_slot):
        def start(t, c):
            pltpu.make_async_copy(ys_ref.at[pl.ds(idx_ref[t], 1)],
                                  y_buf.at[to_slot, pl.ds(t, 1)], sems.at[to_slot]).start()
            return c
        lax.fori_loop(0, n_tok, start, 0, unroll=16)

    @pl.when(step == 0)
    def _():
        gather(dest_ref, 0)

    @pl.when(step + 1 < pl.num_programs(0))
    def _():
        gather(dest_next_ref, 1 - slot)

    pltpu.make_async_copy(ys_ref.at[pl.ds(0, n_tok)], y_buf.at[slot], sems.at[slot]).wait()
    o_ref[...] = x1_ref[...] + y_buf[slot]


def _combine_call(dest, x1, ys, tok_per_step):
    T = x1.shape[0]
    n_steps = T // tok_per_step
    return pl.pallas_call(
        _combine_body,
        grid=(n_steps,),
        in_specs=[
            pl.BlockSpec((tok_per_step,), lambda i: (i,), memory_space=pltpu.SMEM),
            pl.BlockSpec((tok_per_step,), lambda i: (jnp.minimum(i + 1, n_steps - 1),),
                         memory_space=pltpu.SMEM),
            pl.BlockSpec((tok_per_step, D_MODEL), lambda i: (i, 0)),
            pl.BlockSpec(memory_space=pl.ANY),
        ],
        out_specs=pl.BlockSpec((tok_per_step, D_MODEL), lambda i: (i, 0)),
        out_shape=jax.ShapeDtypeStruct((T, D_MODEL), F32),
        scratch_shapes=[pltpu.VMEM((2, tok_per_step, D_MODEL), F32),
                        pltpu.SemaphoreType.DMA((2,))],
        compiler_params=_cparams(("arbitrary",)),
        name="combine",
    )(dest, dest, x1, ys)


def _layer(x, attn_norm_g, w_in, mla_q_lat_g, w_q_up, mla_kv_lat_g, w_kv_up, mla_q_head_g,
           mla_k_head_g, moba_q_head_g, moba_k_head_g, out_norm_mla_g, out_norm_moba_g, w_out,
           ffn_norm_g, w_router_group, b_router_group, w_router_expert, b_router_expert,
           w_gate, w_up, w_down):
    B, S, D = x.shape
    T = B * S
    x2 = x.reshape(T, D)

    win, wq, wkv, gains = _prep_weights(w_in, w_q_up, w_kv_up, mla_q_head_g, mla_k_head_g,
                                        moba_q_head_g, moba_k_head_g)
    glat = jnp.concatenate([mla_q_lat_g, mla_kv_lat_g])[None, :]
    qa, ka, va, qb, kb, vb, kmean = _proj_call(
        x2, attn_norm_g[None, :], win, glat, wq, wkv, gains, _group_matrices(), _rope_tables(S), S)

    oa = _attnT_call(False, qa, ka, va, None, B, S, "mla")
    ob = _attnT_call(True, qb, kb, vb, kmean.reshape(T // MOBA_BLOCK, 512), B, S, "moba")

    wr = jnp.zeros((LANES, D), F32).at[:N_EXPERTS].set(w_router_expert.T)
    wr = wr.at[N_EXPERTS:N_EXPERTS + N_GROUPS].set(w_router_group.T).astype(BF16)
    br = jnp.zeros((LANES, 1), F32).at[:N_EXPERTS, 0].set(b_router_expert)
    br = br.at[N_EXPERTS:N_EXPERTS + N_GROUPS, 0].set(b_router_group)
    tri = jnp.asarray(np.triu(np.ones((MERGE_CHUNK, MERGE_CHUNK), np.float32), 1), BF16)
    gout = jnp.concatenate([out_norm_mla_g, out_norm_moba_g])[None, :]
    x1, rows, meta, counts = _merge_call(oa, ob, x2, gout, w_out.astype(BF16), ffn_norm_g[None, :],
                                         wr, br, tri)

    counts = counts[:N_CLASSES, 0].astype(jnp.int32)
    padded = (counts + ROW_BLOCK - 1) // ROW_BLOCK * ROW_BLOCK
    pad_end = jnp.cumsum(padded)
    pad_start = pad_end - padded
    n_blocks = (T + ROW_BLOCK - 1) // ROW_BLOCK + N_CLASSES
    n_used = (pad_end[-1] // ROW_BLOCK).astype(jnp.int32)
    blk = jnp.minimum(jnp.arange(n_blocks, dtype=jnp.int32), n_used - 1) * ROW_BLOCK
    block_class = jnp.minimum(jnp.sum(pad_end[None, :] <= blk[:, None], axis=1), N_CLASSES - 1)
    pairs = [(a, b) for a in range(EPG) for b in range(a + 1, EPG)]
    class_ea = np.array([g * EPG + a for g in range(N_GROUPS) for a, _ in pairs], np.int32)
    class_eb = np.array([g * EPG + b for g in range(N_GROUPS) for _, b in pairs], np.int32)
    block_ea = jnp.take(jnp.asarray(class_ea), block_class)
    block_eb = jnp.take(jnp.asarray(class_eb), block_class)
    is_class = meta[0][:, None] == jnp.arange(N_CLASSES, dtype=jnp.int32)[None, :]
    dest = (jnp.sum(jnp.where(is_class, pad_start[None, :], 0), axis=1) + meta[1]).astype(jnp.int32)

    xs = _dispatch_call(pad_end.astype(jnp.int32), n_used[None], dest, rows, n_blocks, 1024)
    ys = _expert_call(block_ea, block_eb, n_used[None], xs,
                      w_gate.astype(BF16), w_up.astype(BF16), w_down.astype(BF16))
    out = _combine_call(dest, x1, ys, 512)
    return out.reshape(B, S, D)


def kernel(x, attn_norm_g, w_in, mla_q_lat_g, w_q_up, mla_kv_lat_g, w_kv_up, mla_q_head_g, mla_k_head_g, moba_q_head_g, moba_k_head_g, out_norm_mla_g, out_norm_moba_g, w_out, ffn_norm_g, w_router_group, b_router_group, w_router_expert, b_router_expert, w_gate, w_up, w_down):
    return _layer(x, attn_norm_g[0], w_in[0], mla_q_lat_g[0], w_q_up[0], mla_kv_lat_g[0], w_kv_up[0],
                  mla_q_head_g[0], mla_k_head_g[0], moba_q_head_g[0], moba_k_head_g[0],
                  out_norm_mla_g[0], out_norm_moba_g[0], w_out[0], ffn_norm_g[0],
                  w_router_group[0], b_router_group[0], w_router_expert[0], b_router_expert[0],
                  w_gate[0], w_up[0], w_down[0])
```

```python
import functools

import jax
import jax.numpy as jnp
import numpy as np
from jax import lax
from jax.experimental import pallas as pl
from jax.experimental.pallas import tpu as pltpu

F32 = jnp.float32
BF16 = jnp.bfloat16

D_MODEL = 1024
MLA_HEADS = 8
MLA_NOPE = 64
MLA_ROPE = 32
MLA_QK = MLA_NOPE + MLA_ROPE
MLA_V = 64
MLA_Q_LORA = 256
MLA_KV_LORA = 128
MOBA_HEADS = 8
MOBA_D = 64
MOBA_BLOCK = 256
MOBA_TOPK = 3
ROPE_THETA = 10000.0
EPS = 1e-6
N_GROUPS = 4
EPG = 8
N_EXPERTS = N_GROUPS * EPG
EXPERT_HIDDEN = 256
ROW_BLOCK = 256
PAIRS_PER_GROUP = EPG * (EPG - 1) // 2
N_CLASSES = N_GROUPS * PAIRS_PER_GROUP
ROW_WORDS = D_MODEL // 2 + 128

LANES = 128
HEAD_SLOT = LANES
PAIR = LANES
PROJ_WIDTH = 2048
MERGE_ROWS = 1024
MERGE_CHUNK = 256
PROJ_CHUNK = 256
PROJ_ROWS = 1024
NEG_INF = float("-inf")

VMEM_LIMIT = 48 * 1024 * 1024


def _cparams(sem):
    return pltpu.CompilerParams(dimension_semantics=sem, vmem_limit_bytes=VMEM_LIMIT)


def _mla_slot_index():
    idx = -np.ones((HEAD_SLOT,), np.int64)
    idx[0:32] = np.arange(0, 32)
    idx[32:48] = MLA_NOPE + np.arange(0, 16)
    idx[64:96] = np.arange(32, 64)
    idx[96:112] = MLA_NOPE + np.arange(16, 32)
    return idx


def _moba_pair_index():
    head = np.concatenate([np.zeros(32), np.ones(32), np.zeros(32), np.ones(32)]).astype(np.int64)
    feat = np.concatenate([np.arange(32), np.arange(32), 32 + np.arange(32), 32 + np.arange(32)])
    return head, feat


def _gather_cols(w, idx):
    safe = np.where(idx < 0, 0, idx)
    out = jnp.take(w, jnp.asarray(safe), axis=-1)
    return jnp.where(jnp.asarray(idx < 0), 0.0, out)


def _prep_weights(w_in, w_q_up, w_kv_up, q_head_g, k_head_g, mq_g, mk_g):
    slot = _mla_slot_index()
    head, feat = _moba_pair_index()

    kpe_idx = np.where(slot >= MLA_NOPE, slot - MLA_NOPE, -1)
    off_kpe = MLA_Q_LORA + MLA_KV_LORA
    off_mq = off_kpe + MLA_ROPE
    off_mk = off_mq + MOBA_HEADS * MOBA_D
    off_mv = off_mk + MOBA_HEADS * MOBA_D
    moba_idx = np.concatenate([(2 * p + head) * MOBA_D + feat for p in range(MOBA_HEADS // 2)])
    cols = np.concatenate([
        np.arange(0, off_kpe),
        np.where(kpe_idx < 0, -1, off_kpe + kpe_idx),
        off_mq + moba_idx,
        off_mk + moba_idx,
        off_mv + np.arange(MOBA_HEADS * MOBA_D),
    ])
    win = _gather_cols(w_in, cols).astype(BF16)

    q_idx = np.concatenate([np.where(slot < 0, -1, h * MLA_QK + slot) for h in range(MLA_HEADS)])
    wq = _gather_cols(w_q_up, q_idx).astype(BF16)

    nope_slot = np.where((slot >= 0) & (slot < MLA_NOPE), slot, -1)
    kn_idx = np.concatenate([np.where(nope_slot < 0, -1, h * (MLA_NOPE + MLA_V) + nope_slot)
                             for h in range(MLA_HEADS)])
    v_idx = np.concatenate([h * (MLA_NOPE + MLA_V) + MLA_NOPE + np.arange(MLA_V) for h in range(MLA_HEADS)])
    wkv = _gather_cols(w_kv_up, np.concatenate([kn_idx, v_idx])).astype(BF16)

    gq = _gather_cols(q_head_g, slot) * (MLA_QK ** -0.5 * LOG2E)
    gk = _gather_cols(k_head_g, slot)
    gmq = jnp.take(mq_g, jnp.asarray(feat)) * (MOBA_D ** -0.5 * LOG2E)
    gmk = jnp.take(mk_g, jnp.asarray(feat))
    gains = jnp.stack([gq, gk, gmq, gmk]).astype(F32)
    return win, wq, wkv, gains


def _rope_tables(seq):
    def tab(dim):
        inv = ROPE_THETA ** (-(jnp.arange(0, dim, 2, dtype=F32) / dim))
        ang = jnp.arange(seq, dtype=F32)[:, None] * inv[None, :]
        return jnp.cos(ang), jnp.sin(ang)

    cr, sr = tab(MLA_ROPE)
    cf, sf = tab(MOBA_D)
    one = jnp.ones((seq, 16), F32)
    zero = jnp.zeros((seq, 16), F32)
    cos_a = jnp.concatenate([one, one, cr, one, one, one, cr, one], axis=1)
    sin_a = jnp.concatenate([zero, zero, -sr, zero, zero, zero, sr, zero], axis=1)
    cos_b = jnp.concatenate([cf, cf, cf, cf], axis=1)
    sin_b = jnp.concatenate([-sf, -sf, sf, sf], axis=1)
    return jnp.stack([cos_a, sin_a, cos_b, sin_b])


def _group_matrices():
    lane = np.arange(256)
    g_mla = (lane[:, None] // HEAD_SLOT == lane[None, :] // HEAD_SLOT)
    head, _ = _moba_pair_index()
    hid = np.concatenate([head, 2 + head])
    g_moba = hid[:, None] == hid[None, :]
    return jnp.asarray(np.stack([g_mla, g_moba]), BF16)


def _rms(x, width):
    return lax.rsqrt(jnp.sum(x * x, axis=-1, keepdims=True) * (1.0 / width) + EPS)


def _rope(t, cos, sin):
    outs = []
    for c in range(0, t.shape[1], LANES):
        xc = t[:, c:c + LANES]
        outs.append(xc * cos + pltpu.roll(xc, 64, 1) * sin)
    return jnp.concatenate(outs, axis=1)


def _proj_body(x_ref, gx_ref, win_ref, glat_ref, wq_ref, wkv_ref, gains_ref, gmat_ref, rope_ref,
               qa_ref, ka_ref, va_ref, qb_ref, kb_ref, vb_ref, kmean_ref):
    n_chunks = x_ref.shape[0] // PROJ_CHUNK
    gains = gains_ref[...]
    g_mla = gmat_ref[0]
    g_moba = gmat_ref[1]

    def tile_gain(row, n):
        return jnp.concatenate([gains[row:row + 1, :]] * n, axis=1)

    def rows(c):
        return pl.ds(c * PROJ_CHUNK, PROJ_CHUNK)

    def group_sums(sq, gmat):
        return jnp.concatenate([jnp.dot(sq[:, c:c + 256], gmat, preferred_element_type=F32)
                                for c in range(0, sq.shape[1], 256)], axis=1)

    def finish(t, ssum, dim, gain, cos, sin):
        return _rope(t * lax.rsqrt(ssum * (1.0 / dim) + EPS) * gain, cos, sin)

    st = [dict() for _ in range(n_chunks)]

    def stage_a(c):
        x = x_ref[rows(c), :]
        st[c]["h"] = (x * _rms(x, D_MODEL) * gx_ref[...]).astype(BF16)

    def stage_b(c):
        st[c]["proj"] = jnp.dot(st[c].pop("h"), win_ref[...], preferred_element_type=F32)

    def stage_c(c):
        proj = st[c].pop("proj")
        q_lat = proj[:, 0:256]
        kv_lat = proj[:, 256:384]
        st[c]["ql"] = (q_lat * _rms(q_lat, MLA_Q_LORA) * glat_ref[:, 0:256]).astype(BF16)
        st[c]["kvl"] = (kv_lat * _rms(kv_lat, MLA_KV_LORA) * glat_ref[:, 256:384]).astype(BF16)
        st[c]["kpe"] = proj[:, 384:512]
        st[c]["mq"] = proj[:, 512:1024]
        st[c]["mk"] = proj[:, 1024:1536]
        st[c]["mq2"] = (st[c]["mq"] * st[c]["mq"]).astype(BF16)
        st[c]["mk2"] = (st[c]["mk"] * st[c]["mk"]).astype(BF16)
        vb_ref[rows(c), :] = proj[:, 1536:2048].astype(BF16)

    def stage_d(c):
        st[c]["qa"] = jnp.dot(st[c].pop("ql"), wq_ref[...], preferred_element_type=F32)
        st[c]["kv"] = jnp.dot(st[c].pop("kvl"), wkv_ref[...], preferred_element_type=F32)
        st[c]["mq_ss"] = group_sums(st[c].pop("mq2"), g_moba)
        st[c]["mk_ss"] = group_sums(st[c].pop("mk2"), g_moba)

    def stage_e(c):
        cos_b, sin_b = rope_ref[2, rows(c), :], rope_ref[3, rows(c), :]
        qb_ref[rows(c), :] = finish(st[c].pop("mq"), st[c].pop("mq_ss"), MOBA_D,
                                    tile_gain(2, 4), cos_b, sin_b).astype(BF16)
        kb = finish(st[c].pop("mk"), st[c].pop("mk_ss"), MOBA_D, tile_gain(3, 4), cos_b, sin_b)
        kb_ref[rows(c), :] = kb.astype(BF16)
        part = jnp.sum(kb, axis=0, keepdims=True) * (1.0 / MOBA_BLOCK)
        per_block = MOBA_BLOCK // PROJ_CHUNK
        if c % per_block == 0:
            kmean_ref[c // per_block] = part
        else:
            kmean_ref[c // per_block] = kmean_ref[c // per_block] + part
        kv = st[c].pop("kv")
        va_ref[rows(c), :] = kv[:, 1024:1536].astype(BF16)
        st[c]["ka"] = kv[:, 0:1024] + jnp.concatenate([st[c].pop("kpe")] * MLA_HEADS, axis=1)
        st[c]["qa2"] = (st[c]["qa"] * st[c]["qa"]).astype(BF16)
        st[c]["ka2"] = (st[c]["ka"] * st[c]["ka"]).astype(BF16)

    def stage_f(c):
        st[c]["qa_ss"] = group_sums(st[c].pop("qa2"), g_mla)
        st[c]["ka_ss"] = group_sums(st[c].pop("ka2"), g_mla)

    def stage_g(c):
        cos_a, sin_a = rope_ref[0, rows(c), :], rope_ref[1, rows(c), :]
        qa_ref[rows(c), :] = finish(st[c].pop("qa"), st[c].pop("qa_ss"), MLA_QK,
                                    tile_gain(0, 8), cos_a, sin_a).astype(BF16)
        ka_ref[rows(c), :] = finish(st[c].pop("ka"), st[c].pop("ka_ss"), MLA_QK,
                                    tile_gain(1, 8), cos_a, sin_a).astype(BF16)

    def run(stage, c):
        if 0 <= c < n_chunks:
            stage(c)

    stage_a(0)
    for s in range(n_chunks + 2):
        run(stage_f, s - 2)
        run(stage_c, s - 1)
        run(stage_d, s - 1)
        run(stage_b, s)
        run(stage_e, s - 1)
        run(stage_g, s - 2)
        run(stage_a, s + 1)


def _proj_call(x2, gx, win, glat, wq, wkv, gains, gmat, rope, seq):
    T = x2.shape[0]
    tm = PROJ_ROWS
    nt = T // tm
    spb = seq // tm
    bpt = tm // MOBA_BLOCK
    const = lambda *shape: pl.BlockSpec(shape, lambda i: (0,) * len(shape))
    rows = lambda w: pl.BlockSpec((tm, w), lambda i: (i, 0))
    return pl.pallas_call(
        _proj_body,
        grid=(nt,),
        in_specs=[
            rows(D_MODEL),
            const(1, D_MODEL),
            const(D_MODEL, PROJ_WIDTH),
            const(1, 384),
            const(MLA_Q_LORA, 1024),
            const(MLA_KV_LORA, 1536),
            const(4, LANES),
            const(2, 256, 256),
            pl.BlockSpec((4, tm, LANES), lambda i: (0, i % spb, 0)),
        ],
        out_specs=[rows(1024), rows(1024), rows(512), rows(512), rows(512), rows(512),
                   pl.BlockSpec((bpt, 1, 512), lambda i: (i, 0, 0))],
        out_shape=[
            jax.ShapeDtypeStruct((T, 1024), BF16),
            jax.ShapeDtypeStruct((T, 1024), BF16),
            jax.ShapeDtypeStruct((T, 512), BF16),
            jax.ShapeDtypeStruct((T, 512), BF16),
            jax.ShapeDtypeStruct((T, 512), BF16),
            jax.ShapeDtypeStruct((T, 512), BF16),
            jax.ShapeDtypeStruct((nt * bpt, 1, 512), F32),
        ],
        compiler_params=_cparams(("arbitrary",)),
        name="proj",
    )(x2, gx, win, glat, wq, wkv, gains, gmat, rope)


def _nt_dot(a, b):
    return lax.dot_general(a, b, (((1,), (1,)), ((), ())), preferred_element_type=F32)


ATT_BLOCK = MOBA_BLOCK
ATT_CHAINS = 4
ATT_PAIRS_PER_STEP = 2
ATT_AHEAD = 2
M_INIT = -1e30
ONES_ROWS = 16
LOG2E = 1.4426950408889634


def _attnT_body(moba, q_ref, k_ref, v_ref, *rest):
    tb = ATT_BLOCK
    nq = q_ref.shape[0] // tb
    n_pairs = v_ref.shape[1] // PAIR
    qw = q_ref.shape[1] // n_pairs
    order = [(r, t) if t <= r else (nq - 1 - r, t - (r + 1))
             for t in range(nq + 1) for r in range(ATT_CHAINS)]
    pair_fns = [_attn_pair(moba, pp, qw, q_ref, k_ref, v_ref, rest) for pp in range(n_pairs)]
    for prep, _, _, _ in pair_fns:
        for i in range(nq):
            prep(i)
    stream = [(pp, i, j) for pp in range(n_pairs) for i, j in order]
    pending = [pair_fns[pp][1](i, j) for pp, i, j in stream[:ATT_AHEAD]]
    for n, (pp, i, j) in enumerate(stream):
        if n + ATT_AHEAD < len(stream):
            nxt = stream[n + ATT_AHEAD]
            pending.append(pair_fns[nxt[0]][1](nxt[1], nxt[2]))
        pair_fns[pp][2](i, j, pending.pop(0))
        if i == j:
            pair_fns[pp][3](i)


def _attn_pair(moba, pp, qw, q_all, k_all, v_all, rest):
    if moba:
        km_all, o_all, qt_all, vt_all, m_all, acc_all, sel_all = rest
        km_ref = km_all.at[:, pl.ds(pp * PAIR, PAIR)]
        sel_ref = sel_all.at[pp]
    else:
        o_all, qt_all, vt_all, m_all, acc_all = rest
    q_ref = q_all.at[:, pl.ds(pp * qw, qw)]
    k_ref = k_all.at[:, pl.ds(pp * qw, qw)]
    v_ref = v_all.at[:, pl.ds(pp * PAIR, PAIR)]
    o_ref = o_all.at[:, pl.ds(pp * PAIR, PAIR)]
    qt_ref, vt_ref, m_ref, acc_ref = qt_all.at[pp], vt_all.at[pp], m_all.at[pp], acc_all.at[pp]
    tb = ATT_BLOCK
    cols = 2 * tb
    nq = q_ref.shape[0] // tb
    lane_q = lax.broadcasted_iota(jnp.int32, (tb, qw), 1)

    if moba:
        km = km_ref[...]
        km_hi = km.astype(BF16)
        km_lo = (km - km_hi.astype(F32)).astype(BF16)
        zpad = jnp.zeros((16 - nq, PAIR), BF16)
        km_hi = jnp.concatenate([km_hi, zpad], axis=0)
        km_lo = jnp.concatenate([km_lo, zpad], axis=0)
        blk = lax.broadcasted_iota(jnp.int32, (16, cols), 0)

    def prep(i):
        start = i * tb
        q = q_ref[pl.ds(start, tb), :]
        if moba:
            head_of_lane = (lane_q // 32) % 2
            zero = jnp.zeros_like(q)
            per_head = [jnp.where(head_of_lane == h, q, zero) for h in range(2)]
        else:
            per_head = [q[:, h * HEAD_SLOT:(h + 1) * HEAD_SLOT] for h in range(2)]
        qts = [qh.astype(F32).T.astype(BF16) for qh in per_head]
        for h in range(2):
            qt_ref[i, h] = qts[h]
        vt = v_ref[pl.ds(start, tb), :].astype(F32).T.astype(BF16)
        vt_ref[i] = jnp.concatenate([vt, jnp.ones((ONES_ROWS, tb), BF16)], axis=0)
        m_ref[i] = jnp.full((1, cols), M_INIT, F32)
        acc_ref[i] = jnp.zeros((PAIR + ONES_ROWS, cols), F32)
        if moba:
            qt = jnp.concatenate(qts, axis=1)
            gate = (jnp.dot(km_hi, qt, preferred_element_type=F32)
                    + jnp.dot(km_lo, qt, preferred_element_type=F32))
            cnt = jnp.zeros((16, cols), F32)
            for b in range(i):
                gb = gate[b:b + 1, :]
                beats = (gb > gate) | ((gb == gate) & (blk > b))
                cnt = cnt + jnp.where(beats, 1.0, 0.0)
            picked = (blk < i) & (cnt < float(MOBA_TOPK))
            sel_ref[i] = jnp.where(picked, 1.0, 0.0)

    key = lax.broadcasted_iota(jnp.int32, (tb, cols), 0)
    qry = lax.broadcasted_iota(jnp.int32, (tb, cols), 1) % tb
    causal = key <= qry

    def scores(i, j):
        halves = []
        for h in range(2):
            lanes = slice(0, PAIR) if moba else slice(h * HEAD_SLOT, (h + 1) * HEAD_SLOT)
            halves.append(jnp.dot(k_ref[pl.ds(j * tb, tb), lanes], qt_ref[i, h],
                                  preferred_element_type=F32))
        return jnp.concatenate(halves, axis=1)

    def item(i, j, s):
        m_prev = m_ref[i]
        if i == j:
            s = jnp.where(causal, s, NEG_INF)
            m_new = jnp.maximum(m_prev, jnp.max(s, axis=0, keepdims=True))
            m_sub = m_new
        else:
            m_new = jnp.maximum(m_prev, jnp.max(s, axis=0, keepdims=True))
            m_sub = m_new
            if moba:
                chosen = sel_ref[i, pl.ds(j, 1), :] > 0.5
                m_new = jnp.where(chosen, m_new, m_prev)
                m_sub = jnp.where(chosen, m_new, -M_INIT)
        alpha = jnp.exp2(m_prev - m_new)
        p = jnp.exp2(s - m_sub).astype(BF16)
        m_ref[i] = m_new
        acc_ref[i] = alpha * acc_ref[i] + jnp.dot(vt_ref[j], p, preferred_element_type=F32)

    def finish(i):
        acc = acc_ref[i]
        o = acc[0:PAIR, :] / acc[PAIR:PAIR + 1, :]
        vrow = lax.broadcasted_iota(jnp.int32, (PAIR, tb), 0)
        ot = jnp.where(vrow < PAIR // 2, o[:, 0:tb], o[:, tb:cols])
        o_ref[pl.ds(i * tb, tb), :] = ot.T.astype(o_ref.dtype)

    return prep, scores, item, finish


def _attnT_call(moba, q, k, v, kmean, batch, seq, name):
    T, width = q.shape
    pairs = MLA_HEADS // 2
    qw = width // pairs
    nq = seq // ATT_BLOCK
    assert nq == 2 * ATT_CHAINS, "chain pairing (r, nq-1-r) needs nq == 2 * ATT_CHAINS"
    cols = 2 * ATT_BLOCK
    npp = ATT_PAIRS_PER_STEP
    seq_block = lambda w: pl.BlockSpec((seq, npp * w), lambda b, p: (b, p))
    in_specs = [seq_block(qw), seq_block(qw), seq_block(PAIR)]
    scratch = [pltpu.VMEM((npp, nq, 2, PAIR, ATT_BLOCK), BF16),
               pltpu.VMEM((npp, nq, PAIR + ONES_ROWS, ATT_BLOCK), BF16),
               pltpu.VMEM((npp, nq, 1, cols), F32),
               pltpu.VMEM((npp, nq, PAIR + ONES_ROWS, cols), F32)]
    args = [q, k, v]
    if moba:
        in_specs.append(pl.BlockSpec((nq, npp * PAIR), lambda b, p: (b, p)))
        scratch.append(pltpu.VMEM((npp, nq, 16, cols), F32))
        args.append(kmean)
    return pl.pallas_call(
        functools.partial(_attnT_body, moba),
        grid=(batch, pairs // npp),
        in_specs=in_specs,
        out_specs=seq_block(PAIR),
        out_shape=jax.ShapeDtypeStruct((T, pairs * PAIR), BF16),
        scratch_shapes=scratch,
        compiler_params=_cparams(("arbitrary", "arbitrary")),
        name=name,
    )(*args)


def _merge_body(oa_ref, ob_ref, x_ref, gout_ref, wout_ref, gffn_ref, wr_ref, br_ref, tri_ref,
                x1_ref, row_ref, meta_ref, counts_ref):
    ck = tri_ref.shape[0]
    n_chunks = x_ref.shape[0] // ck

    @pl.when(pl.program_id(0) == 0)
    def _():
        counts_ref[...] = jnp.zeros_like(counts_ref)

    ridx = lax.broadcasted_iota(jnp.int32, (LANES, ck), 0)
    ridx_f = ridx.astype(F32)
    st = [dict() for _ in range(n_chunks)]

    def rows(c):
        return pl.ds(c * ck, ck)

    def first_max(vals):
        mx = jnp.max(vals, axis=0, keepdims=True)
        idx = jnp.min(jnp.where(vals == mx, ridx_f, float(LANES)), axis=0, keepdims=True)
        return mx, idx

    def stage_a(c):
        oa = oa_ref[rows(c), :].astype(F32)
        ob = ob_ref[rows(c), :].astype(F32)
        na = oa * _rms(oa, 512) * gout_ref[:, 0:512]
        nb = ob * _rms(ob, 512) * gout_ref[:, 512:1024]
        st[c]["mixed"] = jnp.concatenate([na, nb], axis=1).astype(BF16)

    def stage_b(c):
        st[c]["proj"] = jnp.dot(st[c].pop("mixed"), wout_ref[...], preferred_element_type=F32)

    def stage_c(c):
        x1 = x_ref[rows(c), :] + st[c].pop("proj")
        x1_ref[rows(c), :] = x1
        h2 = x1 * _rms(x1, D_MODEL) * gffn_ref[...]
        st[c]["h2b"] = h2.astype(BF16)
        half = D_MODEL // 2
        lo = pltpu.bitcast(h2[:, 0:half].astype(BF16).astype(F32), jnp.uint32)
        hi = pltpu.bitcast(h2[:, half:D_MODEL].astype(BF16).astype(F32), jnp.uint32)
        row_ref[rows(c), 0:half] = (hi & jnp.uint32(0xFFFF0000)) | (lo >> 16)

    def stage_d(c):
        st[c]["logits"] = _nt_dot(wr_ref[...], st[c].pop("h2b")) + br_ref[...]

    def stage_e(c):
        logits = st[c].pop("logits")
        is_group = (ridx >= N_EXPERTS) & (ridx < N_EXPERTS + N_GROUPS)
        gl = jnp.where(is_group, logits, NEG_INF)
        gmax, gidx = first_max(gl)
        g_w = 1.0 / jnp.sum(jnp.exp(gl - gmax), axis=0, keepdims=True)
        g_sel = gidx - float(N_EXPERTS)
        in_group = (ridx < N_EXPERTS) & ((ridx // EPG).astype(F32) == g_sel)
        el = jnp.where(in_group, logits, NEG_INF)
        v1, i1 = first_max(el)
        v2, i2 = first_max(jnp.where(ridx_f == i1, NEG_INF, el))
        t = jnp.exp(v2 - v1)
        w1 = g_w * (1.0 / (1.0 + t))
        w2 = g_w * (t / (1.0 + t))
        swap = i2 < i1
        la = jnp.where(swap, i2, i1) - EPG * g_sel
        lb = jnp.where(swap, i1, i2) - EPG * g_sel
        cls = PAIRS_PER_GROUP * g_sel + la * (2 * EPG - 1 - la) * 0.5 + (lb - la - 1.0)
        st[c]["cls"] = cls
        st[c]["onehot"] = ridx_f == cls
        wa = jnp.where(swap, w2, w1)
        wb = jnp.where(swap, w1, w2)
        wt = jnp.where(ridx == 0, wa, jnp.where(ridx == 1, wb, 0.0))
        row_ref[rows(c), D_MODEL // 2:ROW_WORDS] = pltpu.bitcast(wt.T, jnp.uint32)

    def stage_f(c):
        ones = jnp.where(st[c]["onehot"], 1.0, 0.0).astype(BF16)
        st[c]["ones"] = ones
        st[c]["before"] = jnp.dot(ones, tri_ref[...], preferred_element_type=F32)

    def stage_g(c):
        before = st[c].pop("before") + counts_ref[...][:, 0:1]
        pos = jnp.sum(jnp.where(st[c].pop("onehot"), before, 0.0), axis=0, keepdims=True)
        total = jnp.sum(st[c].pop("ones").astype(F32), axis=1, keepdims=True)
        counts_ref[...] = counts_ref[...] + total
        srow = lax.broadcasted_iota(jnp.int32, (8, ck), 0)
        meta = jnp.where(srow == 0, st[c].pop("cls"), jnp.where(srow == 1, pos, 0.0))
        meta_ref[:, rows(c)] = meta.astype(jnp.int32)

    def run(stage, c):
        if 0 <= c < n_chunks:
            stage(c)

    stage_a(0)
    for s in range(n_chunks + 2):
        run(stage_f, s - 2)
        run(stage_c, s - 1)
        run(stage_d, s - 1)
        run(stage_b, s)
        run(stage_e, s - 1)
        run(stage_g, s - 2)
        run(stage_a, s + 1)


def _merge_call(oa, ob, x2, gout, wout, gffn, wr, br, tri):
    T = x2.shape[0]
    tm = MERGE_ROWS
    ck = tri.shape[0]
    const = lambda *shape: pl.BlockSpec(shape, lambda i: (0,) * len(shape))
    rows = lambda w: pl.BlockSpec((tm, w), lambda i: (i, 0))
    return pl.pallas_call(
        _merge_body,
        grid=(T // tm,),
        in_specs=[rows(512), rows(512), rows(D_MODEL), const(1, D_MODEL), const(D_MODEL, D_MODEL),
                  const(1, D_MODEL), const(LANES, D_MODEL), const(LANES, 1), const(ck, ck)],
        out_specs=[rows(D_MODEL), rows(ROW_WORDS), pl.BlockSpec((8, tm), lambda i: (0, i)),
                   const(LANES, LANES)],
        out_shape=[
            jax.ShapeDtypeStruct((T, D_MODEL), F32),
            jax.ShapeDtypeStruct((T, ROW_WORDS), jnp.uint32),
            jax.ShapeDtypeStruct((8, T), jnp.int32),
            jax.ShapeDtypeStruct((LANES, LANES), F32),
        ],
        compiler_params=_cparams(("arbitrary",)),
        name="merge",
    )(oa, ob, x2, gout, wout, gffn, wr, br, tri)


def _dispatch_body(pad_end_ref, nused_ref, dest_ref, row_ref, xs_ref, zero_buf, sem, zero_sem):
    n_tok = row_ref.shape[0]
    n_blocks = xs_ref.shape[0] // ROW_BLOCK

    @pl.when(pl.program_id(0) == 0)
    def _():
        zero_buf[...] = jnp.zeros_like(zero_buf)

        def block_copy(first_row):
            return pltpu.make_async_copy(
                zero_buf, xs_ref.at[pl.ds(pl.multiple_of(first_row, ROW_BLOCK), ROW_BLOCK)], zero_sem)

        def nonempty(c):
            return pad_end_ref[c] > jnp.where(c > 0, pad_end_ref[jnp.maximum(c - 1, 0)], 0)

        def start_class(c, carry):
            @pl.when(nonempty(c))
            def _():
                block_copy(pad_end_ref[c] - ROW_BLOCK).start()
            return carry

        def wait_class(c, carry):
            @pl.when(nonempty(c))
            def _():
                block_copy(pad_end_ref[c] - ROW_BLOCK).wait()
            return carry

        def start_tail(b, carry):
            block_copy(b * ROW_BLOCK).start()
            return carry

        def wait_tail(b, carry):
            block_copy(b * ROW_BLOCK).wait()
            return carry

        lax.fori_loop(0, N_CLASSES, start_class, 0)
        lax.fori_loop(nused_ref[0], n_blocks, start_tail, 0)
        lax.fori_loop(0, N_CLASSES, wait_class, 0)
        lax.fori_loop(nused_ref[0], n_blocks, wait_tail, 0)

    def start(t, c):
        pltpu.make_async_copy(row_ref.at[pl.ds(t, 1)], xs_ref.at[pl.ds(dest_ref[t], 1)], sem).start()
        return c

    lax.fori_loop(0, n_tok, start, 0, unroll=16)
    pltpu.make_async_copy(row_ref, xs_ref.at[pl.ds(0, n_tok)], sem).wait()


def _dispatch_call(pad_end, n_used, dest, rows, n_blocks, tok_per_step):
    T, width = rows.shape
    grid_spec = pltpu.PrefetchScalarGridSpec(
        num_scalar_prefetch=2,
        grid=(T // tok_per_step,),
        in_specs=[
            pl.BlockSpec((tok_per_step,), lambda i, pe, nu: (i,), memory_space=pltpu.SMEM),
            pl.BlockSpec((tok_per_step, width), lambda i, pe, nu: (i, 0)),
        ],
        out_specs=pl.BlockSpec(memory_space=pl.ANY),
        scratch_shapes=[pltpu.VMEM((ROW_BLOCK, width), rows.dtype),
                        pltpu.SemaphoreType.DMA(()),
                        pltpu.SemaphoreType.DMA(())],
    )
    return pl.pallas_call(
        _dispatch_body,
        grid_spec=grid_spec,
        out_shape=jax.ShapeDtypeStruct((n_blocks * ROW_BLOCK, width), rows.dtype),
        compiler_params=_cparams(("arbitrary",)),
        name="dispatch",
    )(pad_end, n_used, dest, rows)


def _expert_body(bea_ref, beb_ref, nused_ref, xs_ref, wga_ref, wua_ref, wda_ref,
                 wgb_ref, wub_ref, wdb_ref, y_ref):
    del bea_ref, beb_ref
    used = pl.program_id(0) < nused_ref[0]
    half = D_MODEL // 2

    @pl.when(used)
    def _():
        row = xs_ref[...]
        w = row[:, 0:half]
        lo = pltpu.bitcast(w << 16, F32)
        hi = pltpu.bitcast(w & jnp.uint32(0xFFFF0000), F32)
        xb = jnp.concatenate([lo, hi], axis=1).astype(BF16)
        wts = pltpu.bitcast(row[:, half:ROW_WORDS], F32)

        def hidden(wg_ref, wu_ref):
            a = jnp.dot(xb, wg_ref[...], preferred_element_type=F32)
            u = jnp.dot(xb, wu_ref[...], preferred_element_type=F32)
            return (a * (1.0 / (1.0 + jnp.exp(-a))) * u).astype(BF16)

        act_a = hidden(wga_ref, wua_ref)
        act_b = hidden(wgb_ref, wub_ref)
        y_ref[...] = (wts[:, 0:1] * jnp.dot(act_a, wda_ref[...], preferred_element_type=F32)
                      + wts[:, 1:2] * jnp.dot(act_b, wdb_ref[...], preferred_element_type=F32))

    @pl.when(jnp.logical_not(used))
    def _():
        y_ref[...] = jnp.zeros_like(y_ref)


def _expert_call(block_ea, block_eb, n_used, xs, wg, wu, wd):
    P = xs.shape[0]
    n_blocks = P // ROW_BLOCK

    def row_map(i, bea, beb, nu):
        return (jnp.minimum(i, nu[0] - 1), 0)

    def wa_map(i, bea, beb, nu):
        return (bea[jnp.minimum(i, nu[0] - 1)], 0, 0)

    def wb_map(i, bea, beb, nu):
        return (beb[jnp.minimum(i, nu[0] - 1)], 0, 0)

    up = (None, D_MODEL, EXPERT_HIDDEN)
    down = (None, EXPERT_HIDDEN, D_MODEL)
    grid_spec = pltpu.PrefetchScalarGridSpec(
        num_scalar_prefetch=3,
        grid=(n_blocks,),
        in_specs=[
            pl.BlockSpec((ROW_BLOCK, ROW_WORDS), row_map),
            pl.BlockSpec(up, wa_map), pl.BlockSpec(up, wa_map), pl.BlockSpec(down, wa_map),
            pl.BlockSpec(up, wb_map), pl.BlockSpec(up, wb_map), pl.BlockSpec(down, wb_map),
        ],
        out_specs=pl.BlockSpec((ROW_BLOCK, D_MODEL), lambda i, bea, beb, nu: (i, 0)),
    )
    return pl.pallas_call(
        _expert_body,
        grid_spec=grid_spec,
        out_shape=jax.ShapeDtypeStruct((P, D_MODEL), F32),
        compiler_params=_cparams(("arbitrary",)),
        name="experts",
    )(block_ea, block_eb, n_used, xs, wg, wu, wd, wg, wu, wd)


def _combine_body(dest_ref, dest_next_ref, x1_ref, ys_ref, o_ref, y_buf, sems):
    n_tok = x1_ref.shape[0]
    step = pl.program_id(0)
    slot = step % 2

    def gather(idx_ref, to_slot):
        def start(t, c):
            pltpu.make_async_copy(ys_ref.at[pl.ds(idx_ref[t], 1)],
                                  y_buf.at[to_slot, pl.ds(t, 1)], sems.at[to_slot]).start()
            return c
        lax.fori_loop(0, n_tok, start, 0, unroll=16)

    @pl.when(step == 0)
    def _():
        gather(dest_ref, 0)

    @pl.when(step + 1 < pl.num_programs(0))
    def _():
        gather(dest_next_ref, 1 - slot)

    pltpu.make_async_copy(ys_ref.at[pl.ds(0, n_tok)], y_buf.at[slot], sems.at[slot]).wait()
    o_ref[...] = x1_ref[...] + y_buf[slot]


def _combine_call(dest, x1, ys, tok_per_step):
    T = x1.shape[0]
    n_steps = T // tok_per_step
    return pl.pallas_call(
        _combine_body,
        grid=(n_steps,),
        in_specs=[
            pl.BlockSpec((tok_per_step,), lambda i: (i,), memory_space=pltpu.SMEM),
            pl.BlockSpec((tok_per_step,), lambda i: (jnp.minimum(i + 1, n_steps - 1),),
                         memory_space=pltpu.SMEM),
            pl.BlockSpec((tok_per_step, D_MODEL), lambda i: (i, 0)),
            pl.BlockSpec(memory_space=pl.ANY),
        ],
        out_specs=pl.BlockSpec((tok_per_step, D_MODEL), lambda i: (i, 0)),
        out_shape=jax.ShapeDtypeStruct((T, D_MODEL), F32),
        scratch_shapes=[pltpu.VMEM((2, tok_per_step, D_MODEL), F32),
                        pltpu.SemaphoreType.DMA((2,))],
        compiler_params=_cparams(("arbitrary",)),
        name="combine",
    )(dest, dest, x1, ys)


def _layer(x, attn_norm_g, w_in, mla_q_lat_g, w_q_up, mla_kv_lat_g, w_kv_up, mla_q_head_g,
           mla_k_head_g, moba_q_head_g, moba_k_head_g, out_norm_mla_g, out_norm_moba_g, w_out,
           ffn_norm_g, w_router_group, b_router_group, w_router_expert, b_router_expert,
           w_gate, w_up, w_down):
    B, S, D = x.shape
    T = B * S
    x2 = x.reshape(T, D)

    win, wq, wkv, gains = _prep_weights(w_in, w_q_up, w_kv_up, mla_q_head_g, mla_k_head_g,
                                        moba_q_head_g, moba_k_head_g)
    glat = jnp.concatenate([mla_q_lat_g, mla_kv_lat_g])[None, :]
    qa, ka, va, qb, kb, vb, kmean = _proj_call(
        x2, attn_norm_g[None, :], win, glat, wq, wkv, gains, _group_matrices(), _rope_tables(S), S)

    oa = _attnT_call(False, qa, ka, va, None, B, S, "mla")
    ob = _attnT_call(True, qb, kb, vb, kmean.reshape(T // MOBA_BLOCK, 512), B, S, "moba")

    wr = jnp.zeros((LANES, D), F32).at[:N_EXPERTS].set(w_router_expert.T)
    wr = wr.at[N_EXPERTS:N_EXPERTS + N_GROUPS].set(w_router_group.T).astype(BF16)
    br = jnp.zeros((LANES, 1), F32).at[:N_EXPERTS, 0].set(b_router_expert)
    br = br.at[N_EXPERTS:N_EXPERTS + N_GROUPS, 0].set(b_router_group)
    tri = jnp.asarray(np.triu(np.ones((MERGE_CHUNK, MERGE_CHUNK), np.float32), 1), BF16)
    gout = jnp.concatenate([out_norm_mla_g, out_norm_moba_g])[None, :]
    x1, rows, meta, counts = _merge_call(oa, ob, x2, gout, w_out.astype(BF16), ffn_norm_g[None, :],
                                         wr, br, tri)

    counts = counts[:N_CLASSES, 0].astype(jnp.int32)
    padded = (counts + ROW_BLOCK - 1) // ROW_BLOCK * ROW_BLOCK
    pad_end = jnp.cumsum(padded)
    pad_start = pad_end - padded
    n_blocks = (T + ROW_BLOCK - 1) // ROW_BLOCK + N_CLASSES
    n_used = (pad_end[-1] // ROW_BLOCK).astype(jnp.int32)
    blk = jnp.minimum(jnp.arange(n_blocks, dtype=jnp.int32), n_used - 1) * ROW_BLOCK
    block_class = jnp.minimum(jnp.sum(pad_end[None, :] <= blk[:, None], axis=1), N_CLASSES - 1)
    pairs = [(a, b) for a in range(EPG) for b in range(a + 1, EPG)]
    class_ea = np.array([g * EPG + a for g in range(N_GROUPS) for a, _ in pairs], np.int32)
    class_eb = np.array([g * EPG + b for g in range(N_GROUPS) for _, b in pairs], np.int32)
    block_ea = jnp.take(jnp.asarray(class_ea), block_class)
    block_eb = jnp.take(jnp.asarray(class_eb), block_class)
    is_class = meta[0][:, None] == jnp.arange(N_CLASSES, dtype=jnp.int32)[None, :]
    dest = (jnp.sum(jnp.where(is_class, pad_start[None, :], 0), axis=1) + meta[1]).astype(jnp.int32)

    xs = _dispatch_call(pad_end.astype(jnp.int32), n_used[None], dest, rows, n_blocks, 1024)
    ys = _expert_call(block_ea, block_eb, n_used[None], xs,
                      w_gate.astype(BF16), w_up.astype(BF16), w_down.astype(BF16))
    out = _combine_call(dest, x1, ys, 512)
    return out.reshape(B, S, D)


def kernel(x, attn_norm_g, w_in, mla_q_lat_g, w_q_up, mla_kv_lat_g, w_kv_up, mla_q_head_g, mla_k_head_g, moba_q_head_g, moba_k_head_g, out_norm_mla_g, out_norm_moba_g, w_out, ffn_norm_g, w_router_group, b_router_group, w_router_expert, b_router_expert, w_gate, w_up, w_down):
    return _layer(x, attn_norm_g[0], w_in[0], mla_q_lat_g[0], w_q_up[0], mla_kv_lat_g[0], w_kv_up[0],
                  mla_q_head_g[0], mla_k_head_g[0], moba_q_head_g[0], moba_k_head_g[0],
                  out_norm_mla_g[0], out_norm_moba_g[0], w_out[0], ffn_norm_g[0],
                  w_router_group[0], b_router_group[0], w_router_expert[0], b_router_expert[0],
                  w_gate[0], w_up[0], w_down[0])
```

```python
import functools

import jax
import jax.numpy as jnp
import numpy as np
from jax import lax
from jax.experimental import pallas as pl
from jax.experimental.pallas import tpu as pltpu

F32 = jnp.float32
BF16 = jnp.bfloat16

D_MODEL = 1024
MLA_HEADS = 8
MLA_NOPE = 64
MLA_ROPE = 32
MLA_QK = MLA_NOPE + MLA_ROPE
MLA_V = 64
MLA_Q_LORA = 256
MLA_KV_LORA = 128
MOBA_HEADS = 8
MOBA_D = 64
MOBA_BLOCK = 256
MOBA_TOPK = 3
ROPE_THETA = 10000.0
EPS = 1e-6
N_GROUPS = 4
EPG = 8
N_EXPERTS = N_GROUPS * EPG
EXPERT_HIDDEN = 256
ROW_BLOCK = 256
PAIRS_PER_GROUP = EPG * (EPG - 1) // 2
N_CLASSES = N_GROUPS * PAIRS_PER_GROUP
ROW_WORDS = D_MODEL // 2 + 128

LANES = 128
HEAD_SLOT = LANES
PAIR = LANES
PROJ_WIDTH = 2048
MERGE_ROWS = 1024
MERGE_CHUNK = 256
PROJ_CHUNK = 256
PROJ_ROWS = 1024
NEG_INF = float("-inf")

VMEM_LIMIT = 48 * 1024 * 1024


def _cparams(sem):
    return pltpu.CompilerParams(dimension_semantics=sem, vmem_limit_bytes=VMEM_LIMIT)


def _mla_slot_index():
    idx = -np.ones((HEAD_SLOT,), np.int64)
    idx[0:32] = np.arange(0, 32)
    idx[32:48] = MLA_NOPE + np.arange(0, 16)
    idx[64:96] = np.arange(32, 64)
    idx[96:112] = MLA_NOPE + np.arange(16, 32)
    return idx


def _moba_pair_index():
    head = np.concatenate([np.zeros(32), np.ones(32), np.zeros(32), np.ones(32)]).astype(np.int64)
    feat = np.concatenate([np.arange(32), np.arange(32), 32 + np.arange(32), 32 + np.arange(32)])
    return head, feat


def _gather_cols(w, idx):
    safe = np.where(idx < 0, 0, idx)
    out = jnp.take(w, jnp.asarray(safe), axis=-1)
    return jnp.where(jnp.asarray(idx < 0), 0.0, out)


def _prep_weights(w_in, w_q_up, w_kv_up, q_head_g, k_head_g, mq_g, mk_g):
    slot = _mla_slot_index()
    head, feat = _moba_pair_index()

    kpe_idx = np.where(slot >= MLA_NOPE, slot - MLA_NOPE, -1)
    off_kpe = MLA_Q_LORA + MLA_KV_LORA
    off_mq = off_kpe + MLA_ROPE
    off_mk = off_mq + MOBA_HEADS * MOBA_D
    off_mv = off_mk + MOBA_HEADS * MOBA_D
    moba_idx = np.concatenate([(2 * p + head) * MOBA_D + feat for p in range(MOBA_HEADS // 2)])
    cols = np.concatenate([
        np.arange(0, off_kpe),
        np.where(kpe_idx < 0, -1, off_kpe + kpe_idx),
        off_mq + moba_idx,
        off_mk + moba_idx,
        off_mv + np.arange(MOBA_HEADS * MOBA_D),
    ])
    win = _gather_cols(w_in, cols).astype(BF16)

    q_idx = np.concatenate([np.where(slot < 0, -1, h * MLA_QK + slot) for h in range(MLA_HEADS)])
    wq = _gather_cols(w_q_up, q_idx).astype(BF16)

    nope_slot = np.where((slot >= 0) & (slot < MLA_NOPE), slot, -1)
    kn_idx = np.concatenate([np.where(nope_slot < 0, -1, h * (MLA_NOPE + MLA_V) + nope_slot)
                             for h in range(MLA_HEADS)])
    v_idx = np.concatenate([h * (MLA_NOPE + MLA_V) + MLA_NOPE + np.arange(MLA_V) for h in range(MLA_HEADS)])
    wkv = _gather_cols(w_kv_up, np.concatenate([kn_idx, v_idx])).astype(BF16)

    gq = _gather_cols(q_head_g, slot) * (MLA_QK ** -0.5 * LOG2E)
    gk = _gather_cols(k_head_g, slot)
    gmq = jnp.take(mq_g, jnp.asarray(feat)) * (MOBA_D ** -0.5 * LOG2E)
    gmk = jnp.take(mk_g, jnp.asarray(feat))
    gains = jnp.stack([gq, gk, gmq, gmk]).astype(F32)
    return win, wq, wkv, gains


def _rope_tables(seq):
    def tab(dim):
        inv = ROPE_THETA ** (-(jnp.arange(0, dim, 2, dtype=F32) / dim))
        ang = jnp.arange(seq, dtype=F32)[:, None] * inv[None, :]
        return jnp.cos(ang), jnp.sin(ang)

    cr, sr = tab(MLA_ROPE)
    cf, sf = tab(MOBA_D)
    one = jnp.ones((seq, 16), F32)
    zero = jnp.zeros((seq, 16), F32)
    cos_a = jnp.concatenate([one, one, cr, one, one, one, cr, one], axis=1)
    sin_a = jnp.concatenate([zero, zero, -sr, zero, zero, zero, sr, zero], axis=1)
    cos_b = jnp.concatenate([cf, cf, cf, cf], axis=1)
    sin_b = jnp.concatenate([-sf, -sf, sf, sf], axis=1)
    return jnp.stack([cos_a, sin_a, cos_b, sin_b])


def _group_matrices():
    lane = np.arange(256)
    g_mla = (lane[:, None] // HEAD_SLOT == lane[None, :] // HEAD_SLOT)
    head, _ = _moba_pair_index()
    hid = np.concatenate([head, 2 + head])
    g_moba = hid[:, None] == hid[None, :]
    return jnp.asarray(np.stack([g_mla, g_moba]), BF16)


def _rms(x, width):
    return lax.rsqrt(jnp.sum(x * x, axis=-1, keepdims=True) * (1.0 / width) + EPS)


def _rope(t, cos, sin):
    outs = []
    for c in range(0, t.shape[1], LANES):
        xc = t[:, c:c + LANES]
        outs.append(xc * cos + pltpu.roll(xc, 64, 1) * sin)
    return jnp.concatenate(outs, axis=1)


def _proj_body(x_ref, gx_ref, win_ref, glat_ref, wq_ref, wkv_ref, gains_ref, gmat_ref, rope_ref,
               qa_ref, ka_ref, va_ref, qb_ref, kb_ref, vb_ref, kmean_ref):
    n_chunks = x_ref.shape[0] // PROJ_CHUNK
    gains = gains_ref[...]
    g_mla = gmat_ref[0]
    g_moba = gmat_ref[1]

    def tile_gain(row, n):
        return jnp.concatenate([gains[row:row + 1, :]] * n, axis=1)

    def rows(c):
        return pl.ds(c * PROJ_CHUNK, PROJ_CHUNK)

    def group_sums(sq, gmat):
        return jnp.concatenate([jnp.dot(sq[:, c:c + 256], gmat, preferred_element_type=F32)
                                for c in range(0, sq.shape[1], 256)], axis=1)

    def finish(t, ssum, dim, gain, cos, sin):
        return _rope(t * lax.rsqrt(ssum * (1.0 / dim) + EPS) * gain, cos, sin)

    st = [dict() for _ in range(n_chunks)]

    def stage_a(c):
        x = x_ref[rows(c), :]
        st[c]["h"] = (x * _rms(x, D_MODEL) * gx_ref[...]).astype(BF16)

    def stage_b(c):
        st[c]["proj"] = jnp.dot(st[c].pop("h"), win_ref[...], preferred_element_type=F32)

    def stage_c(c):
        proj = st[c].pop("proj")
        q_lat = proj[:, 0:256]
        kv_lat = proj[:, 256:384]
        st[c]["ql"] = (q_lat * _rms(q_lat, MLA_Q_LORA) * glat_ref[:, 0:256]).astype(BF16)
        st[c]["kvl"] = (kv_lat * _rms(kv_lat, MLA_KV_LORA) * glat_ref[:, 256:384]).astype(BF16)
        st[c]["kpe"] = proj[:, 384:512]
        st[c]["mq"] = proj[:, 512:1024]
        st[c]["mk"] = proj[:, 1024:1536]
        st[c]["mq2"] = (st[c]["mq"] * st[c]["mq"]).astype(BF16)
        st[c]["mk2"] = (st[c]["mk"] * st[c]["mk"]).astype(BF16)
        vb_ref[rows(c), :] = proj[:, 1536:2048].astype(BF16)

    def stage_d(c):
        st[c]["qa"] = jnp.dot(st[c].pop("ql"), wq_ref[...], preferred_element_type=F32)
        st[c]["kv"] = jnp.dot(st[c].pop("kvl"), wkv_ref[...], preferred_element_type=F32)
        st[c]["mq_ss"] = group_sums(st[c].pop("mq2"), g_moba)
        st[c]["mk_ss"] = group_sums(st[c].pop("mk2"), g_moba)

    def stage_e(c):
        cos_b, sin_b = rope_ref[2, rows(c), :], rope_ref[3, rows(c), :]
        qb_ref[rows(c), :] = finish(st[c].pop("mq"), st[c].pop("mq_ss"), MOBA_D,
                                    tile_gain(2, 4), cos_b, sin_b).astype(BF16)
        kb = finish(st[c].pop("mk"), st[c].pop("mk_ss"), MOBA_D, tile_gain(3, 4), cos_b, sin_b)
        kb_ref[rows(c), :] = kb.astype(BF16)
        part = jnp.sum(kb, axis=0, keepdims=True) * (1.0 / MOBA_BLOCK)
        per_block = MOBA_BLOCK // PROJ_CHUNK
        if c % per_block == 0:
            kmean_ref[c // per_block] = part
        else:
            kmean_ref[c // per_block] = kmean_ref[c // per_block] + part
        kv = st[c].pop("kv")
        va_ref[rows(c), :] = kv[:, 1024:1536].astype(BF16)
        st[c]["ka"] = kv[:, 0:1024] + jnp.concatenate([st[c].pop("kpe")] * MLA_HEADS, axis=1)
        st[c]["qa2"] = (st[c]["qa"] * st[c]["qa"]).astype(BF16)
        st[c]["ka2"] = (st[c]["ka"] * st[c]["ka"]).astype(BF16)

    def stage_f(c):
        st[c]["qa_ss"] = group_sums(st[c].pop("qa2"), g_mla)
        st[c]["ka_ss"] = group_sums(st[c].pop("ka2"), g_mla)

    def stage_g(c):
        cos_a, sin_a = rope_ref[0, rows(c), :], rope_ref[1, rows(c), :]
        qa_ref[rows(c), :] = finish(st[c].pop("qa"), st[c].pop("qa_ss"), MLA_QK,
                                    tile_gain(0, 8), cos_a, sin_a).astype(BF16)
        ka_ref[rows(c), :] = finish(st[c].pop("ka"), st[c].pop("ka_ss"), MLA_QK,
                                    tile_gain(1, 8), cos_a, sin_a).astype(BF16)

    def run(stage, c):
        if 0 <= c < n_chunks:
            stage(c)

    stage_a(0)
    for s in range(n_chunks + 2):
        run(stage_f, s - 2)
        run(stage_c, s - 1)
        run(stage_d, s - 1)
        run(stage_b, s)
        run(stage_e, s - 1)
        run(stage_g, s - 2)
        run(stage_a, s + 1)


def _proj_call(x2, gx, win, glat, wq, wkv, gains, gmat, rope, seq):
    T = x2.shape[0]
    tm = PROJ_ROWS
    nt = T // tm
    spb = seq // tm
    bpt = tm // MOBA_BLOCK
    const = lambda *shape: pl.BlockSpec(shape, lambda i: (0,) * len(shape))
    rows = lambda w: pl.BlockSpec((tm, w), lambda i: (i, 0))
    return pl.pallas_call(
        _proj_body,
        grid=(nt,),
        in_specs=[
            rows(D_MODEL),
            const(1, D_MODEL),
            const(D_MODEL, PROJ_WIDTH),
            const(1, 384),
            const(MLA_Q_LORA, 1024),
            const(MLA_KV_LORA, 1536),
            const(4, LANES),
            const(2, 256, 256),
            pl.BlockSpec((4, tm, LANES), lambda i: (0, i % spb, 0)),
        ],
        out_specs=[rows(1024), rows(1024), rows(512), rows(512), rows(512), rows(512),
                   pl.BlockSpec((bpt, 1, 512), lambda i: (i, 0, 0))],
        out_shape=[
            jax.ShapeDtypeStruct((T, 1024), BF16),
            jax.ShapeDtypeStruct((T, 1024), BF16),
            jax.ShapeDtypeStruct((T, 512), BF16),
            jax.ShapeDtypeStruct((T, 512), BF16),
            jax.ShapeDtypeStruct((T, 512), BF16),
            jax.ShapeDtypeStruct((T, 512), BF16),
            jax.ShapeDtypeStruct((nt * bpt, 1, 512), F32),
        ],
        compiler_params=_cparams(("arbitrary",)),
        name="proj",
    )(x2, gx, win, glat, wq, wkv, gains, gmat, rope)


def _nt_dot(a, b):
    return lax.dot_general(a, b, (((1,), (1,)), ((), ())), preferred_element_type=F32)


ATT_BLOCK = MOBA_BLOCK
ATT_CHAINS = 4
MLA_PAIRS_PER_STEP = 2
MOBA_PAIRS_PER_STEP = 4
ATT_AHEAD = 2
M_INIT = -1e30
ONES_ROWS = 16
LOG2E = 1.4426950408889634


def _attnT_body(moba, q_ref, k_ref, v_ref, *rest):
    tb = ATT_BLOCK
    nq = q_ref.shape[0] // tb
    n_pairs = v_ref.shape[1] // PAIR
    qw = q_ref.shape[1] // n_pairs
    order = [(r, t) if t <= r else (nq - 1 - r, t - (r + 1))
             for t in range(nq + 1) for r in range(ATT_CHAINS)]
    pair_fns = [_attn_pair(moba, pp, qw, q_ref, k_ref, v_ref, rest) for pp in range(n_pairs)]
    for prep, _, _, _ in pair_fns:
        for i in range(nq):
            prep(i)
    stream = [(pp, i, j) for pp in range(n_pairs) for i, j in order]
    pending = [pair_fns[pp][1](i, j) for pp, i, j in stream[:ATT_AHEAD]]
    for n, (pp, i, j) in enumerate(stream):
        if n + ATT_AHEAD < len(stream):
            nxt = stream[n + ATT_AHEAD]
            pending.append(pair_fns[nxt[0]][1](nxt[1], nxt[2]))
        pair_fns[pp][2](i, j, pending.pop(0))
        if i == j:
            pair_fns[pp][3](i)


def _attn_pair(moba, pp, qw, q_all, k_all, v_all, rest):
    if moba:
        km_all, o_all, qt_all, vt_all, m_all, acc_all, sel_all = rest
        km_ref = km_all.at[:, pl.ds(pp * PAIR, PAIR)]
        sel_ref = sel_all.at[pp]
    else:
        o_all, qt_all, vt_all, m_all, acc_all = rest
    q_ref = q_all.at[:, pl.ds(pp * qw, qw)]
    k_ref = k_all.at[:, pl.ds(pp * qw, qw)]
    v_ref = v_all.at[:, pl.ds(pp * PAIR, PAIR)]
    o_ref = o_all.at[:, pl.ds(pp * PAIR, PAIR)]
    qt_ref, vt_ref, m_ref, acc_ref = qt_all.at[pp], vt_all.at[pp], m_all.at[pp], acc_all.at[pp]
    tb = ATT_BLOCK
    cols = 2 * tb
    nq = q_ref.shape[0] // tb
    lane_q = lax.broadcasted_iota(jnp.int32, (tb, qw), 1)

    if moba:
        km = km_ref[...]
        km_hi = km.astype(BF16)
        km_lo = (km - km_hi.astype(F32)).astype(BF16)
        zpad = jnp.zeros((16 - nq, PAIR), BF16)
        km_hi = jnp.concatenate([km_hi, zpad], axis=0)
        km_lo = jnp.concatenate([km_lo, zpad], axis=0)
        blk = lax.broadcasted_iota(jnp.int32, (16, cols), 0)

    def prep(i):
        start = i * tb
        q = q_ref[pl.ds(start, tb), :]
        if moba:
            head_of_lane = (lane_q // 32) % 2
            zero = jnp.zeros_like(q)
            per_head = [jnp.where(head_of_lane == h, q, zero) for h in range(2)]
        else:
            per_head = [q[:, h * HEAD_SLOT:(h + 1) * HEAD_SLOT] for h in range(2)]
        qts = [qh.astype(F32).T.astype(BF16) for qh in per_head]
        for h in range(2):
            qt_ref[i, h] = qts[h]
        vt = v_ref[pl.ds(start, tb), :].astype(F32).T.astype(BF16)
        vt_ref[i] = jnp.concatenate([vt, jnp.ones((ONES_ROWS, tb), BF16)], axis=0)
        m_ref[i] = jnp.full((1, cols), M_INIT, F32)
        acc_ref[i] = jnp.zeros((PAIR + ONES_ROWS, cols), F32)
        if moba:
            qt = jnp.concatenate(qts, axis=1)
            gate = (jnp.dot(km_hi, qt, preferred_element_type=F32)
                    + jnp.dot(km_lo, qt, preferred_element_type=F32))
            cnt = jnp.zeros((16, cols), F32)
            for b in range(i):
                gb = gate[b:b + 1, :]
                beats = (gb > gate) | ((gb == gate) & (blk > b))
                cnt = cnt + jnp.where(beats, 1.0, 0.0)
            picked = (blk < i) & (cnt < float(MOBA_TOPK))
            sel_ref[i] = jnp.where(picked, 1.0, 0.0)

    key = lax.broadcasted_iota(jnp.int32, (tb, cols), 0)
    qry = lax.broadcasted_iota(jnp.int32, (tb, cols), 1) % tb
    causal = key <= qry

    def scores(i, j):
        halves = []
        for h in range(2):
            lanes = slice(0, PAIR) if moba else slice(h * HEAD_SLOT, (h + 1) * HEAD_SLOT)
            halves.append(jnp.dot(k_ref[pl.ds(j * tb, tb), lanes], qt_ref[i, h],
                                  preferred_element_type=F32))
        return jnp.concatenate(halves, axis=1)

    def item(i, j, s):
        m_prev = m_ref[i]
        if i == j:
            s = jnp.where(causal, s, NEG_INF)
            m_new = jnp.maximum(m_prev, jnp.max(s, axis=0, keepdims=True))
            m_sub = m_new
        else:
            m_new = jnp.maximum(m_prev, jnp.max(s, axis=0, keepdims=True))
            m_sub = m_new
            if moba:
                chosen = sel_ref[i, pl.ds(j, 1), :] > 0.5
                m_new = jnp.where(chosen, m_new, m_prev)
                m_sub = jnp.where(chosen, m_new, -M_INIT)
        alpha = jnp.exp2(m_prev - m_new)
        p = jnp.exp2(s - m_sub).astype(BF16)
        m_ref[i] = m_new
        acc_ref[i] = alpha * acc_ref[i] + jnp.dot(vt_ref[j], p, preferred_element_type=F32)

    def finish(i):
        acc = acc_ref[i]
        o = acc[0:PAIR, :] / acc[PAIR:PAIR + 1, :]
        vrow = lax.broadcasted_iota(jnp.int32, (PAIR, tb), 0)
        ot = jnp.where(vrow < PAIR // 2, o[:, 0:tb], o[:, tb:cols])
        o_ref[pl.ds(i * tb, tb), :] = ot.T.astype(o_ref.dtype)

    return prep, scores, item, finish


def _attnT_call(moba, q, k, v, kmean, batch, seq, name):
    T, width = q.shape
    pairs = MLA_HEADS // 2
    qw = width // pairs
    nq = seq // ATT_BLOCK
    assert nq == 2 * ATT_CHAINS, "chain pairing (r, nq-1-r) needs nq == 2 * ATT_CHAINS"
    cols = 2 * ATT_BLOCK
    npp = MOBA_PAIRS_PER_STEP if moba else MLA_PAIRS_PER_STEP
    seq_block = lambda w: pl.BlockSpec((seq, npp * w), lambda b, p: (b, p))
    in_specs = [seq_block(qw), seq_block(qw), seq_block(PAIR)]
    scratch = [pltpu.VMEM((npp, nq, 2, PAIR, ATT_BLOCK), BF16),
               pltpu.VMEM((npp, nq, PAIR + ONES_ROWS, ATT_BLOCK), BF16),
               pltpu.VMEM((npp, nq, 1, cols), F32),
               pltpu.VMEM((npp, nq, PAIR + ONES_ROWS, cols), F32)]
    args = [q, k, v]
    if moba:
        in_specs.append(pl.BlockSpec((nq, npp * PAIR), lambda b, p: (b, p)))
        scratch.append(pltpu.VMEM((npp, nq, 16, cols), F32))
        args.append(kmean)
    return pl.pallas_call(
        functools.partial(_attnT_body, moba),
        grid=(batch, pairs // npp),
        in_specs=in_specs,
        out_specs=seq_block(PAIR),
        out_shape=jax.ShapeDtypeStruct((T, pairs * PAIR), BF16),
        scratch_shapes=scratch,
        compiler_params=_cparams(("arbitrary", "arbitrary")),
        name=name,
    )(*args)


def _merge_body(oa_ref, ob_ref, x_ref, gout_ref, wout_ref, gffn_ref, wr_ref, br_ref, tri_ref,
                x1_ref, row_ref, meta_ref, counts_ref):
    ck = tri_ref.shape[0]
    n_chunks = x_ref.shape[0] // ck

    @pl.when(pl.program_id(0) == 0)
    def _():
        counts_ref[...] = jnp.zeros_like(counts_ref)

    ridx = lax.broadcasted_iota(jnp.int32, (LANES, ck), 0)
    ridx_f = ridx.astype(F32)
    st = [dict() for _ in range(n_chunks)]

    def rows(c):
        return pl.ds(c * ck, ck)

    def first_max(vals):
        mx = jnp.max(vals, axis=0, keepdims=True)
        idx = jnp.min(jnp.where(vals == mx, ridx_f, float(LANES)), axis=0, keepdims=True)
        return mx, idx

    def stage_a(c):
        oa = oa_ref[rows(c), :].astype(F32)
        ob = ob_ref[rows(c), :].astype(F32)
        na = oa * _rms(oa, 512) * gout_ref[:, 0:512]
        nb = ob * _rms(ob, 512) * gout_ref[:, 512:1024]
        st[c]["mixed"] = jnp.concatenate([na, nb], axis=1).astype(BF16)

    def stage_b(c):
        st[c]["proj"] = jnp.dot(st[c].pop("mixed"), wout_ref[...], preferred_element_type=F32)

    def stage_c(c):
        x1 = x_ref[rows(c), :] + st[c].pop("proj")
        x1_ref[rows(c), :] = x1
        h2 = x1 * _rms(x1, D_MODEL) * gffn_ref[...]
        st[c]["h2b"] = h2.astype(BF16)
        half = D_MODEL // 2
        lo = pltpu.bitcast(h2[:, 0:half].astype(BF16).astype(F32), jnp.uint32)
        hi = pltpu.bitcast(h2[:, half:D_MODEL].astype(BF16).astype(F32), jnp.uint32)
        row_ref[rows(c), 0:half] = (hi & jnp.uint32(0xFFFF0000)) | (lo >> 16)

    def stage_d(c):
        st[c]["logits"] = _nt_dot(wr_ref[...], st[c].pop("h2b")) + br_ref[...]

    def stage_e(c):
        logits = st[c].pop("logits")
        is_group = (ridx >= N_EXPERTS) & (ridx < N_EXPERTS + N_GROUPS)
        gl = jnp.where(is_group, logits, NEG_INF)
        gmax, gidx = first_max(gl)
        g_w = 1.0 / jnp.sum(jnp.exp(gl - gmax), axis=0, keepdims=True)
        g_sel = gidx - float(N_EXPERTS)
        in_group = (ridx < N_EXPERTS) & ((ridx // EPG).astype(F32) == g_sel)
        el = jnp.where(in_group, logits, NEG_INF)
        v1, i1 = first_max(el)
        v2, i2 = first_max(jnp.where(ridx_f == i1, NEG_INF, el))
        t = jnp.exp(v2 - v1)
        w1 = g_w * (1.0 / (1.0 + t))
        w2 = g_w * (t / (1.0 + t))
        swap = i2 < i1
        la = jnp.where(swap, i2, i1) - EPG * g_sel
        lb = jnp.where(swap, i1, i2) - EPG * g_sel
        cls = PAIRS_PER_GROUP * g_sel + la * (2 * EPG - 1 - la) * 0.5 + (lb - la - 1.0)
        st[c]["cls"] = cls
        st[c]["onehot"] = ridx_f == cls
        wa = jnp.where(swap, w2, w1)
        wb = jnp.where(swap, w1, w2)
        wt = jnp.where(ridx == 0, wa, jnp.where(ridx == 1, wb, 0.0))
        row_ref[rows(c), D_MODEL // 2:ROW_WORDS] = pltpu.bitcast(wt.T, jnp.uint32)

    def stage_f(c):
        ones = jnp.where(st[c]["onehot"], 1.0, 0.0).astype(BF16)
        st[c]["ones"] = ones
        st[c]["before"] = jnp.dot(ones, tri_ref[...], preferred_element_type=F32)

    def stage_g(c):
        before = st[c].pop("before") + counts_ref[...][:, 0:1]
        pos = jnp.sum(jnp.where(st[c].pop("onehot"), before, 0.0), axis=0, keepdims=True)
        total = jnp.sum(st[c].pop("ones").astype(F32), axis=1, keepdims=True)
        counts_ref[...] = counts_ref[...] + total
        srow = lax.broadcasted_iota(jnp.int32, (8, ck), 0)
        meta = jnp.where(srow == 0, st[c].pop("cls"), jnp.where(srow == 1, pos, 0.0))
        meta_ref[:, rows(c)] = meta.astype(jnp.int32)

    def run(stage, c):
        if 0 <= c < n_chunks:
            stage(c)

    stage_a(0)
    for s in range(n_chunks + 2):
        run(stage_f, s - 2)
        run(stage_c, s - 1)
        run(stage_d, s - 1)
        run(stage_b, s)
        run(stage_e, s - 1)
        run(stage_g, s - 2)
        run(stage_a, s + 1)


def _merge_call(oa, ob, x2, gout, wout, gffn, wr, br, tri):
    T = x2.shape[0]
    tm = MERGE_ROWS
    ck = tri.shape[0]
    const = lambda *shape: pl.BlockSpec(shape, lambda i: (0,) * len(shape))
    rows = lambda w: pl.BlockSpec((tm, w), lambda i: (i, 0))
    return pl.pallas_call(
        _merge_body,
        grid=(T // tm,),
        in_specs=[rows(512), rows(512), rows(D_MODEL), const(1, D_MODEL), const(D_MODEL, D_MODEL),
                  const(1, D_MODEL), const(LANES, D_MODEL), const(LANES, 1), const(ck, ck)],
        out_specs=[rows(D_MODEL), rows(ROW_WORDS), pl.BlockSpec((8, tm), lambda i: (0, i)),
                   const(LANES, LANES)],
        out_shape=[
            jax.ShapeDtypeStruct((T, D_MODEL), F32),
            jax.ShapeDtypeStruct((T, ROW_WORDS), jnp.uint32),
            jax.ShapeDtypeStruct((8, T), jnp.int32),
            jax.ShapeDtypeStruct((LANES, LANES), F32),
        ],
        compiler_params=_cparams(("arbitrary",)),
        name="merge",
    )(oa, ob, x2, gout, wout, gffn, wr, br, tri)


def _dispatch_body(pad_end_ref, nused_ref, dest_ref, row_ref, xs_ref, zero_buf, sem, zero_sem):
    n_tok = row_ref.shape[0]
    n_blocks = xs_ref.shape[0] // ROW_BLOCK

    @pl.when(pl.program_id(0) == 0)
    def _():
        zero_buf[...] = jnp.zeros_like(zero_buf)

        def block_copy(first_row):
            return pltpu.make_async_copy(
                zero_buf, xs_ref.at[pl.ds(pl.multiple_of(first_row, ROW_BLOCK), ROW_BLOCK)], zero_sem)

        def nonempty(c):
            return pad_end_ref[c] > jnp.where(c > 0, pad_end_ref[jnp.maximum(c - 1, 0)], 0)

        def start_class(c, carry):
            @pl.when(nonempty(c))
            def _():
                block_copy(pad_end_ref[c] - ROW_BLOCK).start()
            return carry

        def wait_class(c, carry):
            @pl.when(nonempty(c))
            def _():
                block_copy(pad_end_ref[c] - ROW_BLOCK).wait()
            return carry

        def start_tail(b, carry):
            block_copy(b * ROW_BLOCK).start()
            return carry

        def wait_tail(b, carry):
            block_copy(b * ROW_BLOCK).wait()
            return carry

        lax.fori_loop(0, N_CLASSES, start_class, 0)
        lax.fori_loop(nused_ref[0], n_blocks, start_tail, 0)
        lax.fori_loop(0, N_CLASSES, wait_class, 0)
        lax.fori_loop(nused_ref[0], n_blocks, wait_tail, 0)

    def start(t, c):
        pltpu.make_async_copy(row_ref.at[pl.ds(t, 1)], xs_ref.at[pl.ds(dest_ref[t], 1)], sem).start()
        return c

    lax.fori_loop(0, n_tok, start, 0, unroll=16)
    pltpu.make_async_copy(row_ref, xs_ref.at[pl.ds(0, n_tok)], sem).wait()


def _dispatch_call(pad_end, n_used, dest, rows, n_blocks, tok_per_step):
    T, width = rows.shape
    grid_spec = pltpu.PrefetchScalarGridSpec(
        num_scalar_prefetch=2,
        grid=(T // tok_per_step,),
        in_specs=[
            pl.BlockSpec((tok_per_step,), lambda i, pe, nu: (i,), memory_space=pltpu.SMEM),
            pl.BlockSpec((tok_per_step, width), lambda i, pe, nu: (i, 0)),
        ],
        out_specs=pl.BlockSpec(memory_space=pl.ANY),
        scratch_shapes=[pltpu.VMEM((ROW_BLOCK, width), rows.dtype),
                        pltpu.SemaphoreType.DMA(()),
                        pltpu.SemaphoreType.DMA(())],
    )
    return pl.pallas_call(
        _dispatch_body,
        grid_spec=grid_spec,
        out_shape=jax.ShapeDtypeStruct((n_blocks * ROW_BLOCK, width), rows.dtype),
        compiler_params=_cparams(("arbitrary",)),
        name="dispatch",
    )(pad_end, n_used, dest, rows)


EXPERT_BLOCKS_PER_STEP = 2


def _expert_body(bea_ref, beb_ref, nused_ref, *refs):
    del bea_ref, beb_ref
    nb = EXPERT_BLOCKS_PER_STEP
    y_ref = refs[-1]
    first = pl.program_id(0) * nb
    half = D_MODEL // 2

    @pl.when(first < nused_ref[0])
    def _():
        st = [dict() for _ in range(nb)]

        def unpack(k):
            row = refs[7 * k][...]
            w = row[:, 0:half]
            lo = pltpu.bitcast(w << 16, F32)
            hi = pltpu.bitcast(w & jnp.uint32(0xFFFF0000), F32)
            st[k]["x"] = jnp.concatenate([lo, hi], axis=1).astype(BF16)
            st[k]["w"] = pltpu.bitcast(row[:, half:ROW_WORDS], F32)

        def up(k):
            x = st[k].pop("x")
            st[k]["au"] = [(jnp.dot(x, refs[7 * k + 1 + 3 * e][...], preferred_element_type=F32),
                            jnp.dot(x, refs[7 * k + 2 + 3 * e][...], preferred_element_type=F32))
                           for e in range(2)]

        def act(k):
            st[k]["act"] = [(a * (1.0 / (1.0 + jnp.exp(-a))) * u).astype(BF16)
                            for a, u in st[k].pop("au")]

        def down(k):
            st[k]["y"] = [jnp.dot(h, refs[7 * k + 3 + 3 * e][...], preferred_element_type=F32)
                          for e, h in enumerate(st[k].pop("act"))]

        def finish(k):
            ya, yb = st[k].pop("y")
            w = st[k].pop("w")
            y = w[:, 0:1] * ya + w[:, 1:2] * yb
            if k > 0:
                y = jnp.where(first + k < nused_ref[0], y, 0.0)
            y_ref[pl.ds(k * ROW_BLOCK, ROW_BLOCK), :] = y

        for k in range(nb):
            unpack(k)
        up(0)
        for k in range(nb):
            if k + 1 < nb:
                up(k + 1)
            act(k)
            down(k)
            if k > 0:
                finish(k - 1)
        finish(nb - 1)

    @pl.when(first >= nused_ref[0])
    def _():
        y_ref[...] = jnp.zeros_like(y_ref)


def _expert_call(block_ea, block_eb, n_used, xs, wg, wu, wd):
    P = xs.shape[0]
    nb = EXPERT_BLOCKS_PER_STEP
    n_steps = P // (ROW_BLOCK * nb)
    up = (None, D_MODEL, EXPERT_HIDDEN)
    down = (None, EXPERT_HIDDEN, D_MODEL)

    in_specs, args = [], []
    for k in range(nb):
        def blk(i, nu, k=k):
            return jnp.minimum(i * nb + k, nu[0] - 1)

        def row_map(i, bea, beb, nu, blk=blk):
            return (blk(i, nu), 0)

        def wa_map(i, bea, beb, nu, blk=blk):
            return (bea[blk(i, nu)], 0, 0)

        def wb_map(i, bea, beb, nu, blk=blk):
            return (beb[blk(i, nu)], 0, 0)

        in_specs += [pl.BlockSpec((ROW_BLOCK, ROW_WORDS), row_map),
                     pl.BlockSpec(up, wa_map), pl.BlockSpec(up, wa_map), pl.BlockSpec(down, wa_map),
                     pl.BlockSpec(up, wb_map), pl.BlockSpec(up, wb_map), pl.BlockSpec(down, wb_map)]
        args += [xs, wg, wu, wd, wg, wu, wd]

    grid_spec = pltpu.PrefetchScalarGridSpec(
        num_scalar_prefetch=3,
        grid=(n_steps,),
        in_specs=in_specs,
        out_specs=pl.BlockSpec((ROW_BLOCK * nb, D_MODEL), lambda i, bea, beb, nu: (i, 0)),
    )
    return pl.pallas_call(
        _expert_body,
        grid_spec=grid_spec,
        out_shape=jax.ShapeDtypeStruct((P, D_MODEL), F32),
        compiler_params=_cparams(("arbitrary",)),
        name="experts",
    )(block_ea, block_eb, n_used, *args)


def _combine_body(dest_ref, dest_next_ref, x1_ref, ys_ref, o_ref, y_buf, sems):
    n_tok = x1_ref.shape[0]
    step = pl.program_id(0)
    slot = step % 2

    def gather(idx_ref, to_slot):
        def start(t, c):
            pltpu.make_async_copy(ys_ref.at[pl.ds(idx_ref[t], 1)],
                                  y_buf.at[to_slot, pl.ds(t, 1)], sems.at[to_slot]).start()
            return c
        lax.fori_loop(0, n_tok, start, 0, unroll=16)

    @pl.when(step == 0)
    def _():
        gather(dest_ref, 0)

    @pl.when(step + 1 < pl.num_programs(0))
    def _():
        gather(dest_next_ref, 1 - slot)

    pltpu.make_async_copy(ys_ref.at[pl.ds(0, n_tok)], y_buf.at[slot], sems.at[slot]).wait()
    o_ref[...] = x1_ref[...] + y_buf[slot]


def _combine_call(dest, x1, ys, tok_per_step):
    T = x1.shape[0]
    n_steps = T // tok_per_step
    return pl.pallas_call(
        _combine_body,
        grid=(n_steps,),
        in_specs=[
            pl.BlockSpec((tok_per_step,), lambda i: (i,), memory_space=pltpu.SMEM),
            pl.BlockSpec((tok_per_step,), lambda i: (jnp.minimum(i + 1, n_steps - 1),),
                         memory_space=pltpu.SMEM),
            pl.BlockSpec((tok_per_step, D_MODEL), lambda i: (i, 0)),
            pl.BlockSpec(memory_space=pl.ANY),
        ],
        out_specs=pl.BlockSpec((tok_per_step, D_MODEL), lambda i: (i, 0)),
        out_shape=jax.ShapeDtypeStruct((T, D_MODEL), F32),
        scratch_shapes=[pltpu.VMEM((2, tok_per_step, D_MODEL), F32),
                        pltpu.SemaphoreType.DMA((2,))],
        compiler_params=_cparams(("arbitrary",)),
        name="combine",
    )(dest, dest, x1, ys)


def _layer(x, attn_norm_g, w_in, mla_q_lat_g, w_q_up, mla_kv_lat_g, w_kv_up, mla_q_head_g,
           mla_k_head_g, moba_q_head_g, moba_k_head_g, out_norm_mla_g, out_norm_moba_g, w_out,
           ffn_norm_g, w_router_group, b_router_group, w_router_expert, b_router_expert,
           w_gate, w_up, w_down):
    B, S, D = x.shape
    T = B * S
    x2 = x.reshape(T, D)

    win, wq, wkv, gains = _prep_weights(w_in, w_q_up, w_kv_up, mla_q_head_g, mla_k_head_g,
                                        moba_q_head_g, moba_k_head_g)
    glat = jnp.concatenate([mla_q_lat_g, mla_kv_lat_g])[None, :]
    qa, ka, va, qb, kb, vb, kmean = _proj_call(
        x2, attn_norm_g[None, :], win, glat, wq, wkv, gains, _group_matrices(), _rope_tables(S), S)

    oa = _attnT_call(False, qa, ka, va, None, B, S, "mla")
    ob = _attnT_call(True, qb, kb, vb, kmean.reshape(T // MOBA_BLOCK, 512), B, S, "moba")

    wr = jnp.zeros((LANES, D), F32).at[:N_EXPERTS].set(w_router_expert.T)
    wr = wr.at[N_EXPERTS:N_EXPERTS + N_GROUPS].set(w_router_group.T).astype(BF16)
    br = jnp.zeros((LANES, 1), F32).at[:N_EXPERTS, 0].set(b_router_expert)
    br = br.at[N_EXPERTS:N_EXPERTS + N_GROUPS, 0].set(b_router_group)
    tri = jnp.asarray(np.triu(np.ones((MERGE_CHUNK, MERGE_CHUNK), np.float32), 1), BF16)
    gout = jnp.concatenate([out_norm_mla_g, out_norm_moba_g])[None, :]
    x1, rows, meta, counts = _merge_call(oa, ob, x2, gout, w_out.astype(BF16), ffn_norm_g[None, :],
                                         wr, br, tri)

    counts = counts[:N_CLASSES, 0].astype(jnp.int32)
    padded = (counts + ROW_BLOCK - 1) // ROW_BLOCK * ROW_BLOCK
    pad_end = jnp.cumsum(padded)
    pad_start = pad_end - padded
    n_blocks = (T + ROW_BLOCK - 1) // ROW_BLOCK + N_CLASSES
    n_blocks = -(-n_blocks // EXPERT_BLOCKS_PER_STEP) * EXPERT_BLOCKS_PER_STEP
    n_used = (pad_end[-1] // ROW_BLOCK).astype(jnp.int32)
    blk = jnp.minimum(jnp.arange(n_blocks, dtype=jnp.int32), n_used - 1) * ROW_BLOCK
    block_class = jnp.minimum(jnp.sum(pad_end[None, :] <= blk[:, None], axis=1), N_CLASSES - 1)
    pairs = [(a, b) for a in range(EPG) for b in range(a + 1, EPG)]
    class_ea = np.array([g * EPG + a for g in range(N_GROUPS) for a, _ in pairs], np.int32)
    class_eb = np.array([g * EPG + b for g in range(N_GROUPS) for _, b in pairs], np.int32)
    block_ea = jnp.take(jnp.asarray(class_ea), block_class)
    block_eb = jnp.take(jnp.asarray(class_eb), block_class)
    is_class = meta[0][:, None] == jnp.arange(N_CLASSES, dtype=jnp.int32)[None, :]
    dest = (jnp.sum(jnp.where(is_class, pad_start[None, :], 0), axis=1) + meta[1]).astype(jnp.int32)

    xs = _dispatch_call(pad_end.astype(jnp.int32), n_used[None], dest, rows, n_blocks, 2048)
    ys = _expert_call(block_ea, block_eb, n_used[None], xs,
                      w_gate.astype(BF16), w_up.astype(BF16), w_down.astype(BF16))
    out = _combine_call(dest, x1, ys, 1024)
    return out.reshape(B, S, D)


def kernel(x, attn_norm_g, w_in, mla_q_lat_g, w_q_up, mla_kv_lat_g, w_kv_up, mla_q_head_g, mla_k_head_g, moba_q_head_g, moba_k_head_g, out_norm_mla_g, out_norm_moba_g, w_out, ffn_norm_g, w_router_group, b_router_group, w_router_expert, b_router_expert, w_gate, w_up, w_down):
    return _layer(x, attn_norm_g[0], w_in[0], mla_q_lat_g[0], w_q_up[0], mla_kv_lat_g[0], w_kv_up[0],
                  mla_q_head_g[0], mla_k_head_g[0], moba_q_head_g[0], moba_k_head_g[0],
                  out_norm_mla_g[0], out_norm_moba_g[0], w_out[0], ffn_norm_g[0],
                  w_router_group[0], b_router_group[0], w_router_expert[0], b_router_expert[0],
                  w_gate[0], w_up[0], w_down[0])
```

```python
import functools

import jax
import jax.numpy as jnp
import numpy as np
from jax import lax
from jax.experimental import pallas as pl
from jax.experimental.pallas import tpu as pltpu

F32 = jnp.float32
BF16 = jnp.bfloat16

D_MODEL = 1024
MLA_HEADS = 8
MLA_NOPE = 64
MLA_ROPE = 32
MLA_QK = MLA_NOPE + MLA_ROPE
MLA_V = 64
MLA_Q_LORA = 256
MLA_KV_LORA = 128
MOBA_HEADS = 8
MOBA_D = 64
MOBA_BLOCK = 256
MOBA_TOPK = 3
ROPE_THETA = 10000.0
EPS = 1e-6
N_GROUPS = 4
EPG = 8
N_EXPERTS = N_GROUPS * EPG
EXPERT_HIDDEN = 256
ROW_BLOCK = 256
PAIRS_PER_GROUP = EPG * (EPG - 1) // 2
N_CLASSES = N_GROUPS * PAIRS_PER_GROUP
ROW_WORDS = D_MODEL // 2 + 128

LANES = 128
HEAD_SLOT = LANES
PAIR = LANES
PROJ_WIDTH = 2048
MERGE_ROWS = 1024
MERGE_CHUNK = 256
PROJ_CHUNK = 256
PROJ_ROWS = 1024
NEG_INF = float("-inf")

VMEM_LIMIT = 48 * 1024 * 1024


def _cparams(sem):
    return pltpu.CompilerParams(dimension_semantics=sem, vmem_limit_bytes=VMEM_LIMIT)


def _mla_slot_index():
    idx = -np.ones((HEAD_SLOT,), np.int64)
    idx[0:32] = np.arange(0, 32)
    idx[32:48] = MLA_NOPE + np.arange(0, 16)
    idx[64:96] = np.arange(32, 64)
    idx[96:112] = MLA_NOPE + np.arange(16, 32)
    return idx


def _moba_pair_index():
    head = np.concatenate([np.zeros(32), np.ones(32), np.zeros(32), np.ones(32)]).astype(np.int64)
    feat = np.concatenate([np.arange(32), np.arange(32), 32 + np.arange(32), 32 + np.arange(32)])
    return head, feat


def _gather_cols(w, idx):
    safe = np.where(idx < 0, 0, idx)
    out = jnp.take(w, jnp.asarray(safe), axis=-1)
    return jnp.where(jnp.asarray(idx < 0), 0.0, out)


def _prep_weights(w_in, w_q_up, w_kv_up, q_head_g, k_head_g, mq_g, mk_g):
    slot = _mla_slot_index()
    head, feat = _moba_pair_index()

    kpe_idx = np.where(slot >= MLA_NOPE, slot - MLA_NOPE, -1)
    off_kpe = MLA_Q_LORA + MLA_KV_LORA
    off_mq = off_kpe + MLA_ROPE
    off_mk = off_mq + MOBA_HEADS * MOBA_D
    off_mv = off_mk + MOBA_HEADS * MOBA_D
    moba_idx = np.concatenate([(2 * p + head) * MOBA_D + feat for p in range(MOBA_HEADS // 2)])
    cols = np.concatenate([
        np.arange(0, off_kpe),
        np.where(kpe_idx < 0, -1, off_kpe + kpe_idx),
        off_mq + moba_idx,
        off_mk + moba_idx,
        off_mv + np.arange(MOBA_HEADS * MOBA_D),
    ])
    win = _gather_cols(w_in, cols).astype(BF16)

    q_idx = np.concatenate([np.where(slot < 0, -1, h * MLA_QK + slot) for h in range(MLA_HEADS)])
    wq = _gather_cols(w_q_up, q_idx).astype(BF16)

    nope_slot = np.where((slot >= 0) & (slot < MLA_NOPE), slot, -1)
    kn_idx = np.concatenate([np.where(nope_slot < 0, -1, h * (MLA_NOPE + MLA_V) + nope_slot)
                             for h in range(MLA_HEADS)])
    v_idx = np.concatenate([h * (MLA_NOPE + MLA_V) + MLA_NOPE + np.arange(MLA_V) for h in range(MLA_HEADS)])
    wkv = _gather_cols(w_kv_up, np.concatenate([kn_idx, v_idx])).astype(BF16)

    gq = _gather_cols(q_head_g, slot) * (MLA_QK ** -0.5 * LOG2E)
    gk = _gather_cols(k_head_g, slot)
    gmq = jnp.take(mq_g, jnp.asarray(feat)) * (MOBA_D ** -0.5 * LOG2E)
    gmk = jnp.take(mk_g, jnp.asarray(feat))
    gains = jnp.stack([gq, gk, gmq, gmk]).astype(F32)
    return win, wq, wkv, gains


def _rope_tables(seq):
    def tab(dim):
        inv = ROPE_THETA ** (-(jnp.arange(0, dim, 2, dtype=F32) / dim))
        ang = jnp.arange(seq, dtype=F32)[:, None] * inv[None, :]
        return jnp.cos(ang), jnp.sin(ang)

    cr, sr = tab(MLA_ROPE)
    cf, sf = tab(MOBA_D)
    one = jnp.ones((seq, 16), F32)
    zero = jnp.zeros((seq, 16), F32)
    cos_a = jnp.concatenate([one, one, cr, one, one, one, cr, one], axis=1)
    sin_a = jnp.concatenate([zero, zero, -sr, zero, zero, zero, sr, zero], axis=1)
    cos_b = jnp.concatenate([cf, cf, cf, cf], axis=1)
    sin_b = jnp.concatenate([-sf, -sf, sf, sf], axis=1)
    return jnp.stack([cos_a, sin_a, cos_b, sin_b])


def _group_matrices():
    lane = np.arange(256)
    g_mla = (lane[:, None] // HEAD_SLOT == lane[None, :] // HEAD_SLOT)
    head, _ = _moba_pair_index()
    hid = np.concatenate([head, 2 + head])
    g_moba = hid[:, None] == hid[None, :]
    return jnp.asarray(np.stack([g_mla, g_moba]), BF16)


def _rms(x, width):
    return lax.rsqrt(jnp.sum(x * x, axis=-1, keepdims=True) * (1.0 / width) + EPS)


def _rope(t, cos, sin):
    outs = []
    for c in range(0, t.shape[1], LANES):
        xc = t[:, c:c + LANES]
        outs.append(xc * cos + pltpu.roll(xc, 64, 1) * sin)
    return jnp.concatenate(outs, axis=1)


def _proj_body(x_ref, gx_ref, win_ref, glat_ref, wq_ref, wkv_ref, gains_ref, gmat_ref, rope_ref,
               qa_ref, ka_ref, va_ref, qb_ref, kb_ref, vb_ref, kmean_ref):
    n_chunks = x_ref.shape[0] // PROJ_CHUNK
    gains = gains_ref[...]
    g_mla = gmat_ref[0]
    g_moba = gmat_ref[1]

    def tile_gain(row, n):
        return jnp.concatenate([gains[row:row + 1, :]] * n, axis=1)

    def rows(c):
        return pl.ds(c * PROJ_CHUNK, PROJ_CHUNK)

    def group_sums(sq, gmat):
        return jnp.concatenate([jnp.dot(sq[:, c:c + 256], gmat, preferred_element_type=F32)
                                for c in range(0, sq.shape[1], 256)], axis=1)

    def finish(t, ssum, dim, gain, cos, sin):
        return _rope(t * lax.rsqrt(ssum * (1.0 / dim) + EPS) * gain, cos, sin)

    st = [dict() for _ in range(n_chunks)]

    def stage_a(c):
        x = x_ref[rows(c), :]
        st[c]["h"] = (x * _rms(x, D_MODEL) * gx_ref[...]).astype(BF16)

    def stage_b(c):
        st[c]["proj"] = jnp.dot(st[c].pop("h"), win_ref[...], preferred_element_type=F32)

    def stage_c(c):
        proj = st[c].pop("proj")
        q_lat = proj[:, 0:256]
        kv_lat = proj[:, 256:384]
        st[c]["ql"] = (q_lat * _rms(q_lat, MLA_Q_LORA) * glat_ref[:, 0:256]).astype(BF16)
        st[c]["kvl"] = (kv_lat * _rms(kv_lat, MLA_KV_LORA) * glat_ref[:, 256:384]).astype(BF16)
        st[c]["kpe"] = proj[:, 384:512]
        st[c]["mq"] = proj[:, 512:1024]
        st[c]["mk"] = proj[:, 1024:1536]
        st[c]["mq2"] = (st[c]["mq"] * st[c]["mq"]).astype(BF16)
        st[c]["mk2"] = (st[c]["mk"] * st[c]["mk"]).astype(BF16)
        vb_ref[rows(c), :] = proj[:, 1536:2048].astype(BF16)

    def stage_d(c):
        st[c]["qa"] = jnp.dot(st[c].pop("ql"), wq_ref[...], preferred_element_type=F32)
        st[c]["kv"] = jnp.dot(st[c].pop("kvl"), wkv_ref[...], preferred_element_type=F32)
        st[c]["mq_ss"] = group_sums(st[c].pop("mq2"), g_moba)
        st[c]["mk_ss"] = group_sums(st[c].pop("mk2"), g_moba)

    def stage_e(c):
        cos_b, sin_b = rope_ref[2, rows(c), :], rope_ref[3, rows(c), :]
        qb_ref[rows(c), :] = finish(st[c].pop("mq"), st[c].pop("mq_ss"), MOBA_D,
                                    tile_gain(2, 4), cos_b, sin_b).astype(BF16)
        kb = finish(st[c].pop("mk"), st[c].pop("mk_ss"), MOBA_D, tile_gain(3, 4), cos_b, sin_b)
        kb_ref[rows(c), :] = kb.astype(BF16)
        part = jnp.sum(kb, axis=0, keepdims=True) * (1.0 / MOBA_BLOCK)
        per_block = MOBA_BLOCK // PROJ_CHUNK
        if c % per_block == 0:
            kmean_ref[c // per_block] = part
        else:
            kmean_ref[c // per_block] = kmean_ref[c // per_block] + part
        kv = st[c].pop("kv")
        va_ref[rows(c), :] = kv[:, 1024:1536].astype(BF16)
        st[c]["ka"] = kv[:, 0:1024] + jnp.concatenate([st[c].pop("kpe")] * MLA_HEADS, axis=1)
        st[c]["qa2"] = (st[c]["qa"] * st[c]["qa"]).astype(BF16)
        st[c]["ka2"] = (st[c]["ka"] * st[c]["ka"]).astype(BF16)

    def stage_f(c):
        st[c]["qa_ss"] = group_sums(st[c].pop("qa2"), g_mla)
        st[c]["ka_ss"] = group_sums(st[c].pop("ka2"), g_mla)

    def stage_g(c):
        cos_a, sin_a = rope_ref[0, rows(c), :], rope_ref[1, rows(c), :]
        qa_ref[rows(c), :] = finish(st[c].pop("qa"), st[c].pop("qa_ss"), MLA_QK,
                                    tile_gain(0, 8), cos_a, sin_a).astype(BF16)
        ka_ref[rows(c), :] = finish(st[c].pop("ka"), st[c].pop("ka_ss"), MLA_QK,
                                    tile_gain(1, 8), cos_a, sin_a).astype(BF16)

    def run(stage, c):
        if 0 <= c < n_chunks:
            stage(c)

    stage_a(0)
    for s in range(n_chunks + 2):
        run(stage_f, s - 2)
        run(stage_c, s - 1)
        run(stage_d, s - 1)
        run(stage_b, s)
        run(stage_e, s - 1)
        run(stage_g, s - 2)
        run(stage_a, s + 1)


def _proj_call(x2, gx, win, glat, wq, wkv, gains, gmat, rope, seq):
    T = x2.shape[0]
    tm = PROJ_ROWS
    nt = T // tm
    spb = seq // tm
    bpt = tm // MOBA_BLOCK
    const = lambda *shape: pl.BlockSpec(shape, lambda i: (0,) * len(shape))
    rows = lambda w: pl.BlockSpec((tm, w), lambda i: (i, 0))
    return pl.pallas_call(
        _proj_body,
        grid=(nt,),
        in_specs=[
            rows(D_MODEL),
            const(1, D_MODEL),
            const(D_MODEL, PROJ_WIDTH),
            const(1, 384),
            const(MLA_Q_LORA, 1024),
            const(MLA_KV_LORA, 1536),
            const(4, LANES),
            const(2, 256, 256),
            pl.BlockSpec((4, tm, LANES), lambda i: (0, i % spb, 0)),
        ],
        out_specs=[rows(1024), rows(1024), rows(512), rows(512), rows(512), rows(512),
                   pl.BlockSpec((bpt, 1, 512), lambda i: (i, 0, 0))],
        out_shape=[
            jax.ShapeDtypeStruct((T, 1024), BF16),
            jax.ShapeDtypeStruct((T, 1024), BF16),
            jax.ShapeDtypeStruct((T, 512), BF16),
            jax.ShapeDtypeStruct((T, 512), BF16),
            jax.ShapeDtypeStruct((T, 512), BF16),
            jax.ShapeDtypeStruct((T, 512), BF16),
            jax.ShapeDtypeStruct((nt * bpt, 1, 512), F32),
        ],
        compiler_params=_cparams(("arbitrary",)),
        name="proj",
    )(x2, gx, win, glat, wq, wkv, gains, gmat, rope)


def _nt_dot(a, b):
    return lax.dot_general(a, b, (((1,), (1,)), ((), ())), preferred_element_type=F32)


ATT_BLOCK = MOBA_BLOCK
ATT_CHAINS = 4
MLA_PAIRS_PER_STEP = 2
MOBA_PAIRS_PER_STEP = 4
ATT_AHEAD = 2
M_INIT = -1e30
ONES_ROWS = 16
LOG2E = 1.4426950408889634


def _attnT_body(moba, q_ref, k_ref, v_ref, *rest):
    tb = ATT_BLOCK
    nq = q_ref.shape[0] // tb
    n_pairs = v_ref.shape[1] // PAIR
    qw = q_ref.shape[1] // n_pairs
    order = [(r, t) if t <= r else (nq - 1 - r, t - (r + 1))
             for t in range(nq + 1) for r in range(ATT_CHAINS)]
    pair_fns = [_attn_pair(moba, pp, qw, q_ref, k_ref, v_ref, rest) for pp in range(n_pairs)]
    for prep, _, _, _ in pair_fns:
        for i in range(nq):
            prep(i)
    stream = [(pp, i, j) for pp in range(n_pairs) for i, j in order]
    pending = [pair_fns[pp][1](i, j) for pp, i, j in stream[:ATT_AHEAD]]
    for n, (pp, i, j) in enumerate(stream):
        if n + ATT_AHEAD < len(stream):
            nxt = stream[n + ATT_AHEAD]
            pending.append(pair_fns[nxt[0]][1](nxt[1], nxt[2]))
        pair_fns[pp][2](i, j, pending.pop(0))
        if i == j:
            pair_fns[pp][3](i)


def _attn_pair(moba, pp, qw, q_all, k_all, v_all, rest):
    if moba:
        km_all, o_all, qt_all, vt_all, m_all, acc_all, sel_all = rest
        km_ref = km_all.at[:, pl.ds(pp * PAIR, PAIR)]
        sel_ref = sel_all.at[pp]
    else:
        o_all, qt_all, vt_all, m_all, acc_all = rest
    q_ref = q_all.at[:, pl.ds(pp * qw, qw)]
    k_ref = k_all.at[:, pl.ds(pp * qw, qw)]
    v_ref = v_all.at[:, pl.ds(pp * PAIR, PAIR)]
    o_ref = o_all.at[:, pl.ds(pp * PAIR, PAIR)]
    qt_ref, vt_ref, m_ref, acc_ref = qt_all.at[pp], vt_all.at[pp], m_all.at[pp], acc_all.at[pp]
    tb = ATT_BLOCK
    cols = 2 * tb
    nq = q_ref.shape[0] // tb
    lane_q = lax.broadcasted_iota(jnp.int32, (tb, qw), 1)

    if moba:
        km = km_ref[...]
        km_hi = km.astype(BF16)
        km_lo = (km - km_hi.astype(F32)).astype(BF16)
        zpad = jnp.zeros((16 - nq, PAIR), BF16)
        km_hi = jnp.concatenate([km_hi, zpad], axis=0)
        km_lo = jnp.concatenate([km_lo, zpad], axis=0)
        blk = lax.broadcasted_iota(jnp.int32, (16, cols), 0)

    def prep(i):
        start = i * tb
        q = q_ref[pl.ds(start, tb), :]
        if moba:
            head_of_lane = (lane_q // 32) % 2
            zero = jnp.zeros_like(q)
            per_head = [jnp.where(head_of_lane == h, q, zero) for h in range(2)]
        else:
            per_head = [q[:, h * HEAD_SLOT:(h + 1) * HEAD_SLOT] for h in range(2)]
        qts = [qh.astype(F32).T.astype(BF16) for qh in per_head]
        for h in range(2):
            qt_ref[i, h] = qts[h]
        vt = v_ref[pl.ds(start, tb), :].astype(F32).T.astype(BF16)
        vt_ref[i] = jnp.concatenate([vt, jnp.ones((ONES_ROWS, tb), BF16)], axis=0)
        m_ref[i] = jnp.full((1, cols), M_INIT, F32)
        acc_ref[i] = jnp.zeros((PAIR + ONES_ROWS, cols), F32)
        if moba:
            qt = jnp.concatenate(qts, axis=1)
            gate = (jnp.dot(km_hi, qt, preferred_element_type=F32)
                    + jnp.dot(km_lo, qt, preferred_element_type=F32))
            cnt = jnp.zeros((16, cols), F32)
            for b in range(i):
                gb = gate[b:b + 1, :]
                beats = (gb > gate) | ((gb == gate) & (blk > b))
                cnt = cnt + jnp.where(beats, 1.0, 0.0)
            picked = (blk < i) & (cnt < float(MOBA_TOPK))
            sel_ref[i] = jnp.where(picked, 1.0, 0.0)

    key = lax.broadcasted_iota(jnp.int32, (tb, cols), 0)
    qry = lax.broadcasted_iota(jnp.int32, (tb, cols), 1) % tb
    causal = key <= qry

    def scores(i, j):
        halves = []
        for h in range(2):
            lanes = slice(0, PAIR) if moba else slice(h * HEAD_SLOT, (h + 1) * HEAD_SLOT)
            halves.append(jnp.dot(k_ref[pl.ds(j * tb, tb), lanes], qt_ref[i, h],
                                  preferred_element_type=F32))
        return jnp.concatenate(halves, axis=1)

    def item(i, j, s):
        m_prev = m_ref[i]
        if i == j:
            s = jnp.where(causal, s, NEG_INF)
            m_new = jnp.maximum(m_prev, jnp.max(s, axis=0, keepdims=True))
            m_sub = m_new
        else:
            m_new = jnp.maximum(m_prev, jnp.max(s, axis=0, keepdims=True))
            m_sub = m_new
            if moba:
                chosen = sel_ref[i, pl.ds(j, 1), :] > 0.5
                m_new = jnp.where(chosen, m_new, m_prev)
                m_sub = jnp.where(chosen, m_new, -M_INIT)
        alpha = jnp.exp2(m_prev - m_new)
        p = jnp.exp2(s - m_sub).astype(BF16)
        m_ref[i] = m_new
        acc_ref[i] = alpha * acc_ref[i] + jnp.dot(vt_ref[j], p, preferred_element_type=F32)

    def finish(i):
        acc = acc_ref[i]
        o = acc[0:PAIR, :] / acc[PAIR:PAIR + 1, :]
        vrow = lax.broadcasted_iota(jnp.int32, (PAIR, tb), 0)
        ot = jnp.where(vrow < PAIR // 2, o[:, 0:tb], o[:, tb:cols])
        o_ref[pl.ds(i * tb, tb), :] = ot.T.astype(o_ref.dtype)

    return prep, scores, item, finish


def _attnT_call(moba, q, k, v, kmean, batch, seq, name):
    T, width = q.shape
    pairs = MLA_HEADS // 2
    qw = width // pairs
    nq = seq // ATT_BLOCK
    assert nq == 2 * ATT_CHAINS, "chain pairing (r, nq-1-r) needs nq == 2 * ATT_CHAINS"
    cols = 2 * ATT_BLOCK
    npp = MOBA_PAIRS_PER_STEP if moba else MLA_PAIRS_PER_STEP
    seq_block = lambda w: pl.BlockSpec((seq, npp * w), lambda b, p: (b, p))
    in_specs = [seq_block(qw), seq_block(qw), seq_block(PAIR)]
    scratch = [pltpu.VMEM((npp, nq, 2, PAIR, ATT_BLOCK), BF16),
               pltpu.VMEM((npp, nq, PAIR + ONES_ROWS, ATT_BLOCK), BF16),
               pltpu.VMEM((npp, nq, 1, cols), F32),
               pltpu.VMEM((npp, nq, PAIR + ONES_ROWS, cols), F32)]
    args = [q, k, v]
    if moba:
        in_specs.append(pl.BlockSpec((nq, npp * PAIR), lambda b, p: (b, p)))
        scratch.append(pltpu.VMEM((npp, nq, 16, cols), F32))
        args.append(kmean)
    return pl.pallas_call(
        functools.partial(_attnT_body, moba),
        grid=(batch, pairs // npp),
        in_specs=in_specs,
        out_specs=seq_block(PAIR),
        out_shape=jax.ShapeDtypeStruct((T, pairs * PAIR), BF16),
        scratch_shapes=scratch,
        compiler_params=_cparams(("arbitrary", "arbitrary")),
        name=name,
    )(*args)


def _merge_body(oa_ref, ob_ref, x_ref, gout_ref, wout_ref, gffn_ref, wr_ref, br_ref, tri_ref,
                x1_ref, row_ref, meta_ref, counts_ref):
    ck = tri_ref.shape[0]
    n_chunks = x_ref.shape[0] // ck

    @pl.when(pl.program_id(0) == 0)
    def _():
        counts_ref[...] = jnp.zeros_like(counts_ref)

    ridx = lax.broadcasted_iota(jnp.int32, (LANES, ck), 0)
    ridx_f = ridx.astype(F32)
    st = [dict() for _ in range(n_chunks)]

    def rows(c):
        return pl.ds(c * ck, ck)

    def first_max(vals):
        mx = jnp.max(vals, axis=0, keepdims=True)
        idx = jnp.min(jnp.where(vals == mx, ridx_f, float(LANES)), axis=0, keepdims=True)
        return mx, idx

    def stage_a(c):
        oa = oa_ref[rows(c), :].astype(F32)
        ob = ob_ref[rows(c), :].astype(F32)
        na = oa * _rms(oa, 512) * gout_ref[:, 0:512]
        nb = ob * _rms(ob, 512) * gout_ref[:, 512:1024]
        st[c]["mixed"] = jnp.concatenate([na, nb], axis=1).astype(BF16)

    def stage_b(c):
        st[c]["proj"] = jnp.dot(st[c].pop("mixed"), wout_ref[...], preferred_element_type=F32)

    def stage_c(c):
        x1 = x_ref[rows(c), :] + st[c].pop("proj")
        x1_ref[rows(c), :] = x1
        h2 = x1 * _rms(x1, D_MODEL) * gffn_ref[...]
        st[c]["h2b"] = h2.astype(BF16)
        half = D_MODEL // 2
        lo = pltpu.bitcast(h2[:, 0:half].astype(BF16).astype(F32), jnp.uint32)
        hi = pltpu.bitcast(h2[:, half:D_MODEL].astype(BF16).astype(F32), jnp.uint32)
        row_ref[rows(c), 0:half] = (hi & jnp.uint32(0xFFFF0000)) | (lo >> 16)

    def stage_d(c):
        st[c]["logits"] = _nt_dot(wr_ref[...], st[c].pop("h2b")) + br_ref[...]

    def stage_e(c):
        logits = st[c].pop("logits")
        is_group = (ridx >= N_EXPERTS) & (ridx < N_EXPERTS + N_GROUPS)
        gl = jnp.where(is_group, logits, NEG_INF)
        gmax, gidx = first_max(gl)
        g_w = 1.0 / jnp.sum(jnp.exp(gl - gmax), axis=0, keepdims=True)
        g_sel = gidx - float(N_EXPERTS)
        in_group = (ridx < N_EXPERTS) & ((ridx // EPG).astype(F32) == g_sel)
        el = jnp.where(in_group, logits, NEG_INF)
        v1, i1 = first_max(el)
        v2, i2 = first_max(jnp.where(ridx_f == i1, NEG_INF, el))
        t = jnp.exp(v2 - v1)
        w1 = g_w * (1.0 / (1.0 + t))
        w2 = g_w * (t / (1.0 + t))
        swap = i2 < i1
        la = jnp.where(swap, i2, i1) - EPG * g_sel
        lb = jnp.where(swap, i1, i2) - EPG * g_sel
        cls = PAIRS_PER_GROUP * g_sel + la * (2 * EPG - 1 - la) * 0.5 + (lb - la - 1.0)
        st[c]["cls"] = cls
        st[c]["onehot"] = ridx_f == cls
        wa = jnp.where(swap, w2, w1)
        wb = jnp.where(swap, w1, w2)
        wt = jnp.where(ridx == 0, wa, jnp.where(ridx == 1, wb, 0.0))
        row_ref[rows(c), D_MODEL // 2:ROW_WORDS] = pltpu.bitcast(wt.T, jnp.uint32)

    def stage_f(c):
        ones = jnp.where(st[c]["onehot"], 1.0, 0.0).astype(BF16)
        st[c]["ones"] = ones
        st[c]["before"] = jnp.dot(ones, tri_ref[...], preferred_element_type=F32)

    def stage_g(c):
        before = st[c].pop("before") + counts_ref[...][:, 0:1]
        pos = jnp.sum(jnp.where(st[c].pop("onehot"), before, 0.0), axis=0, keepdims=True)
        total = jnp.sum(st[c].pop("ones").astype(F32), axis=1, keepdims=True)
        counts_ref[...] = counts_ref[...] + total
        srow = lax.broadcasted_iota(jnp.int32, (8, ck), 0)
        meta = jnp.where(srow == 0, st[c].pop("cls"), jnp.where(srow == 1, pos, 0.0))
        meta_ref[:, rows(c)] = meta.astype(jnp.int32)

    def run(stage, c):
        if 0 <= c < n_chunks:
            stage(c)

    stage_a(0)
    for s in range(n_chunks + 2):
        run(stage_f, s - 2)
        run(stage_c, s - 1)
        run(stage_d, s - 1)
        run(stage_b, s)
        run(stage_e, s - 1)
        run(stage_g, s - 2)
        run(stage_a, s + 1)


def _merge_call(oa, ob, x2, gout, wout, gffn, wr, br, tri):
    T = x2.shape[0]
    tm = MERGE_ROWS
    ck = tri.shape[0]
    const = lambda *shape: pl.BlockSpec(shape, lambda i: (0,) * len(shape))
    rows = lambda w: pl.BlockSpec((tm, w), lambda i: (i, 0))
    return pl.pallas_call(
        _merge_body,
        grid=(T // tm,),
        in_specs=[rows(512), rows(512), rows(D_MODEL), const(1, D_MODEL), const(D_MODEL, D_MODEL),
                  const(1, D_MODEL), const(LANES, D_MODEL), const(LANES, 1), const(ck, ck)],
        out_specs=[rows(D_MODEL), rows(ROW_WORDS), pl.BlockSpec((8, tm), lambda i: (0, i)),
                   const(LANES, LANES)],
        out_shape=[
            jax.ShapeDtypeStruct((T, D_MODEL), F32),
            jax.ShapeDtypeStruct((T, ROW_WORDS), jnp.uint32),
            jax.ShapeDtypeStruct((8, T), jnp.int32),
            jax.ShapeDtypeStruct((LANES, LANES), F32),
        ],
        compiler_params=_cparams(("arbitrary",)),
        name="merge",
    )(oa, ob, x2, gout, wout, gffn, wr, br, tri)


def _dispatch_body(pad_end_ref, nused_ref, dest_ref, row_ref, xs_ref, zero_buf, sem, zero_sem):
    n_tok = row_ref.shape[0]
    n_blocks = xs_ref.shape[0] // ROW_BLOCK

    @pl.when(pl.program_id(0) == 0)
    def _():
        zero_buf[...] = jnp.zeros_like(zero_buf)

        def block_copy(first_row):
            return pltpu.make_async_copy(
                zero_buf, xs_ref.at[pl.ds(pl.multiple_of(first_row, ROW_BLOCK), ROW_BLOCK)], zero_sem)

        def nonempty(c):
            return pad_end_ref[c] > jnp.where(c > 0, pad_end_ref[jnp.maximum(c - 1, 0)], 0)

        def start_class(c, carry):
            @pl.when(nonempty(c))
            def _():
                block_copy(pad_end_ref[c] - ROW_BLOCK).start()
            return carry

        def wait_class(c, carry):
            @pl.when(nonempty(c))
            def _():
                block_copy(pad_end_ref[c] - ROW_BLOCK).wait()
            return carry

        def start_tail(b, carry):
            block_copy(b * ROW_BLOCK).start()
            return carry

        def wait_tail(b, carry):
            block_copy(b * ROW_BLOCK).wait()
            return carry

        lax.fori_loop(0, N_CLASSES, start_class, 0)
        lax.fori_loop(nused_ref[0], n_blocks, start_tail, 0)
        lax.fori_loop(0, N_CLASSES, wait_class, 0)
        lax.fori_loop(nused_ref[0], n_blocks, wait_tail, 0)

    def start(t, c):
        pltpu.make_async_copy(row_ref.at[pl.ds(t, 1)], xs_ref.at[pl.ds(dest_ref[t], 1)], sem).start()
        return c

    lax.fori_loop(0, n_tok, start, 0, unroll=16)
    pltpu.make_async_copy(row_ref, xs_ref.at[pl.ds(0, n_tok)], sem).wait()


def _dispatch_call(pad_end, n_used, dest, rows, n_blocks, tok_per_step):
    T, width = rows.shape
    grid_spec = pltpu.PrefetchScalarGridSpec(
        num_scalar_prefetch=2,
        grid=(T // tok_per_step,),
        in_specs=[
            pl.BlockSpec((tok_per_step,), lambda i, pe, nu: (i,), memory_space=pltpu.SMEM),
            pl.BlockSpec((tok_per_step, width), lambda i, pe, nu: (i, 0)),
        ],
        out_specs=pl.BlockSpec(memory_space=pl.ANY),
        scratch_shapes=[pltpu.VMEM((ROW_BLOCK, width), rows.dtype),
                        pltpu.SemaphoreType.DMA(()),
                        pltpu.SemaphoreType.DMA(())],
    )
    return pl.pallas_call(
        _dispatch_body,
        grid_spec=grid_spec,
        out_shape=jax.ShapeDtypeStruct((n_blocks * ROW_BLOCK, width), rows.dtype),
        compiler_params=_cparams(("arbitrary",)),
        name="dispatch",
    )(pad_end, n_used, dest, rows)


EXPERT_BLOCKS_PER_STEP = 4


def _expert_body(bea_ref, beb_ref, nused_ref, *refs):
    del bea_ref, beb_ref
    nb = EXPERT_BLOCKS_PER_STEP
    y_ref = refs[-1]
    first = pl.program_id(0) * nb
    half = D_MODEL // 2

    @pl.when(first < nused_ref[0])
    def _():
        st = [dict() for _ in range(nb)]

        def unpack(k):
            row = refs[7 * k][...]
            w = row[:, 0:half]
            lo = pltpu.bitcast(w << 16, F32)
            hi = pltpu.bitcast(w & jnp.uint32(0xFFFF0000), F32)
            st[k]["x"] = jnp.concatenate([lo, hi], axis=1).astype(BF16)
            st[k]["w"] = pltpu.bitcast(row[:, half:ROW_WORDS], F32)

        def up(k):
            x = st[k].pop("x")
            st[k]["au"] = [(jnp.dot(x, refs[7 * k + 1 + 3 * e][...], preferred_element_type=F32),
                            jnp.dot(x, refs[7 * k + 2 + 3 * e][...], preferred_element_type=F32))
                           for e in range(2)]

        def act(k):
            st[k]["act"] = [(a * (1.0 / (1.0 + jnp.exp(-a))) * u).astype(BF16)
                            for a, u in st[k].pop("au")]

        def down(k):
            st[k]["y"] = [jnp.dot(h, refs[7 * k + 3 + 3 * e][...], preferred_element_type=F32)
                          for e, h in enumerate(st[k].pop("act"))]

        def finish(k):
            ya, yb = st[k].pop("y")
            w = st[k].pop("w")
            y = w[:, 0:1] * ya + w[:, 1:2] * yb
            if k > 0:
                y = jnp.where(first + k < nused_ref[0], y, 0.0)
            y_ref[pl.ds(k * ROW_BLOCK, ROW_BLOCK), :] = y

        for k in range(nb):
            unpack(k)
        up(0)
        for k in range(nb):
            if k + 1 < nb:
                up(k + 1)
            act(k)
            down(k)
            if k > 0:
                finish(k - 1)
        finish(nb - 1)

    @pl.when(first >= nused_ref[0])
    def _():
        y_ref[...] = jnp.zeros_like(y_ref)


def _expert_call(block_ea, block_eb, n_used, xs, wg, wu, wd):
    P = xs.shape[0]
    nb = EXPERT_BLOCKS_PER_STEP
    n_steps = P // (ROW_BLOCK * nb)
    up = (None, D_MODEL, EXPERT_HIDDEN)
    down = (None, EXPERT_HIDDEN, D_MODEL)

    in_specs, args = [], []
    for k in range(nb):
        def blk(i, nu, k=k):
            return jnp.minimum(i * nb + k, nu[0] - 1)

        def row_map(i, bea, beb, nu, blk=blk):
            return (blk(i, nu), 0)

        def wa_map(i, bea, beb, nu, blk=blk):
            return (bea[blk(i, nu)], 0, 0)

        def wb_map(i, bea, beb, nu, blk=blk):
            return (beb[blk(i, nu)], 0, 0)

        in_specs += [pl.BlockSpec((ROW_BLOCK, ROW_WORDS), row_map),
                     pl.BlockSpec(up, wa_map), pl.BlockSpec(up, wa_map), pl.BlockSpec(down, wa_map),
                     pl.BlockSpec(up, wb_map), pl.BlockSpec(up, wb_map), pl.BlockSpec(down, wb_map)]
        args += [xs, wg, wu, wd, wg, wu, wd]

    grid_spec = pltpu.PrefetchScalarGridSpec(
        num_scalar_prefetch=3,
        grid=(n_steps,),
        in_specs=in_specs,
        out_specs=pl.BlockSpec((ROW_BLOCK * nb, D_MODEL), lambda i, bea, beb, nu: (i, 0)),
    )
    return pl.pallas_call(
        _expert_body,
        grid_spec=grid_spec,
        out_shape=jax.ShapeDtypeStruct((P, D_MODEL), F32),
        compiler_params=_cparams(("arbitrary",)),
        name="experts",
    )(block_ea, block_eb, n_used, *args)


def _combine_body(dest_ref, dest_next_ref, x1_ref, ys_ref, o_ref, y_buf, sems):
    n_tok = x1_ref.shape[0]
    step = pl.program_id(0)
    slot = step % 2

    def gather(idx_ref, to_slot):
        def start(t, c):
            pltpu.make_async_copy(ys_ref.at[pl.ds(idx_ref[t], 1)],
                                  y_buf.at[to_slot, pl.ds(t, 1)], sems.at[to_slot]).start()
            return c
        lax.fori_loop(0, n_tok, start, 0, unroll=16)

    @pl.when(step == 0)
    def _():
        gather(dest_ref, 0)

    @pl.when(step + 1 < pl.num_programs(0))
    def _():
        gather(dest_next_ref, 1 - slot)

    pltpu.make_async_copy(ys_ref.at[pl.ds(0, n_tok)], y_buf.at[slot], sems.at[slot]).wait()
    o_ref[...] = x1_ref[...] + y_buf[slot]


def _combine_call(dest, x1, ys, tok_per_step):
    T = x1.shape[0]
    n_steps = T // tok_per_step
    return pl.pallas_call(
        _combine_body,
        grid=(n_steps,),
        in_specs=[
            pl.BlockSpec((tok_per_step,), lambda i: (i,), memory_space=pltpu.SMEM),
            pl.BlockSpec((tok_per_step,), lambda i: (jnp.minimum(i + 1, n_steps - 1),),
                         memory_space=pltpu.SMEM),
            pl.BlockSpec((tok_per_step, D_MODEL), lambda i: (i, 0)),
            pl.BlockSpec(memory_space=pl.ANY),
        ],
        out_specs=pl.BlockSpec((tok_per_step, D_MODEL), lambda i: (i, 0)),
        out_shape=jax.ShapeDtypeStruct((T, D_MODEL), F32),
        scratch_shapes=[pltpu.VMEM((2, tok_per_step, D_MODEL), F32),
                        pltpu.SemaphoreType.DMA((2,))],
        compiler_params=_cparams(("arbitrary",)),
        name="combine",
    )(dest, dest, x1, ys)


def _layer(x, attn_norm_g, w_in, mla_q_lat_g, w_q_up, mla_kv_lat_g, w_kv_up, mla_q_head_g,
           mla_k_head_g, moba_q_head_g, moba_k_head_g, out_norm_mla_g, out_norm_moba_g, w_out,
           ffn_norm_g, w_router_group, b_router_group, w_router_expert, b_router_expert,
           w_gate, w_up, w_down):
    B, S, D = x.shape
    T = B * S
    x2 = x.reshape(T, D)

    win, wq, wkv, gains = _prep_weights(w_in, w_q_up, w_kv_up, mla_q_head_g, mla_k_head_g,
                                        moba_q_head_g, moba_k_head_g)
    glat = jnp.concatenate([mla_q_lat_g, mla_kv_lat_g])[None, :]
    qa, ka, va, qb, kb, vb, kmean = _proj_call(
        x2, attn_norm_g[None, :], win, glat, wq, wkv, gains, _group_matrices(), _rope_tables(S), S)

    oa = _attnT_call(False, qa, ka, va, None, B, S, "mla")
    ob = _attnT_call(True, qb, kb, vb, kmean.reshape(T // MOBA_BLOCK, 512), B, S, "moba")

    wr = jnp.zeros((LANES, D), F32).at[:N_EXPERTS].set(w_router_expert.T)
    wr = wr.at[N_EXPERTS:N_EXPERTS + N_GROUPS].set(w_router_group.T).astype(BF16)
    br = jnp.zeros((LANES, 1), F32).at[:N_EXPERTS, 0].set(b_router_expert)
    br = br.at[N_EXPERTS:N_EXPERTS + N_GROUPS, 0].set(b_router_group)
    tri = jnp.asarray(np.triu(np.ones((MERGE_CHUNK, MERGE_CHUNK), np.float32), 1), BF16)
    gout = jnp.concatenate([out_norm_mla_g, out_norm_moba_g])[None, :]
    x1, rows, meta, counts = _merge_call(oa, ob, x2, gout, w_out.astype(BF16), ffn_norm_g[None, :],
                                         wr, br, tri)

    counts = counts[:N_CLASSES, 0].astype(jnp.int32)
    padded = (counts + ROW_BLOCK - 1) // ROW_BLOCK * ROW_BLOCK
    pad_end = jnp.cumsum(padded)
    pad_start = pad_end - padded
    n_blocks = (T + ROW_BLOCK - 1) // ROW_BLOCK + N_CLASSES
    n_blocks = -(-n_blocks // EXPERT_BLOCKS_PER_STEP) * EXPERT_BLOCKS_PER_STEP
    n_used = (pad_end[-1] // ROW_BLOCK).astype(jnp.int32)
    blk = jnp.minimum(jnp.arange(n_blocks, dtype=jnp.int32), n_used - 1) * ROW_BLOCK
    block_class = jnp.minimum(jnp.sum(pad_end[None, :] <= blk[:, None], axis=1), N_CLASSES - 1)
    pairs = [(a, b) for a in range(EPG) for b in range(a + 1, EPG)]
    class_ea = np.array([g * EPG + a for g in range(N_GROUPS) for a, _ in pairs], np.int32)
    class_eb = np.array([g * EPG + b for g in range(N_GROUPS) for _, b in pairs], np.int32)
    block_ea = jnp.take(jnp.asarray(class_ea), block_class)
    block_eb = jnp.take(jnp.asarray(class_eb), block_class)
    is_class = meta[0][:, None] == jnp.arange(N_CLASSES, dtype=jnp.int32)[None, :]
    dest = (jnp.sum(jnp.where(is_class, pad_start[None, :], 0), axis=1) + meta[1]).astype(jnp.int32)

    xs = _dispatch_call(pad_end.astype(jnp.int32), n_used[None], dest, rows, n_blocks, 2048)
    ys = _expert_call(block_ea, block_eb, n_used[None], xs,
                      w_gate.astype(BF16), w_up.astype(BF16), w_down.astype(BF16))
    out = _combine_call(dest, x1, ys, 1024)
    return out.reshape(B, S, D)


def kernel(x, attn_norm_g, w_in, mla_q_lat_g, w_q_up, mla_kv_lat_g, w_kv_up, mla_q_head_g, mla_k_head_g, moba_q_head_g, moba_k_head_g, out_norm_mla_g, out_norm_moba_g, w_out, ffn_norm_g, w_router_group, b_router_group, w_router_expert, b_router_expert, w_gate, w_up, w_down):
    return _layer(x, attn_norm_g[0], w_in[0], mla_q_lat_g[0], w_q_up[0], mla_kv_lat_g[0], w_kv_up[0],
                  mla_q_head_g[0], mla_k_head_g[0], moba_q_head_g[0], moba_k_head_g[0],
                  out_norm_mla_g[0], out_norm_moba_g[0], w_out[0], ffn_norm_g[0],
                  w_router_group[0], b_router_group[0], w_router_expert[0], b_router_expert[0],
                  w_gate[0], w_up[0], w_down[0])
```

```python
import functools

import jax
import jax.numpy as jnp
import numpy as np
from jax import lax
from jax.experimental import pallas as pl
from jax.experimental.pallas import tpu as pltpu

F32 = jnp.float32
BF16 = jnp.bfloat16

D_MODEL = 1024
MLA_HEADS = 8
MLA_NOPE = 64
MLA_ROPE = 32
MLA_QK = MLA_NOPE + MLA_ROPE
MLA_V = 64
MLA_Q_LORA = 256
MLA_KV_LORA = 128
MOBA_HEADS = 8
MOBA_D = 64
MOBA_BLOCK = 256
MOBA_TOPK = 3
ROPE_THETA = 10000.0
EPS = 1e-6
N_GROUPS = 4
EPG = 8
N_EXPERTS = N_GROUPS * EPG
EXPERT_HIDDEN = 256
ROW_BLOCK = 256
PAIRS_PER_GROUP = EPG * (EPG - 1) // 2
N_CLASSES = N_GROUPS * PAIRS_PER_GROUP
ROW_WORDS = D_MODEL // 2 + 128

LANES = 128
HEAD_SLOT = LANES
PAIR = LANES
PROJ_WIDTH = 2048
MERGE_ROWS = 1024
MERGE_CHUNK = 256
PROJ_CHUNK = 256
PROJ_ROWS = 1024
NEG_INF = float("-inf")

VMEM_LIMIT = 48 * 1024 * 1024


def _cparams(sem):
    return pltpu.CompilerParams(dimension_semantics=sem, vmem_limit_bytes=VMEM_LIMIT)


def _mla_slot_index():
    idx = -np.ones((HEAD_SLOT,), np.int64)
    idx[0:32] = np.arange(0, 32)
    idx[32:48] = MLA_NOPE + np.arange(0, 16)
    idx[64:96] = np.arange(32, 64)
    idx[96:112] = MLA_NOPE + np.arange(16, 32)
    return idx


def _moba_pair_index():
    head = np.concatenate([np.zeros(32), np.ones(32), np.zeros(32), np.ones(32)]).astype(np.int64)
    feat = np.concatenate([np.arange(32), np.arange(32), 32 + np.arange(32), 32 + np.arange(32)])
    return head, feat


def _gather_cols(w, idx):
    safe = np.where(idx < 0, 0, idx)
    out = jnp.take(w, jnp.asarray(safe), axis=-1)
    return jnp.where(jnp.asarray(idx < 0), 0.0, out)


def _prep_weights(w_in, w_q_up, w_kv_up, q_head_g, k_head_g, mq_g, mk_g):
    slot = _mla_slot_index()
    head, feat = _moba_pair_index()

    kpe_idx = np.where(slot >= MLA_NOPE, slot - MLA_NOPE, -1)
    off_kpe = MLA_Q_LORA + MLA_KV_LORA
    off_mq = off_kpe + MLA_ROPE
    off_mk = off_mq + MOBA_HEADS * MOBA_D
    off_mv = off_mk + MOBA_HEADS * MOBA_D
    moba_idx = np.concatenate([(2 * p + head) * MOBA_D + feat for p in range(MOBA_HEADS // 2)])
    cols = np.concatenate([
        np.arange(0, off_kpe),
        np.where(kpe_idx < 0, -1, off_kpe + kpe_idx),
        off_mq + moba_idx,
        off_mk + moba_idx,
        off_mv + np.arange(MOBA_HEADS * MOBA_D),
    ])
    win = _gather_cols(w_in, cols).astype(BF16)

    q_idx = np.concatenate([np.where(slot < 0, -1, h * MLA_QK + slot) for h in range(MLA_HEADS)])
    wq = _gather_cols(w_q_up, q_idx).astype(BF16)

    nope_slot = np.where((slot >= 0) & (slot < MLA_NOPE), slot, -1)
    kn_idx = np.concatenate([np.where(nope_slot < 0, -1, h * (MLA_NOPE + MLA_V) + nope_slot)
                             for h in range(MLA_HEADS)])
    v_idx = np.concatenate([h * (MLA_NOPE + MLA_V) + MLA_NOPE + np.arange(MLA_V) for h in range(MLA_HEADS)])
    wkv = _gather_cols(w_kv_up, np.concatenate([kn_idx, v_idx])).astype(BF16)

    gq = _gather_cols(q_head_g, slot) * (MLA_QK ** -0.5 * LOG2E)
    gk = _gather_cols(k_head_g, slot)
    gmq = jnp.take(mq_g, jnp.asarray(feat)) * (MOBA_D ** -0.5 * LOG2E)
    gmk = jnp.take(mk_g, jnp.asarray(feat))
    gains = jnp.stack([gq, gk, gmq, gmk]).astype(F32)
    return win, wq, wkv, gains


def _rope_tables(seq):
    def tab(dim):
        inv = ROPE_THETA ** (-(jnp.arange(0, dim, 2, dtype=F32) / dim))
        ang = jnp.arange(seq, dtype=F32)[:, None] * inv[None, :]
        return jnp.cos(ang), jnp.sin(ang)

    cr, sr = tab(MLA_ROPE)
    cf, sf = tab(MOBA_D)
    one = jnp.ones((seq, 16), F32)
    zero = jnp.zeros((seq, 16), F32)
    cos_a = jnp.concatenate([one, one, cr, one, one, one, cr, one], axis=1)
    sin_a = jnp.concatenate([zero, zero, -sr, zero, zero, zero, sr, zero], axis=1)
    cos_b = jnp.concatenate([cf, cf, cf, cf], axis=1)
    sin_b = jnp.concatenate([-sf, -sf, sf, sf], axis=1)
    return jnp.stack([cos_a, sin_a, cos_b, sin_b])


def _group_matrices():
    lane = np.arange(256)
    g_mla = (lane[:, None] // HEAD_SLOT == lane[None, :] // HEAD_SLOT)
    head, _ = _moba_pair_index()
    hid = np.concatenate([head, 2 + head])
    g_moba = hid[:, None] == hid[None, :]
    return jnp.asarray(np.stack([g_mla, g_moba]), BF16)


def _rms(x, width):
    return lax.rsqrt(jnp.sum(x * x, axis=-1, keepdims=True) * (1.0 / width) + EPS)


def _rope(t, cos, sin):
    outs = []
    for c in range(0, t.shape[1], LANES):
        xc = t[:, c:c + LANES]
        outs.append(xc * cos + pltpu.roll(xc, 64, 1) * sin)
    return jnp.concatenate(outs, axis=1)


def _proj_body(x_ref, gx_ref, win_ref, glat_ref, wq_ref, wkv_ref, gains_ref, gmat_ref, rope_ref,
               qa_ref, ka_ref, va_ref, qb_ref, kb_ref, vb_ref, kmean_ref):
    n_chunks = x_ref.shape[0] // PROJ_CHUNK
    gains = gains_ref[...]
    g_mla = gmat_ref[0]
    g_moba = gmat_ref[1]

    def tile_gain(row, n):
        return jnp.concatenate([gains[row:row + 1, :]] * n, axis=1)

    def rows(c):
        return pl.ds(c * PROJ_CHUNK, PROJ_CHUNK)

    def group_sums(sq, gmat):
        return jnp.concatenate([jnp.dot(sq[:, c:c + 256], gmat, preferred_element_type=F32)
                                for c in range(0, sq.shape[1], 256)], axis=1)

    def finish(t, ssum, dim, gain, cos, sin):
        return _rope(t * lax.rsqrt(ssum * (1.0 / dim) + EPS) * gain, cos, sin)

    st = [dict() for _ in range(n_chunks)]

    def stage_a(c):
        x = x_ref[rows(c), :]
        st[c]["h"] = (x * _rms(x, D_MODEL) * gx_ref[...]).astype(BF16)

    def stage_b(c):
        st[c]["proj"] = jnp.dot(st[c].pop("h"), win_ref[...], preferred_element_type=F32)

    def stage_c(c):
        proj = st[c].pop("proj")
        q_lat = proj[:, 0:256]
        kv_lat = proj[:, 256:384]
        st[c]["ql"] = (q_lat * _rms(q_lat, MLA_Q_LORA) * glat_ref[:, 0:256]).astype(BF16)
        st[c]["kvl"] = (kv_lat * _rms(kv_lat, MLA_KV_LORA) * glat_ref[:, 256:384]).astype(BF16)
        st[c]["kpe"] = proj[:, 384:512]
        st[c]["mq"] = proj[:, 512:1024]
        st[c]["mk"] = proj[:, 1024:1536]
        st[c]["mq2"] = (st[c]["mq"] * st[c]["mq"]).astype(BF16)
        st[c]["mk2"] = (st[c]["mk"] * st[c]["mk"]).astype(BF16)
        vb_ref[rows(c), :] = proj[:, 1536:2048].astype(BF16)

    def stage_d(c):
        st[c]["qa"] = jnp.dot(st[c].pop("ql"), wq_ref[...], preferred_element_type=F32)
        st[c]["kv"] = jnp.dot(st[c].pop("kvl"), wkv_ref[...], preferred_element_type=F32)
        st[c]["mq_ss"] = group_sums(st[c].pop("mq2"), g_moba)
        st[c]["mk_ss"] = group_sums(st[c].pop("mk2"), g_moba)

    def stage_e(c):
        cos_b, sin_b = rope_ref[2, rows(c), :], rope_ref[3, rows(c), :]
        qb_ref[rows(c), :] = finish(st[c].pop("mq"), st[c].pop("mq_ss"), MOBA_D,
                                    tile_gain(2, 4), cos_b, sin_b).astype(BF16)
        kb = finish(st[c].pop("mk"), st[c].pop("mk_ss"), MOBA_D, tile_gain(3, 4), cos_b, sin_b)
        kb_ref[rows(c), :] = kb.astype(BF16)
        part = jnp.sum(kb, axis=0, keepdims=True) * (1.0 / MOBA_BLOCK)
        per_block = MOBA_BLOCK // PROJ_CHUNK
        if c % per_block == 0:
            kmean_ref[c // per_block] = part
        else:
            kmean_ref[c // per_block] = kmean_ref[c // per_block] + part
        kv = st[c].pop("kv")
        va_ref[rows(c), :] = kv[:, 1024:1536].astype(BF16)
        st[c]["ka"] = kv[:, 0:1024] + jnp.concatenate([st[c].pop("kpe")] * MLA_HEADS, axis=1)
        st[c]["qa2"] = (st[c]["qa"] * st[c]["qa"]).astype(BF16)
        st[c]["ka2"] = (st[c]["ka"] * st[c]["ka"]).astype(BF16)

    def stage_f(c):
        st[c]["qa_ss"] = group_sums(st[c].pop("qa2"), g_mla)
        st[c]["ka_ss"] = group_sums(st[c].pop("ka2"), g_mla)

    def stage_g(c):
        cos_a, sin_a = rope_ref[0, rows(c), :], rope_ref[1, rows(c), :]
        qa_ref[rows(c), :] = finish(st[c].pop("qa"), st[c].pop("qa_ss"), MLA_QK,
                                    tile_gain(0, 8), cos_a, sin_a).astype(BF16)
        ka_ref[rows(c), :] = finish(st[c].pop("ka"), st[c].pop("ka_ss"), MLA_QK,
                                    tile_gain(1, 8), cos_a, sin_a).astype(BF16)

    def run(stage, c):
        if 0 <= c < n_chunks:
            stage(c)

    stage_a(0)
    for s in range(n_chunks + 2):
        run(stage_f, s - 2)
        run(stage_c, s - 1)
        run(stage_d, s - 1)
        run(stage_b, s)
        run(stage_e, s - 1)
        run(stage_g, s - 2)
        run(stage_a, s + 1)


def _proj_call(x2, gx, win, glat, wq, wkv, gains, gmat, rope, seq):
    T = x2.shape[0]
    tm = PROJ_ROWS
    nt = T // tm
    spb = seq // tm
    bpt = tm // MOBA_BLOCK
    const = lambda *shape: pl.BlockSpec(shape, lambda i: (0,) * len(shape))
    rows = lambda w: pl.BlockSpec((tm, w), lambda i: (i, 0))
    return pl.pallas_call(
        _proj_body,
        grid=(nt,),
        in_specs=[
            rows(D_MODEL),
            const(1, D_MODEL),
            const(D_MODEL, PROJ_WIDTH),
            const(1, 384),
            const(MLA_Q_LORA, 1024),
            const(MLA_KV_LORA, 1536),
            const(4, LANES),
            const(2, 256, 256),
            pl.BlockSpec((4, tm, LANES), lambda i: (0, i % spb, 0)),
        ],
        out_specs=[rows(1024), rows(1024), rows(512), rows(512), rows(512), rows(512),
                   pl.BlockSpec((bpt, 1, 512), lambda i: (i, 0, 0))],
        out_shape=[
            jax.ShapeDtypeStruct((T, 1024), BF16),
            jax.ShapeDtypeStruct((T, 1024), BF16),
            jax.ShapeDtypeStruct((T, 512), BF16),
            jax.ShapeDtypeStruct((T, 512), BF16),
            jax.ShapeDtypeStruct((T, 512), BF16),
            jax.ShapeDtypeStruct((T, 512), BF16),
            jax.ShapeDtypeStruct((nt * bpt, 1, 512), F32),
        ],
        compiler_params=_cparams(("arbitrary",)),
        name="proj",
    )(x2, gx, win, glat, wq, wkv, gains, gmat, rope)


def _nt_dot(a, b):
    return lax.dot_general(a, b, (((1,), (1,)), ((), ())), preferred_element_type=F32)


ATT_BLOCK = MOBA_BLOCK
ATT_CHAINS = 4
MLA_PAIRS_PER_STEP = 2
MOBA_PAIRS_PER_STEP = 4
ATT_AHEAD = 2
M_INIT = -1e30
ONES_ROWS = 16
LOG2E = 1.4426950408889634


def _attnT_body(moba, q_ref, k_ref, v_ref, *rest):
    tb = ATT_BLOCK
    nq = q_ref.shape[0] // tb
    n_pairs = v_ref.shape[1] // PAIR
    qw = q_ref.shape[1] // n_pairs
    order = [(r, t) if t <= r else (nq - 1 - r, t - (r + 1))
             for t in range(nq + 1) for r in range(ATT_CHAINS)]
    pair_fns = [_attn_pair(moba, pp, qw, q_ref, k_ref, v_ref, rest) for pp in range(n_pairs)]
    for prep, _, _, _ in pair_fns:
        for i in range(nq):
            prep(i)
    stream = [(pp, i, j) for pp in range(n_pairs) for i, j in order]
    pending = [pair_fns[pp][1](i, j) for pp, i, j in stream[:ATT_AHEAD]]
    for n, (pp, i, j) in enumerate(stream):
        if n + ATT_AHEAD < len(stream):
            nxt = stream[n + ATT_AHEAD]
            pending.append(pair_fns[nxt[0]][1](nxt[1], nxt[2]))
        pair_fns[pp][2](i, j, pending.pop(0))
        if i == j:
            pair_fns[pp][3](i)


def _attn_pair(moba, pp, qw, q_all, k_all, v_all, rest):
    if moba:
        km_all, o_all, qt_all, vt_all, m_all, acc_all, sel_all = rest
        km_ref = km_all.at[:, pl.ds(pp * PAIR, PAIR)]
        sel_ref = sel_all.at[pp]
    else:
        o_all, qt_all, vt_all, m_all, acc_all = rest
    q_ref = q_all.at[:, pl.ds(pp * qw, qw)]
    k_ref = k_all.at[:, pl.ds(pp * qw, qw)]
    v_ref = v_all.at[:, pl.ds(pp * PAIR, PAIR)]
    o_ref = o_all.at[:, pl.ds(pp * PAIR, PAIR)]
    qt_ref, vt_ref, m_ref, acc_ref = qt_all.at[pp], vt_all.at[pp], m_all.at[pp], acc_all.at[pp]
    tb = ATT_BLOCK
    cols = 2 * tb
    nq = q_ref.shape[0] // tb
    lane_q = lax.broadcasted_iota(jnp.int32, (tb, qw), 1)

    if moba:
        km = km_ref[...]
        km_hi = km.astype(BF16)
        km_lo = (km - km_hi.astype(F32)).astype(BF16)
        zpad = jnp.zeros((16 - nq, PAIR), BF16)
        km_hi = jnp.concatenate([km_hi, zpad], axis=0)
        km_lo = jnp.concatenate([km_lo, zpad], axis=0)
        blk = lax.broadcasted_iota(jnp.int32, (16, cols), 0)

    def prep(i):
        start = i * tb
        q = q_ref[pl.ds(start, tb), :]
        if moba:
            head_of_lane = (lane_q // 32) % 2
            zero = jnp.zeros_like(q)
            per_head = [jnp.where(head_of_lane == h, q, zero) for h in range(2)]
        else:
            per_head = [q[:, h * HEAD_SLOT:(h + 1) * HEAD_SLOT] for h in range(2)]
        qts = [qh.astype(F32).T.astype(BF16) for qh in per_head]
        for h in range(2):
            qt_ref[i, h] = qts[h]
        vt = v_ref[pl.ds(start, tb), :].astype(F32).T.astype(BF16)
        vt_ref[i] = jnp.concatenate([vt, jnp.ones((ONES_ROWS, tb), BF16)], axis=0)
        m_ref[i] = jnp.full((1, cols), M_INIT, F32)
        acc_ref[i] = jnp.zeros((PAIR + ONES_ROWS, cols), F32)
        if moba:
            qt = jnp.concatenate(qts, axis=1)
            gate = (jnp.dot(km_hi, qt, preferred_element_type=F32)
                    + jnp.dot(km_lo, qt, preferred_element_type=F32))
            cnt = jnp.zeros((16, cols), F32)
            for b in range(i):
                gb = gate[b:b + 1, :]
                beats = (gb > gate) | ((gb == gate) & (blk > b))
                cnt = cnt + jnp.where(beats, 1.0, 0.0)
            picked = (blk < i) & (cnt < float(MOBA_TOPK))
            sel_ref[i] = jnp.where(picked, 1.0, 0.0)

    key = lax.broadcasted_iota(jnp.int32, (tb, cols), 0)
    qry = lax.broadcasted_iota(jnp.int32, (tb, cols), 1) % tb
    causal = key <= qry

    def scores(i, j):
        halves = []
        for h in range(2):
            lanes = slice(0, PAIR) if moba else slice(h * HEAD_SLOT, (h + 1) * HEAD_SLOT)
            halves.append(jnp.dot(k_ref[pl.ds(j * tb, tb), lanes], qt_ref[i, h],
                                  preferred_element_type=F32))
        return jnp.concatenate(halves, axis=1)

    def item(i, j, s):
        m_prev = m_ref[i]
        if i == j:
            s = jnp.where(causal, s, NEG_INF)
            m_new = jnp.maximum(m_prev, jnp.max(s, axis=0, keepdims=True))
            m_sub = m_new
        else:
            m_new = jnp.maximum(m_prev, jnp.max(s, axis=0, keepdims=True))
            m_sub = m_new
            if moba:
                chosen = sel_ref[i, pl.ds(j, 1), :] > 0.5
                m_new = jnp.where(chosen, m_new, m_prev)
                m_sub = jnp.where(chosen, m_new, -M_INIT)
        alpha = jnp.exp2(m_prev - m_new)
        p = jnp.exp2(s - m_sub).astype(BF16)
        m_ref[i] = m_new
        acc_ref[i] = alpha * acc_ref[i] + jnp.dot(vt_ref[j], p, preferred_element_type=F32)

    def finish(i):
        acc = acc_ref[i]
        o = acc[0:PAIR, :] / acc[PAIR:PAIR + 1, :]
        vrow = lax.broadcasted_iota(jnp.int32, (PAIR, tb), 0)
        ot = jnp.where(vrow < PAIR // 2, o[:, 0:tb], o[:, tb:cols])
        o_ref[pl.ds(i * tb, tb), :] = ot.T.astype(o_ref.dtype)

    return prep, scores, item, finish


def _attnT_call(moba, q, k, v, kmean, batch, seq, name):
    T, width = q.shape
    pairs = MLA_HEADS // 2
    qw = width // pairs
    nq = seq // ATT_BLOCK
    assert nq == 2 * ATT_CHAINS, "chain pairing (r, nq-1-r) needs nq == 2 * ATT_CHAINS"
    cols = 2 * ATT_BLOCK
    npp = MOBA_PAIRS_PER_STEP if moba else MLA_PAIRS_PER_STEP
    seq_block = lambda w: pl.BlockSpec((seq, npp * w), lambda b, p: (b, p))
    in_specs = [seq_block(qw), seq_block(qw), seq_block(PAIR)]
    scratch = [pltpu.VMEM((npp, nq, 2, PAIR, ATT_BLOCK), BF16),
               pltpu.VMEM((npp, nq, PAIR + ONES_ROWS, ATT_BLOCK), BF16),
               pltpu.VMEM((npp, nq, 1, cols), F32),
               pltpu.VMEM((npp, nq, PAIR + ONES_ROWS, cols), F32)]
    args = [q, k, v]
    if moba:
        in_specs.append(pl.BlockSpec((nq, npp * PAIR), lambda b, p: (b, p)))
        scratch.append(pltpu.VMEM((npp, nq, 16, cols), F32))
        args.append(kmean)
    return pl.pallas_call(
        functools.partial(_attnT_body, moba),
        grid=(batch, pairs // npp),
        in_specs=in_specs,
        out_specs=seq_block(PAIR),
        out_shape=jax.ShapeDtypeStruct((T, pairs * PAIR), BF16),
        scratch_shapes=scratch,
        compiler_params=_cparams(("arbitrary", "arbitrary")),
        name=name,
    )(*args)


def _merge_body(oa_ref, ob_ref, x_ref, gout_ref, wout_ref, gffn_ref, wr_ref, br_ref, tri_ref,
                x1_ref, row_ref, meta_ref, counts_ref):
    ck = tri_ref.shape[0]
    n_chunks = x_ref.shape[0] // ck

    @pl.when(pl.program_id(0) == 0)
    def _():
        counts_ref[...] = jnp.zeros_like(counts_ref)

    ridx = lax.broadcasted_iota(jnp.int32, (LANES, ck), 0)
    ridx_f = ridx.astype(F32)
    st = [dict() for _ in range(n_chunks)]

    def rows(c):
        return pl.ds(c * ck, ck)

    def first_max(vals):
        mx = jnp.max(vals, axis=0, keepdims=True)
        idx = jnp.min(jnp.where(vals == mx, ridx_f, float(LANES)), axis=0, keepdims=True)
        return mx, idx

    def stage_a(c):
        oa = oa_ref[rows(c), :].astype(F32)
        ob = ob_ref[rows(c), :].astype(F32)
        na = oa * _rms(oa, 512) * gout_ref[:, 0:512]
        nb = ob * _rms(ob, 512) * gout_ref[:, 512:1024]
        st[c]["mixed"] = jnp.concatenate([na, nb], axis=1).astype(BF16)

    def stage_b(c):
        st[c]["proj"] = jnp.dot(st[c].pop("mixed"), wout_ref[...], preferred_element_type=F32)

    def stage_c(c):
        x1 = x_ref[rows(c), :] + st[c].pop("proj")
        x1_ref[rows(c), :] = x1
        h2 = x1 * _rms(x1, D_MODEL) * gffn_ref[...]
        st[c]["h2b"] = h2.astype(BF16)
        half = D_MODEL // 2
        lo = pltpu.bitcast(h2[:, 0:half].astype(BF16).astype(F32), jnp.uint32)
        hi = pltpu.bitcast(h2[:, half:D_MODEL].astype(BF16).astype(F32), jnp.uint32)
        row_ref[rows(c), 0:half] = (hi & jnp.uint32(0xFFFF0000)) | (lo >> 16)

    def stage_d(c):
        st[c]["logits"] = _nt_dot(wr_ref[...], st[c].pop("h2b")) + br_ref[...]

    def stage_e(c):
        logits = st[c].pop("logits")
        is_group = (ridx >= N_EXPERTS) & (ridx < N_EXPERTS + N_GROUPS)
        gl = jnp.where(is_group, logits, NEG_INF)
        gmax, gidx = first_max(gl)
        g_w = 1.0 / jnp.sum(jnp.exp(gl - gmax), axis=0, keepdims=True)
        g_sel = gidx - float(N_EXPERTS)
        in_group = (ridx < N_EXPERTS) & ((ridx // EPG).astype(F32) == g_sel)
        el = jnp.where(in_group, logits, NEG_INF)
        v1, i1 = first_max(el)
        v2, i2 = first_max(jnp.where(ridx_f == i1, NEG_INF, el))
        t = jnp.exp(v2 - v1)
        w1 = g_w * (1.0 / (1.0 + t))
        w2 = g_w * (t / (1.0 + t))
        swap = i2 < i1
        la = jnp.where(swap, i2, i1) - EPG * g_sel
        lb = jnp.where(swap, i1, i2) - EPG * g_sel
        cls = PAIRS_PER_GROUP * g_sel + la * (2 * EPG - 1 - la) * 0.5 + (lb - la - 1.0)
        st[c]["cls"] = cls
        st[c]["onehot"] = ridx_f == cls
        wa = jnp.where(swap, w2, w1)
        wb = jnp.where(swap, w1, w2)
        wt = jnp.where(ridx == 0, wa, jnp.where(ridx == 1, wb, 0.0))
        row_ref[rows(c), D_MODEL // 2:ROW_WORDS] = pltpu.bitcast(wt.T, jnp.uint32)

    def stage_f(c):
        ones = jnp.where(st[c]["onehot"], 1.0, 0.0).astype(BF16)
        st[c]["ones"] = ones
        st[c]["before"] = jnp.dot(ones, tri_ref[...], preferred_element_type=F32)

    def stage_g(c):
        before = st[c].pop("before") + counts_ref[...][:, 0:1]
        pos = jnp.sum(jnp.where(st[c].pop("onehot"), before, 0.0), axis=0, keepdims=True)
        total = jnp.sum(st[c].pop("ones").astype(F32), axis=1, keepdims=True)
        counts_ref[...] = counts_ref[...] + total
        srow = lax.broadcasted_iota(jnp.int32, (8, ck), 0)
        meta = jnp.where(srow == 0, st[c].pop("cls"), jnp.where(srow == 1, pos, 0.0))
        meta_ref[:, rows(c)] = meta.astype(jnp.int32)

    def run(stage, c):
        if 0 <= c < n_chunks:
            stage(c)

    stage_a(0)
    for s in range(n_chunks + 2):
        run(stage_f, s - 2)
        run(stage_c, s - 1)
        run(stage_d, s - 1)
        run(stage_b, s)
        run(stage_e, s - 1)
        run(stage_g, s - 2)
        run(stage_a, s + 1)


def _merge_call(oa, ob, x2, gout, wout, gffn, wr, br, tri):
    T = x2.shape[0]
    tm = MERGE_ROWS
    ck = tri.shape[0]
    const = lambda *shape: pl.BlockSpec(shape, lambda i: (0,) * len(shape))
    rows = lambda w: pl.BlockSpec((tm, w), lambda i: (i, 0))
    return pl.pallas_call(
        _merge_body,
        grid=(T // tm,),
        in_specs=[rows(512), rows(512), rows(D_MODEL), const(1, D_MODEL), const(D_MODEL, D_MODEL),
                  const(1, D_MODEL), const(LANES, D_MODEL), const(LANES, 1), const(ck, ck)],
        out_specs=[rows(D_MODEL), rows(ROW_WORDS), pl.BlockSpec((8, tm), lambda i: (0, i)),
                   const(LANES, LANES)],
        out_shape=[
            jax.ShapeDtypeStruct((T, D_MODEL), F32),
            jax.ShapeDtypeStruct((T, ROW_WORDS), jnp.uint32),
            jax.ShapeDtypeStruct((8, T), jnp.int32),
            jax.ShapeDtypeStruct((LANES, LANES), F32),
        ],
        compiler_params=_cparams(("arbitrary",)),
        name="merge",
    )(oa, ob, x2, gout, wout, gffn, wr, br, tri)


def _dispatch_body(pad_end_ref, nused_ref, dest_ref, row_ref, xs_ref, zero_buf, sem, zero_sem):
    n_tok = row_ref.shape[0]
    n_blocks = xs_ref.shape[0] // ROW_BLOCK

    @pl.when(pl.program_id(0) == 0)
    def _():
        zero_buf[...] = jnp.zeros_like(zero_buf)

        def block_copy(first_row):
            return pltpu.make_async_copy(
                zero_buf, xs_ref.at[pl.ds(pl.multiple_of(first_row, ROW_BLOCK), ROW_BLOCK)], zero_sem)

        def nonempty(c):
            return pad_end_ref[c] > jnp.where(c > 0, pad_end_ref[jnp.maximum(c - 1, 0)], 0)

        def start_class(c, carry):
            @pl.when(nonempty(c))
            def _():
                block_copy(pad_end_ref[c] - ROW_BLOCK).start()
            return carry

        def wait_class(c, carry):
            @pl.when(nonempty(c))
            def _():
                block_copy(pad_end_ref[c] - ROW_BLOCK).wait()
            return carry

        def start_tail(b, carry):
            block_copy(b * ROW_BLOCK).start()
            return carry

        def wait_tail(b, carry):
            block_copy(b * ROW_BLOCK).wait()
            return carry

        lax.fori_loop(0, N_CLASSES, start_class, 0)
        lax.fori_loop(nused_ref[0], n_blocks, start_tail, 0)
        lax.fori_loop(0, N_CLASSES, wait_class, 0)
        lax.fori_loop(nused_ref[0], n_blocks, wait_tail, 0)

    def start(g, c):
        for prio in range(2):
            t = 2 * g + prio
            pltpu.make_async_copy(row_ref.at[pl.ds(t, 1)], xs_ref.at[pl.ds(dest_ref[t], 1)],
                                  sem).start(priority=prio)
        return c

    lax.fori_loop(0, n_tok // 2, start, 0, unroll=8)
    pltpu.make_async_copy(row_ref, xs_ref.at[pl.ds(0, n_tok)], sem).wait()


def _dispatch_call(pad_end, n_used, dest, rows, n_blocks, tok_per_step):
    T, width = rows.shape
    grid_spec = pltpu.PrefetchScalarGridSpec(
        num_scalar_prefetch=2,
        grid=(T // tok_per_step,),
        in_specs=[
            pl.BlockSpec((tok_per_step,), lambda i, pe, nu: (i,), memory_space=pltpu.SMEM),
            pl.BlockSpec((tok_per_step, width), lambda i, pe, nu: (i, 0)),
        ],
        out_specs=pl.BlockSpec(memory_space=pl.ANY),
        scratch_shapes=[pltpu.VMEM((ROW_BLOCK, width), rows.dtype),
                        pltpu.SemaphoreType.DMA(()),
                        pltpu.SemaphoreType.DMA(())],
    )
    return pl.pallas_call(
        _dispatch_body,
        grid_spec=grid_spec,
        out_shape=jax.ShapeDtypeStruct((n_blocks * ROW_BLOCK, width), rows.dtype),
        compiler_params=_cparams(("arbitrary",)),
        name="dispatch",
    )(pad_end, n_used, dest, rows)


EXPERT_BLOCKS_PER_STEP = 4


def _expert_body(bea_ref, beb_ref, nused_ref, *refs):
    del bea_ref, beb_ref
    nb = EXPERT_BLOCKS_PER_STEP
    y_ref = refs[-1]
    first = pl.program_id(0) * nb
    half = D_MODEL // 2

    @pl.when(first < nused_ref[0])
    def _():
        st = [dict() for _ in range(nb)]

        def unpack(k):
            row = refs[7 * k][...]
            w = row[:, 0:half]
            lo = pltpu.bitcast(w << 16, F32)
            hi = pltpu.bitcast(w & jnp.uint32(0xFFFF0000), F32)
            st[k]["x"] = jnp.concatenate([lo, hi], axis=1).astype(BF16)
            st[k]["w"] = pltpu.bitcast(row[:, half:ROW_WORDS], F32)

        def up(k):
            x = st[k].pop("x")
            st[k]["au"] = [(jnp.dot(x, refs[7 * k + 1 + 3 * e][...], preferred_element_type=F32),
                            jnp.dot(x, refs[7 * k + 2 + 3 * e][...], preferred_element_type=F32))
                           for e in range(2)]

        def act(k):
            st[k]["act"] = [(a * (1.0 / (1.0 + jnp.exp(-a))) * u).astype(BF16)
                            for a, u in st[k].pop("au")]

        def down(k):
            st[k]["y"] = [jnp.dot(h, refs[7 * k + 3 + 3 * e][...], preferred_element_type=F32)
                          for e, h in enumerate(st[k].pop("act"))]

        def finish(k):
            ya, yb = st[k].pop("y")
            w = st[k].pop("w")
            y = w[:, 0:1] * ya + w[:, 1:2] * yb
            if k > 0:
                y = jnp.where(first + k < nused_ref[0], y, 0.0)
            y_ref[pl.ds(k * ROW_BLOCK, ROW_BLOCK), :] = y

        for k in range(nb):
            unpack(k)
        up(0)
        for k in range(nb):
            if k + 1 < nb:
                up(k + 1)
            act(k)
            down(k)
            if k > 0:
                finish(k - 1)
        finish(nb - 1)

    @pl.when(first >= nused_ref[0])
    def _():
        y_ref[...] = jnp.zeros_like(y_ref)


def _expert_call(block_ea, block_eb, n_used, xs, wg, wu, wd):
    P = xs.shape[0]
    nb = EXPERT_BLOCKS_PER_STEP
    n_steps = P // (ROW_BLOCK * nb)
    up = (None, D_MODEL, EXPERT_HIDDEN)
    down = (None, EXPERT_HIDDEN, D_MODEL)

    in_specs, args = [], []
    for k in range(nb):
        def blk(i, nu, k=k):
            return jnp.minimum(i * nb + k, nu[0] - 1)

        def row_map(i, bea, beb, nu, blk=blk):
            return (blk(i, nu), 0)

        def wa_map(i, bea, beb, nu, blk=blk):
            return (bea[blk(i, nu)], 0, 0)

        def wb_map(i, bea, beb, nu, blk=blk):
            return (beb[blk(i, nu)], 0, 0)

        in_specs += [pl.BlockSpec((ROW_BLOCK, ROW_WORDS), row_map),
                     pl.BlockSpec(up, wa_map), pl.BlockSpec(up, wa_map), pl.BlockSpec(down, wa_map),
                     pl.BlockSpec(up, wb_map), pl.BlockSpec(up, wb_map), pl.BlockSpec(down, wb_map)]
        args += [xs, wg, wu, wd, wg, wu, wd]

    grid_spec = pltpu.PrefetchScalarGridSpec(
        num_scalar_prefetch=3,
        grid=(n_steps,),
        in_specs=in_specs,
        out_specs=pl.BlockSpec((ROW_BLOCK * nb, D_MODEL), lambda i, bea, beb, nu: (i, 0)),
    )
    return pl.pallas_call(
        _expert_body,
        grid_spec=grid_spec,
        out_shape=jax.ShapeDtypeStruct((P, D_MODEL), F32),
        compiler_params=_cparams(("arbitrary",)),
        name="experts",
    )(block_ea, block_eb, n_used, *args)


def _combine_body(dest_ref, dest_next_ref, x1_ref, ys_ref, o_ref, y_buf, sems):
    n_tok = x1_ref.shape[0]
    step = pl.program_id(0)
    slot = step % 2

    def gather(idx_ref, to_slot):
        def start(g, c):
            for prio in range(2):
                t = 2 * g + prio
                pltpu.make_async_copy(ys_ref.at[pl.ds(idx_ref[t], 1)],
                                      y_buf.at[to_slot, pl.ds(t, 1)],
                                      sems.at[to_slot]).start(priority=prio)
            return c
        lax.fori_loop(0, n_tok // 2, start, 0, unroll=8)

    @pl.when(step == 0)
    def _():
        gather(dest_ref, 0)

    @pl.when(step + 1 < pl.num_programs(0))
    def _():
        gather(dest_next_ref, 1 - slot)

    pltpu.make_async_copy(ys_ref.at[pl.ds(0, n_tok)], y_buf.at[slot], sems.at[slot]).wait()
    o_ref[...] = x1_ref[...] + y_buf[slot]


def _combine_call(dest, x1, ys, tok_per_step):
    T = x1.shape[0]
    n_steps = T // tok_per_step
    return pl.pallas_call(
        _combine_body,
        grid=(n_steps,),
        in_specs=[
            pl.BlockSpec((tok_per_step,), lambda i: (i,), memory_space=pltpu.SMEM),
            pl.BlockSpec((tok_per_step,), lambda i: (jnp.minimum(i + 1, n_steps - 1),),
                         memory_space=pltpu.SMEM),
            pl.BlockSpec((tok_per_step, D_MODEL), lambda i: (i, 0)),
            pl.BlockSpec(memory_space=pl.ANY),
        ],
        out_specs=pl.BlockSpec((tok_per_step, D_MODEL), lambda i: (i, 0)),
        out_shape=jax.ShapeDtypeStruct((T, D_MODEL), F32),
        scratch_shapes=[pltpu.VMEM((2, tok_per_step, D_MODEL), F32),
                        pltpu.SemaphoreType.DMA((2,))],
        compiler_params=_cparams(("arbitrary",)),
        name="combine",
    )(dest, dest, x1, ys)


def _layer(x, attn_norm_g, w_in, mla_q_lat_g, w_q_up, mla_kv_lat_g, w_kv_up, mla_q_head_g,
           mla_k_head_g, moba_q_head_g, moba_k_head_g, out_norm_mla_g, out_norm_moba_g, w_out,
           ffn_norm_g, w_router_group, b_router_group, w_router_expert, b_router_expert,
           w_gate, w_up, w_down):
    B, S, D = x.shape
    T = B * S
    x2 = x.reshape(T, D)

    win, wq, wkv, gains = _prep_weights(w_in, w_q_up, w_kv_up, mla_q_head_g, mla_k_head_g,
                                        moba_q_head_g, moba_k_head_g)
    glat = jnp.concatenate([mla_q_lat_g, mla_kv_lat_g])[None, :]
    qa, ka, va, qb, kb, vb, kmean = _proj_call(
        x2, attn_norm_g[None, :], win, glat, wq, wkv, gains, _group_matrices(), _rope_tables(S), S)

    oa = _attnT_call(False, qa, ka, va, None, B, S, "mla")
    ob = _attnT_call(True, qb, kb, vb, kmean.reshape(T // MOBA_BLOCK, 512), B, S, "moba")

    wr = jnp.zeros((LANES, D), F32).at[:N_EXPERTS].set(w_router_expert.T)
    wr = wr.at[N_EXPERTS:N_EXPERTS + N_GROUPS].set(w_router_group.T).astype(BF16)
    br = jnp.zeros((LANES, 1), F32).at[:N_EXPERTS, 0].set(b_router_expert)
    br = br.at[N_EXPERTS:N_EXPERTS + N_GROUPS, 0].set(b_router_group)
    tri = jnp.asarray(np.triu(np.ones((MERGE_CHUNK, MERGE_CHUNK), np.float32), 1), BF16)
    gout = jnp.concatenate([out_norm_mla_g, out_norm_moba_g])[None, :]
    x1, rows, meta, counts = _merge_call(oa, ob, x2, gout, w_out.astype(BF16), ffn_norm_g[None, :],
                                         wr, br, tri)

    counts = counts[:N_CLASSES, 0].astype(jnp.int32)
    padded = (counts + ROW_BLOCK - 1) // ROW_BLOCK * ROW_BLOCK
    pad_end = jnp.cumsum(padded)
    pad_start = pad_end - padded
    n_blocks = (T + ROW_BLOCK - 1) // ROW_BLOCK + N_CLASSES
    n_blocks = -(-n_blocks // EXPERT_BLOCKS_PER_STEP) * EXPERT_BLOCKS_PER_STEP
    n_used = (pad_end[-1] // ROW_BLOCK).astype(jnp.int32)
    blk = jnp.minimum(jnp.arange(n_blocks, dtype=jnp.int32), n_used - 1) * ROW_BLOCK
    block_class = jnp.minimum(jnp.sum(pad_end[None, :] <= blk[:, None], axis=1), N_CLASSES - 1)
    pairs = [(a, b) for a in range(EPG) for b in range(a + 1, EPG)]
    class_ea = np.array([g * EPG + a for g in range(N_GROUPS) for a, _ in pairs], np.int32)
    class_eb = np.array([g * EPG + b for g in range(N_GROUPS) for _, b in pairs], np.int32)
    block_ea = jnp.take(jnp.asarray(class_ea), block_class)
    block_eb = jnp.take(jnp.asarray(class_eb), block_class)
    is_class = meta[0][:, None] == jnp.arange(N_CLASSES, dtype=jnp.int32)[None, :]
    dest = (jnp.sum(jnp.where(is_class, pad_start[None, :], 0), axis=1) + meta[1]).astype(jnp.int32)

    xs = _dispatch_call(pad_end.astype(jnp.int32), n_used[None], dest, rows, n_blocks, 2048)
    ys = _expert_call(block_ea, block_eb, n_used[None], xs,
                      w_gate.astype(BF16), w_up.astype(BF16), w_down.astype(BF16))
    out = _combine_call(dest, x1, ys, 1024)
    return out.reshape(B, S, D)


def kernel(x, attn_norm_g, w_in, mla_q_lat_g, w_q_up, mla_kv_lat_g, w_kv_up, mla_q_head_g, mla_k_head_g, moba_q_head_g, moba_k_head_g, out_norm_mla_g, out_norm_moba_g, w_out, ffn_norm_g, w_router_group, b_router_group, w_router_expert, b_router_expert, w_gate, w_up, w_down):
    return _layer(x, attn_norm_g[0], w_in[0], mla_q_lat_g[0], w_q_up[0], mla_kv_lat_g[0], w_kv_up[0],
                  mla_q_head_g[0], mla_k_head_g[0], moba_q_head_g[0], moba_k_head_g[0],
                  out_norm_mla_g[0], out_norm_moba_g[0], w_out[0], ffn_norm_g[0],
                  w_router_group[0], b_router_group[0], w_router_expert[0], b_router_expert[0],
                  w_gate[0], w_up[0], w_down[0])
```
